```python
import math
import jax, jax.numpy as jnp
from jax import lax
import numpy as np

D_MODEL = 4096
BATCH = 4
SEQ = 4096
DEPTH = 4
DEC_BATCH = 16
DEC_SEQ = 32
PAST_LEN = 2048

CHUNK = 64
N_MEM = 256
EPS = 1e-6
SSM_D_INNER = D_MODEL // 2
SSM_HEAD_DIM = 64
SSM_HEADS = SSM_D_INNER // SSM_HEAD_DIM
SSM_GROUPS = 8
SSM_HPG = SSM_HEADS // SSM_GROUPS
SSM_STATE = 128
SSM_CONV = 4
SSM_XBC = SSM_D_INNER + 2 * SSM_GROUPS * SSM_STATE
DT_MIN = 0.001
DT_MAX = 0.1
POOL_WIDTH = D_MODEL // 4
POOL_WINDOWS = (2, 4, 8, 16)
POOL_GROUPS = 4
POOL_GROUP_DIM = POOL_WIDTH // POOL_GROUPS
POOL_HIST = 15
DIFF_HEADS = 4
DIFF_HEAD_DIM = 128
DIFF_WIDTH = DIFF_HEADS * 2 * DIFF_HEAD_DIM
DIFF_SCALE = 1.0 / math.sqrt(DIFF_HEAD_DIM)
Q_BLOCK = 128
N_BRANCH = 3
IN_SIZES = (SSM_D_INNER, SSM_XBC, SSM_HEADS, POOL_WIDTH, DIFF_WIDTH, DIFF_WIDTH, DIFF_WIDTH, N_BRANCH * D_MODEL)
N_IN = SSM_D_INNER + SSM_XBC + SSM_HEADS + POOL_WIDTH + 3 * DIFF_WIDTH + N_BRANCH * D_MODEL
XA_HEADS = 4
XA_HEAD_DIM = 256
XA_WIDTH = XA_HEADS * XA_HEAD_DIM
XA_SCALE = 1.0 / math.sqrt(XA_HEAD_DIM)
D_FF = 11008
FFN_CONV = 3

kernel_name = 'hybrid_ssd_pool_diffattn_stream_step'


def rmsnorm(x, g):
    xf = x.astype(jnp.float32)
    y = xf * lax.rsqrt(jnp.mean(xf * xf, axis=-1, keepdims=True) + EPS)
    return (y * g.astype(jnp.float32)).astype(x.dtype)


def split_last(x, sizes):
    out, off = [], 0
    for s in sizes:
        out.append(x[..., off:off + s])
        off += s
    return out


def causal_dwconv(u, hist, w, bias):
    k = w.shape[0]
    L = u.shape[1]
    ext = jnp.concatenate([hist.astype(u.dtype), u], axis=1)
    y = ext[:, k - 1:k - 1 + L] * w[k - 1]
    for i in range(k - 1):
        y = y + ext[:, i:i + L] * w[i]
    return y + bias, ext[:, L:]


def ssd_chunked(x, dt, a_neg, bm, cm, h0, q):
    f32 = jnp.float32
    b, L, g, r, pdim = x.shape
    n = bm.shape[-1]
    c = L // q
    xf = x.astype(f32).reshape(b, c, q, g, r, pdim)
    dtf = dt.reshape(b, c, q, g, r)
    bf = bm.astype(f32).reshape(b, c, q, g, n)
    cf = cm.astype(f32).reshape(b, c, q, g, n)
    acum = jnp.cumsum(dtf * a_neg, axis=2)
    seg = acum[:, :, :, None] - acum[:, :, None, :]
    lower = jnp.tril(jnp.ones((q, q), dtype=bool))[None, None, :, :, None, None]
    decay = jnp.exp(jnp.where(lower, seg, -jnp.inf))
    cb = jnp.einsum('bcign,bcjgn->bcijg', cf, bf)
    wmat = cb[..., None] * decay * dtf[:, :, None]
    y_diag = jnp.einsum('bcijgr,bcjgrp->bcigrp', wmat, xf)
    w_end = jnp.exp(acum[:, :, -1:] - acum) * dtf
    states = jnp.einsum('bcjgn,bcjgrp->bcgrpn', bf, xf * w_end[..., None])
    chunk_decay = jnp.exp(acum[:, :, -1])

    def step(hc, inp):
        s_c, d_c = inp
        return hc * d_c[..., None, None] + s_c, hc

    h_last, h_in = lax.scan(step, h0.astype(f32),
                            (jnp.moveaxis(states, 1, 0), jnp.moveaxis(chunk_decay, 1, 0)))
    h_in = jnp.moveaxis(h_in, 0, 1)
    y_off = jnp.einsum('bcign,bcgrpn->bcigrp', cf, h_in) * jnp.exp(acum)[..., None]
    return (y_diag + y_off).reshape(b, L, g, r, pdim), h_last


def multiscale_pool(u_ext, pos0, n_new):
    b, L, _ = u_ext.shape
    uf = u_ext.astype(jnp.float32)
    cs = jnp.concatenate([jnp.zeros((b, 1, POOL_WIDTH), jnp.float32), jnp.cumsum(uf, axis=1)], axis=1)
    pos = pos0 + jnp.arange(L)
    outs = []
    for gi, w in enumerate(POOL_WINDOWS):
        lo, hi = gi * POOL_GROUP_DIM, (gi + 1) * POOL_GROUP_DIM
        csg = cs[..., lo:hi]
        lag = jnp.concatenate([jnp.zeros((b, w, POOL_GROUP_DIM), jnp.float32), csg[:, :L + 1 - w]], axis=1)
        wsum = csg[:, 1:] - lag[:, 1:]
        cnt = jnp.minimum(pos + 1, w).astype(jnp.float32)[None, :, None]
        outs.append(wsum / cnt - uf[..., lo:hi])
    return jnp.concatenate(outs, axis=-1)[:, L - n_new:]


def diff_attention(q, k, v, lam, mask):
    s = jnp.einsum('bqhmd,bkhmd->bhmqk', q.astype(jnp.float32), k.astype(jnp.float32)) * DIFF_SCALE
    if mask is not None:
        s = jnp.where(mask, s, -jnp.inf)
    pr = jax.nn.softmax(s, axis=-1)
    attn = pr[:, :, 0] - lam * pr[:, :, 1]
    return jnp.einsum('bhqk,bkhe->bqhe', attn, v.astype(jnp.float32))


def diff_attention_prompt(q, k, v, lam):
    b, s = q.shape[:2]
    nb = s // Q_BLOCK
    qb = jnp.moveaxis(q.reshape(b, nb, Q_BLOCK, DIFF_HEADS, 2, DIFF_HEAD_DIM), 1, 0)
    k_chunk = jnp.arange(s) // CHUNK

    def blk(args):
        qi, i = args
        q_chunk = (i * Q_BLOCK + jnp.arange(Q_BLOCK)) // CHUNK
        return diff_attention(qi, k, v, lam, k_chunk[None, :] <= q_chunk[:, None])

    out = lax.map(blk, (qb, jnp.arange(nb)))
    return jnp.moveaxis(out, 0, 1).reshape(b, s, DIFF_HEADS, 2 * DIFF_HEAD_DIM)


def cross_attention(xn, mk, mv, wq, wo):
    b, L, _ = xn.shape
    q = (xn @ wq).reshape(b, L, XA_HEADS, XA_HEAD_DIM)
    s = jnp.einsum('bqhd,bkhd->bhqk', q.astype(jnp.float32), mk.astype(jnp.float32)) * XA_SCALE
    pr = jax.nn.softmax(s, axis=-1)
    o = jnp.einsum('bhqk,bkhd->bqhd', pr, mv.astype(jnp.float32))
    return o.reshape(b, L, XA_WIDTH).astype(xn.dtype) @ wo


def layer_forward(h, p, lam_init, mem_k, mem_v, ssm_h0, ssm_hist, pool_hist, k_past, v_past, ffn_hist, prompt):
    f32 = jnp.float32
    act = h.dtype
    b, L, _ = h.shape
    n = rmsnorm(h, p['norm_mix_pre'])
    z, xbc, dt_raw, u_pool, q, k, v, gates = split_last(n @ p['w_in'], IN_SIZES)

    xbc_c, ssm_hist_new = causal_dwconv(xbc, ssm_hist, p['ssm_conv_w'], p['ssm_conv_b'])
    xs, bm, cm = split_last(jax.nn.silu(xbc_c), (SSM_D_INNER, SSM_GROUPS * SSM_STATE, SSM_GROUPS * SSM_STATE))
    xs = xs.reshape(b, L, SSM_GROUPS, SSM_HPG, SSM_HEAD_DIM)
    bm = bm.reshape(b, L, SSM_GROUPS, SSM_STATE)
    cm = cm.reshape(b, L, SSM_GROUPS, SSM_STATE)
    dt = jax.nn.softplus(dt_raw.astype(f32) + p['ssm_dt_bias'].astype(f32)).reshape(b, L, SSM_GROUPS, SSM_HPG)
    a_neg = -jnp.exp(p['ssm_a_log'].astype(f32)).reshape(SSM_GROUPS, SSM_HPG)
    h0 = ssm_h0.reshape(b, SSM_GROUPS, SSM_HPG, SSM_HEAD_DIM, SSM_STATE)
    y, h_last = ssd_chunked(xs, dt, a_neg, bm, cm, h0, CHUNK if prompt else L)
    y = y + xs.astype(f32) * p['ssm_d'].astype(f32).reshape(SSM_GROUPS, SSM_HPG, 1)
    y = y.reshape(b, L, SSM_D_INNER) * jax.nn.silu(z.astype(f32))
    y = rmsnorm(y.reshape(b, L, SSM_GROUPS, SSM_D_INNER // SSM_GROUPS),
                p['ssm_norm'].reshape(SSM_GROUPS, SSM_D_INNER // SSM_GROUPS)).reshape(b, L, SSM_D_INNER)
    br_ssm = y.astype(act) @ p['w_br_ssm']

    if pool_hist is None:
        u_ext, pos0 = u_pool, 0
    else:
        u_ext, pos0 = jnp.concatenate([pool_hist.astype(u_pool.dtype), u_pool], axis=1), PAST_LEN - POOL_HIST
    pooled = multiscale_pool(u_ext, pos0, L).reshape(b, L, POOL_GROUPS, POOL_GROUP_DIM)
    pooled = jnp.einsum('blgc,gcd->blgd', pooled, p['pool_w'].astype(f32)).reshape(b, L, POOL_WIDTH)
    pooled = pooled * p['pool_scale'].astype(f32)
    br_pool = pooled.astype(act) @ p['w_br_pool']
    pool_hist_new = u_ext[:, -POOL_HIST:]

    q = q.reshape(b, L, DIFF_HEADS, 2, DIFF_HEAD_DIM)
    k = k.reshape(b, L, DIFF_HEADS, 2, DIFF_HEAD_DIM)
    v = v.reshape(b, L, DIFF_HEADS, 2 * DIFF_HEAD_DIM)
    lam = (jnp.exp(jnp.sum(p['diff_lq1'].astype(f32) * p['diff_lk1'].astype(f32)))
           - jnp.exp(jnp.sum(p['diff_lq2'].astype(f32) * p['diff_lk2'].astype(f32))) + lam_init)
    if prompt:
        o = diff_attention_prompt(q, k, v, lam)
    else:
        k_all = jnp.concatenate([k_past.astype(k.dtype), k], axis=1)
        v_all = jnp.concatenate([v_past.astype(v.dtype), v], axis=1)
        o = diff_attention(q, k_all, v_all, lam, None)
    o = rmsnorm(o, p['diff_subln']) * (1.0 - lam_init)
    br_diff = o.reshape(b, L, DIFF_WIDTH).astype(act) @ p['w_br_diff']

    g = jax.nn.sigmoid(gates.astype(f32)).reshape(b, L, N_BRANCH, D_MODEL)
    merged = g[:, :, 0] * br_ssm + g[:, :, 1] * br_pool + g[:, :, 2] * br_diff
    h = h + rmsnorm(merged.astype(act) @ p['w_o'], p['norm_mix_post'])

    n = rmsnorm(h, p['norm_xa_pre'])
    h = h + rmsnorm(cross_attention(n, mem_k, mem_v, p['xa_wq'], p['xa_wo']), p['norm_xa_post'])

    n = rmsnorm(h, p['norm_ffn_pre'])
    up, ffn_hist_new = causal_dwconv(n @ p['ffn_w_up'], ffn_hist, p['ffn_conv_w'], p['ffn_conv_b'])
    gate, val = split_last(up, (D_FF, D_FF))
    hid = jax.nn.gelu(gate, approximate=True) * val
    h = h + rmsnorm(hid @ p['ffn_w_down'], p['norm_ffn_post'])

    h_last = h_last.reshape(b, SSM_HEADS, SSM_HEAD_DIM, SSM_STATE).astype(ssm_h0.dtype)
    return h, (h_last, ssm_hist_new, pool_hist_new, k, v, ffn_hist_new)


def setup_inputs(seed: int = 0) -> dict:
    key = jax.random.key(seed)
    ks = iter(jax.random.split(key, 64))
    f32 = jnp.float32

    def nrm(shape, scale=1.0):
        return jax.random.normal(next(ks), shape, f32) * scale

    def gain(shape):
        return 1.0 + nrm(shape, 0.02)

    dt0 = jnp.exp(jax.random.uniform(next(ks), (DEPTH, SSM_HEADS), f32, math.log(DT_MIN), math.log(DT_MAX)))
    ssm_dt_bias = dt0 + jnp.log(-jnp.expm1(-dt0))
    ssm_a_log = jnp.log(jax.random.uniform(next(ks), (DEPTH, SSM_HEADS), f32, 1.0, 16.0))
    return {
        'x_prompt': nrm((BATCH, SEQ, D_MODEL)),
        'x_sample': nrm((DEC_BATCH, DEC_SEQ, D_MODEL)),
        'state_ssm': nrm((DEPTH, DEC_BATCH, SSM_HEADS, SSM_HEAD_DIM, SSM_STATE), 0.1),
        'state_ssm_conv': nrm((DEPTH, DEC_BATCH, SSM_CONV - 1, SSM_XBC)),
        'state_pool': nrm((DEPTH, DEC_BATCH, POOL_HIST, POOL_WIDTH)),
        'cache_diff_k': nrm((DEPTH, DEC_BATCH, PAST_LEN, DIFF_HEADS, 2, DIFF_HEAD_DIM)),
        'cache_diff_v': nrm((DEPTH, DEC_BATCH, PAST_LEN, DIFF_HEADS, 2 * DIFF_HEAD_DIM)),
        'cache_mem_k': nrm((DEPTH, DEC_BATCH, N_MEM, XA_HEADS, XA_HEAD_DIM)),
        'cache_mem_v': nrm((DEPTH, DEC_BATCH, N_MEM, XA_HEADS, XA_HEAD_DIM)),
        'state_ffn_conv': nrm((DEPTH, DEC_BATCH, FFN_CONV - 1, 2 * D_FF)),
        'mem_prompt': nrm((BATCH, N_MEM, D_MODEL)),
        'norm_mix_pre': gain((DEPTH, D_MODEL)),
        'norm_mix_post': gain((DEPTH, D_MODEL)),
        'w_in': nrm((DEPTH, D_MODEL, N_IN), D_MODEL ** -0.5),
        'ssm_conv_w': nrm((DEPTH, SSM_CONV, SSM_XBC), SSM_CONV ** -0.5),
        'ssm_conv_b': nrm((DEPTH, SSM_XBC), 0.01),
        'ssm_dt_bias': ssm_dt_bias,
        'ssm_a_log': ssm_a_log,
        'ssm_d': gain((DEPTH, SSM_HEADS)),
        'ssm_norm': gain((DEPTH, SSM_D_INNER)),
        'w_br_ssm': nrm((DEPTH, SSM_D_INNER, D_MODEL), SSM_D_INNER ** -0.5),
        'pool_w': nrm((DEPTH, POOL_GROUPS, POOL_GROUP_DIM, POOL_GROUP_DIM), POOL_GROUP_DIM ** -0.5),
        'pool_scale': gain((DEPTH, POOL_WIDTH)),
        'w_br_pool': nrm((DEPTH, POOL_WIDTH, D_MODEL), POOL_WIDTH ** -0.5),
        'diff_lq1': nrm((DEPTH, DIFF_HEAD_DIM), 0.1),
        'diff_lk1': nrm((DEPTH, DIFF_HEAD_DIM), 0.1),
        'diff_lq2': nrm((DEPTH, DIFF_HEAD_DIM), 0.1),
        'diff_lk2': nrm((DEPTH, DIFF_HEAD_DIM), 0.1),
        'diff_subln': gain((DEPTH, 2 * DIFF_HEAD_DIM)),
        'w_br_diff': nrm((DEPTH, DIFF_WIDTH, D_MODEL), DIFF_WIDTH ** -0.5),
        'w_o': nrm((DEPTH, D_MODEL, D_MODEL), D_MODEL ** -0.5),
        'norm_xa_pre': gain((DEPTH, D_MODEL)),
        'norm_xa_post': gain((DEPTH, D_MODEL)),
        'norm_mem': gain((DEPTH, D_MODEL)),
        'xa_wq': nrm((DEPTH, D_MODEL, XA_WIDTH), D_MODEL ** -0.5),
        'xa_wk': nrm((DEPTH, D_MODEL, XA_WIDTH), D_MODEL ** -0.5),
        'xa_wv': nrm((DEPTH, D_MODEL, XA_WIDTH), D_MODEL ** -0.5),
        'xa_wo': nrm((DEPTH, XA_WIDTH, D_MODEL), XA_WIDTH ** -0.5),
        'norm_ffn_pre': gain((DEPTH, D_MODEL)),
        'norm_ffn_post': gain((DEPTH, D_MODEL)),
        'ffn_w_up': nrm((DEPTH, D_MODEL, 2 * D_FF), D_MODEL ** -0.5),
        'ffn_conv_w': nrm((DEPTH, FFN_CONV, 2 * D_FF), FFN_CONV ** -0.5),
        'ffn_conv_b': nrm((DEPTH, 2 * D_FF), 0.01),
        'ffn_w_down': nrm((DEPTH, D_FF, D_MODEL), D_FF ** -0.5),
    }


def reference(x_prompt, x_sample, state_ssm, state_ssm_conv, state_pool, cache_diff_k, cache_diff_v,
              cache_mem_k, cache_mem_v, state_ffn_conv, mem_prompt,
              norm_mix_pre, norm_mix_post, w_in, ssm_conv_w, ssm_conv_b, ssm_dt_bias, ssm_a_log, ssm_d,
              ssm_norm, w_br_ssm, pool_w, pool_scale, w_br_pool, diff_lq1, diff_lk1, diff_lq2, diff_lk2,
              diff_subln, w_br_diff, w_o, norm_xa_pre, norm_xa_post, norm_mem, xa_wq, xa_wk, xa_wv, xa_wo,
              norm_ffn_pre, norm_ffn_post, ffn_w_up, ffn_conv_w, ffn_conv_b, ffn_w_down):
    b_p = x_prompt.shape[0]
    hp, hs = x_prompt, x_sample
    names = ('ssm_p', 'ssm_s', 'conv_p', 'conv_s', 'pool_p', 'pool_s', 'k_p', 'k_s', 'v_p', 'v_s',
             'mk_p', 'mv_p', 'ffn_p', 'ffn_s')
    out = {nm: [] for nm in names}
    for l in range(DEPTH):
        p = {
            'norm_mix_pre': norm_mix_pre[l], 'norm_mix_post': norm_mix_post[l], 'w_in': w_in[l],
            'ssm_conv_w': ssm_conv_w[l], 'ssm_conv_b': ssm_conv_b[l], 'ssm_dt_bias': ssm_dt_bias[l],
            'ssm_a_log': ssm_a_log[l], 'ssm_d': ssm_d[l], 'ssm_norm': ssm_norm[l], 'w_br_ssm': w_br_ssm[l],
            'pool_w': pool_w[l], 'pool_scale': pool_scale[l], 'w_br_pool': w_br_pool[l],
            'diff_lq1': diff_lq1[l], 'diff_lk1': diff_lk1[l], 'diff_lq2': diff_lq2[l], 'diff_lk2': diff_lk2[l],
            'diff_subln': diff_subln[l], 'w_br_diff': w_br_diff[l], 'w_o': w_o[l],
            'norm_xa_pre': norm_xa_pre[l], 'norm_xa_post': norm_xa_post[l], 'xa_wq': xa_wq[l], 'xa_wo': xa_wo[l],
            'norm_ffn_pre': norm_ffn_pre[l], 'norm_ffn_post': norm_ffn_post[l], 'ffn_w_up': ffn_w_up[l],
            'ffn_conv_w': ffn_conv_w[l], 'ffn_conv_b': ffn_conv_b[l], 'ffn_w_down': ffn_w_down[l],
        }
        lam_init = 0.8 - 0.6 * math.exp(-0.3 * l)
        mn = rmsnorm(mem_prompt, norm_mem[l])
        mk_p = (mn @ xa_wk[l]).reshape(b_p, N_MEM, XA_HEADS, XA_HEAD_DIM)
        mv_p = (mn @ xa_wv[l]).reshape(b_p, N_MEM, XA_HEADS, XA_HEAD_DIM)
        hp, st_p = layer_forward(
            hp, p, lam_init, mk_p, mv_p,
            jnp.zeros((b_p, SSM_HEADS, SSM_HEAD_DIM, SSM_STATE), jnp.float32),
            jnp.zeros((b_p, SSM_CONV - 1, SSM_XBC), x_prompt.dtype),
            None, None, None,
            jnp.zeros((b_p, FFN_CONV - 1, 2 * D_FF), x_prompt.dtype), True)
        hs, st_s = layer_forward(
            hs, p, lam_init, cache_mem_k[l], cache_mem_v[l], state_ssm[l], state_ssm_conv[l],
            state_pool[l], cache_diff_k[l], cache_diff_v[l], state_ffn_conv[l], False)
        out['ssm_p'].append(st_p[0]); out['ssm_s'].append(st_s[0])
        out['conv_p'].append(st_p[1]); out['conv_s'].append(st_s[1])
        out['pool_p'].append(st_p[2]); out['pool_s'].append(st_s[2])
        out['k_p'].append(st_p[3]); out['k_s'].append(st_s[3])
        out['v_p'].append(st_p[4]); out['v_s'].append(st_s[4])
        out['mk_p'].append(mk_p); out['mv_p'].append(mv_p)
        out['ffn_p'].append(st_p[5]); out['ffn_s'].append(st_s[5])
    return (hp, hs,
            jnp.stack(out['ssm_p']), jnp.stack(out['ssm_s']),
            jnp.stack(out['conv_p']), jnp.stack(out['conv_s']),
            jnp.stack(out['pool_p']), jnp.stack(out['pool_s']),
            jnp.stack(out['k_p']), jnp.stack(out['k_s']),
            jnp.stack(out['v_p']), jnp.stack(out['v_s']),
            jnp.stack(out['mk_p']), jnp.stack(out['mv_p']),
            jnp.stack(out['ffn_p']), jnp.stack(out['ffn_s']))
```

```python
import functools
import math

import jax
import jax.numpy as jnp
from jax import lax
from jax.experimental import pallas as pl
from jax.experimental.pallas import tpu as pltpu

F32 = jnp.float32
BF16 = jnp.bfloat16
EPS = 1e-6
CHUNK = 64
POOL_WINDOWS = (2, 4, 8, 16)
SUBLANES = 8
LANES = 128
VMEM_LIMIT_BYTES = 56 * 1024 * 1024
NT_DIMS = (((1,), (1,)), ((), ()))
TN_DIMS = (((0,), (0,)), ((), ()))


def _tile(n, pref, align):
    t = (min(pref, n) // align) * align
    while t >= align:
        if n % t == 0:
            return t
        t -= align
    return n


def _round_up(n, m):
    return (n + m - 1) // m * m


def _params(*sem):
    return pltpu.CompilerParams(dimension_semantics=sem, vmem_limit_bytes=VMEM_LIMIT_BYTES)


def _sigmoid(x):
    return 1.0 / (1.0 + jnp.exp(-x))


def _softplus(x):
    return jnp.maximum(x, 0.0) + jnp.log(1.0 + jnp.exp(-jnp.abs(x)))


def _gelu_tanh(x):
    return 0.5 * x * (1.0 + jnp.tanh(math.sqrt(2.0 / math.pi) * (x + 0.044715 * (x * x * x))))


def _rms(x, g):
    return x * lax.rsqrt(jnp.mean(x * x, axis=-1, keepdims=True) + EPS) * g


def _norm_mm_kernel(x_ref, g_ref, w_ref, o_ref, xn_ref, *, rows):
    @pl.when(pl.program_id(1) == 0)
    def _():
        def body(r, carry):
            sl = pl.ds(pl.multiple_of(r * rows, rows), rows)
            xn_ref[sl, :] = _rms(x_ref[sl, :], g_ref[...]).astype(BF16)
            return carry
        lax.fori_loop(0, x_ref.shape[0] // rows, body, 0)

    o_ref[...] = jnp.dot(xn_ref[...], w_ref[...], preferred_element_type=F32)


def norm_mm(x, g, w, *, tm_pref=512, tn_pref=1024):
    t, k = x.shape
    n = w.shape[1]
    tm = _tile(t, tm_pref, 16)
    tn = _tile(n, tn_pref, LANES)
    rows = _tile(tm, 16, 16)
    return pl.pallas_call(
        functools.partial(_norm_mm_kernel, rows=rows),
        grid=(t // tm, n // tn),
        in_specs=[pl.BlockSpec((tm, k), lambda i, j: (i, 0)),
                  pl.BlockSpec((1, k), lambda i, j: (0, 0)),
                  pl.BlockSpec((k, tn), lambda i, j: (0, j))],
        out_specs=pl.BlockSpec((tm, tn), lambda i, j: (i, j)),
        out_shape=jax.ShapeDtypeStruct((t, n), F32),
        scratch_shapes=[pltpu.VMEM((tm, k), BF16)],
        compiler_params=_params("parallel", "arbitrary"),
    )(x, g.reshape(1, k), w)


def _mm_post_kernel(x_ref, w_ref, res_ref, g_ref, o_ref, *, rows):
    kk = pl.program_id(1)
    part = jnp.dot(x_ref[...], w_ref[...], preferred_element_type=F32)

    @pl.when(kk == 0)
    def _():
        o_ref[...] = part

    @pl.when(kk > 0)
    def _():
        o_ref[...] += part

    @pl.when(kk == pl.num_programs(1) - 1)
    def _():
        def body(r, carry):
            sl = pl.ds(pl.multiple_of(r * rows, rows), rows)
            o_ref[sl, :] = res_ref[sl, :] + _rms(o_ref[sl, :], g_ref[...])
            return carry
        lax.fori_loop(0, o_ref.shape[0] // rows, body, 0)


def mm_post(x, w, res, g, *, tm_pref=512, tk_pref=512):
    t, k = x.shape
    d = w.shape[1]
    tm = _tile(t, tm_pref, 16)
    tk = _tile(k, tk_pref, LANES)
    rows = _tile(tm, 16, 8)
    return pl.pallas_call(
        functools.partial(_mm_post_kernel, rows=rows),
        grid=(t // tm, k // tk),
        in_specs=[pl.BlockSpec((tm, tk), lambda i, kk: (i, kk)),
                  pl.BlockSpec((tk, d), lambda i, kk: (kk, 0)),
                  pl.BlockSpec((tm, d), lambda i, kk: (i, 0)),
                  pl.BlockSpec((1, d), lambda i, kk: (0, 0))],
        out_specs=pl.BlockSpec((tm, d), lambda i, kk: (i, 0)),
        out_shape=jax.ShapeDtypeStruct((t, d), F32),
        compiler_params=_params("parallel", "arbitrary"),
    )(x, w, res, g.reshape(1, d))


def _merge_kernel(ys_ref, yp_ref, yd_ref, g0_ref, g1_ref, g2_ref, ws_ref, wp_ref, wd_ref, o_ref):
    acc = _sigmoid(g0_ref[...]) * jnp.dot(ys_ref[...], ws_ref[...], preferred_element_type=F32)
    acc += _sigmoid(g1_ref[...]) * jnp.dot(yp_ref[...], wp_ref[...], preferred_element_type=F32)
    acc += _sigmoid(g2_ref[...]) * jnp.dot(yd_ref[...], wd_ref[...], preferred_element_type=F32)
    o_ref[...] = acc.astype(BF16)


def branch_merge(y_ssm, y_pool, y_diff, proj, gate_col0, w_s, w_p, w_d, *, tm_pref=512, tn_pref=1024):
    t = y_ssm.shape[0]
    d = w_s.shape[1]
    tm = _tile(t, tm_pref, 16)
    tn = _tile(d, tn_pref, LANES)
    assert gate_col0 % tn == 0
    nj = d // tn
    gj = gate_col0 // tn

    def gate_spec(br):
        return pl.BlockSpec((tm, tn), lambda i, j: (i, gj + br * nj + j))

    def x_spec(kdim):
        return pl.BlockSpec((tm, kdim), lambda i, j: (i, 0))

    def w_spec(kdim):
        return pl.BlockSpec((kdim, tn), lambda i, j: (0, j))

    return pl.pallas_call(
        _merge_kernel,
        grid=(t // tm, nj),
        in_specs=[x_spec(y_ssm.shape[1]), x_spec(y_pool.shape[1]), x_spec(y_diff.shape[1]),
                  gate_spec(0), gate_spec(1), gate_spec(2),
                  w_spec(w_s.shape[0]), w_spec(w_p.shape[0]), w_spec(w_d.shape[0])],
        out_specs=pl.BlockSpec((tm, tn), lambda i, j: (i, j)),
        out_shape=jax.ShapeDtypeStruct((t, d), BF16),
        compiler_params=_params("parallel", "parallel"),
    )(y_ssm, y_pool, y_diff, proj, proj, proj, w_s, w_p, w_d)


def _causal_conv(ext_ref, u_ref, prev_ref, hist_ref, w_ref, b_ref, first, k, tl):
    ext_ref[0:SUBLANES, :] = jnp.where(first, hist_ref[...], prev_ref[...])
    ext_ref[SUBLANES:SUBLANES + tl, :] = u_ref[...]
    acc = u_ref[...] * w_ref[k - 1:k, :]
    for i in range(k - 1):
        acc = acc + ext_ref[pl.ds(SUBLANES - (k - 1) + i, tl), :] * w_ref[i:i + 1, :]
    return acc + b_ref[...]


def _conv_silu_kernel(u_ref, prev_ref, hist_ref, w_ref, b_ref, o_ref, ext_ref, *, k, tl):
    y = _causal_conv(ext_ref, u_ref, prev_ref, hist_ref, w_ref, b_ref, pl.program_id(1) == 0, k, tl)
    o_ref[...] = y * _sigmoid(y)


def _seq_specs(row0, seq_len, tl, tc, col_blk0):
    blk0 = row0 // tl
    per_seq = seq_len // tl
    sub0 = row0 // SUBLANES
    sub_per_seq = seq_len // SUBLANES
    sub_per_tile = tl // SUBLANES
    cur = pl.BlockSpec((tl, tc), lambda b, t, c: (blk0 + b * per_seq + t, col_blk0 + c))
    prev = pl.BlockSpec(
        (SUBLANES, tc),
        lambda b, t, c: (jnp.maximum(sub0 + b * sub_per_seq + t * sub_per_tile - 1, 0), col_blk0 + c))
    return cur, prev


def _pad_hist(hist, rows):
    return jnp.pad(hist, ((0, 0), (rows - hist.shape[1], 0), (0, 0)))


def conv_silu(proj, col0, width, row0, nb, seq_len, hist, w, bias, *, tl_pref=512, tc_pref=512):
    k = w.shape[0]
    tl = _tile(seq_len, tl_pref, SUBLANES)
    tc = _tile(width, tc_pref, LANES)
    assert row0 % tl == 0 and col0 % tc == 0
    cur, prev = _seq_specs(row0, seq_len, tl, tc, col0 // tc)
    return pl.pallas_call(
        functools.partial(_conv_silu_kernel, k=k, tl=tl),
        grid=(nb, seq_len // tl, width // tc),
        in_specs=[cur, prev,
                  pl.BlockSpec((None, SUBLANES, tc), lambda b, t, c: (b, 0, c)),
                  pl.BlockSpec((k, tc), lambda b, t, c: (0, c)),
                  pl.BlockSpec((1, tc), lambda b, t, c: (0, c))],
        out_specs=pl.BlockSpec((tl, tc), lambda b, t, c: (b * (seq_len // tl) + t, c)),
        out_shape=jax.ShapeDtypeStruct((nb * seq_len, width), F32),
        scratch_shapes=[pltpu.VMEM((SUBLANES + tl, tc), F32)],
        compiler_params=_params("parallel", "parallel", "parallel"),
    )(proj, proj, _pad_hist(hist, SUBLANES), w, bias.reshape(1, width))


def _conv_geglu_kernel(ug_ref, pg_ref, hg_ref, wg_ref, bg_ref, uv_ref, pv_ref, hv_ref, wv_ref, bv_ref,
                       o_ref, extg_ref, extv_ref, *, k, tl):
    first = pl.program_id(1) == 0
    gate = _causal_conv(extg_ref, ug_ref, pg_ref, hg_ref, wg_ref, bg_ref, first, k, tl)
    val = _causal_conv(extv_ref, uv_ref, pv_ref, hv_ref, wv_ref, bv_ref, first, k, tl)
    o_ref[...] = (_gelu_tanh(gate) * val).astype(BF16)


def conv_geglu(up, half, row0, nb, seq_len, hist, w, bias, *, tl_pref=512, tc_pref=1024):
    k = w.shape[0]
    tl = _tile(seq_len, tl_pref, 16)
    tc = _tile(half, tc_pref, LANES)
    assert row0 % tl == 0
    ncb = half // tc
    cur_g, prev_g = _seq_specs(row0, seq_len, tl, tc, 0)
    cur_v, prev_v = _seq_specs(row0, seq_len, tl, tc, ncb)
    hist8 = _pad_hist(hist, SUBLANES)
    bias2 = bias.reshape(1, 2 * half)

    def side(off):
        return [pl.BlockSpec((None, SUBLANES, tc), lambda b, t, c: (b, 0, off + c)),
                pl.BlockSpec((k, tc), lambda b, t, c: (0, off + c)),
                pl.BlockSpec((1, tc), lambda b, t, c: (0, off + c))]

    return pl.pallas_call(
        functools.partial(_conv_geglu_kernel, k=k, tl=tl),
        grid=(nb, seq_len // tl, ncb),
        in_specs=[cur_g, prev_g] + side(0) + [cur_v, prev_v] + side(ncb),
        out_specs=pl.BlockSpec((tl, tc), lambda b, t, c: (b * (seq_len // tl) + t, c)),
        out_shape=jax.ShapeDtypeStruct((nb * seq_len, half), BF16),
        scratch_shapes=[pltpu.VMEM((SUBLANES + tl, tc), F32), pltpu.VMEM((SUBLANES + tl, tc), F32)],
        compiler_params=_params("parallel", "parallel", "parallel"),
    )(up, up, hist8, w, bias2, up, up, hist8, w, bias2)


def _pool_kernel(u_ref, prev_ref, hist_ref, pw_ref, ps_ref, o_ref, ext_ref, *, tl, halo, pos_base, gd):
    t = pl.program_id(1)
    ext_ref[0:halo, :] = jnp.where(t == 0, hist_ref[...], prev_ref[...])
    ext_ref[halo:halo + tl, :] = u_ref[...]
    pos = pos_base + t * tl + lax.broadcasted_iota(jnp.int32, (tl, 1), 0)
    for gi, win in enumerate(POOL_WINDOWS):
        cols = slice(gi * gd, (gi + 1) * gd)
        cur = u_ref[:, cols]
        wsum = cur
        for i in range(1, win):
            wsum = wsum + ext_ref[pl.ds(halo - i, tl), cols]
        cnt = jnp.minimum(pos + 1, win).astype(F32)
        pooled = wsum / cnt - cur
        y = jnp.dot(pooled.astype(BF16), pw_ref[gi], preferred_element_type=F32) * ps_ref[:, cols]
        o_ref[:, cols] = y.astype(BF16)


def pool_mix(proj, col0, row0, nb, seq_len, hist, pos_base, pool_w, pool_scale, *, tl_pref=512):
    ng, gd, _ = pool_w.shape
    width = ng * gd
    halo = 2 * SUBLANES
    assert ng == len(POOL_WINDOWS) and max(POOL_WINDOWS) <= halo
    tl = _tile(seq_len, tl_pref, halo)
    assert row0 % tl == 0 and col0 % width == 0
    blk0, per_seq = row0 // tl, seq_len // tl
    h0, h_per_seq, h_per_tile = row0 // halo, seq_len // halo, tl // halo
    cb = col0 // width
    return pl.pallas_call(
        functools.partial(_pool_kernel, tl=tl, halo=halo, pos_base=pos_base, gd=gd),
        grid=(nb, per_seq),
        in_specs=[pl.BlockSpec((tl, width), lambda b, t: (blk0 + b * per_seq + t, cb)),
                  pl.BlockSpec((halo, width),
                               lambda b, t: (jnp.maximum(h0 + b * h_per_seq + t * h_per_tile - 1, 0), cb)),
                  pl.BlockSpec((None, halo, width), lambda b, t: (b, 0, 0)),
                  pl.BlockSpec((ng, gd, gd), lambda b, t: (0, 0, 0)),
                  pl.BlockSpec((1, width), lambda b, t: (0, 0))],
        out_specs=pl.BlockSpec((tl, width), lambda b, t: (b * per_seq + t, 0)),
        out_shape=jax.ShapeDtypeStruct((nb * seq_len, width), BF16),
        scratch_shapes=[pltpu.VMEM((halo + tl, width), F32)],
        compiler_params=_params("parallel", "parallel"),
    )(proj, proj, _pad_hist(hist, halo), pool_w, pool_scale.reshape(1, width))


def _ssd_kernel(xs_ref, b_ref, c_ref, z_ref, dtc_ref, dtr_ref, biasc_ref, biasr_ref, alogc_ref, alogr_ref,
                dskip_ref, nw_ref, h0_ref, y_ref, hl_ref, h_ref, *, q, hpg, p):
    ci = pl.program_id(2)

    @pl.when(ci == 0)
    def _():
        h_ref[...] = h0_ref[...]

    dt_c = _softplus(dtc_ref[...] + biasc_ref[...])
    dt_r = _softplus(dtr_ref[...] + biasr_ref[...])
    a_c = -jnp.exp(alogc_ref[...])
    a_r = -jnp.exp(alogr_ref[...])
    row = lax.broadcasted_iota(jnp.int32, (q, q), 0)
    col = lax.broadcasted_iota(jnp.int32, (q, q), 1)
    lower = row >= col
    acum_c = jnp.dot(lower.astype(F32), dt_c * a_c, precision=lax.Precision.HIGHEST,
                     preferred_element_type=F32)
    acum_r = jnp.dot(dt_r * a_r, (row <= col).astype(F32), precision=lax.Precision.HIGHEST,
                     preferred_element_type=F32)
    bm = b_ref[...].astype(BF16)
    cm = c_ref[...].astype(BF16)
    cb = lax.dot_general(cm, bm, NT_DIMS, preferred_element_type=F32)
    a_last = acum_c[q - 1:q, :]
    ys = []
    for r in range(hpg):
        x_r = xs_ref[:, r * p:(r + 1) * p]
        ac = acum_c[:, r:r + 1]
        ar = acum_r[r:r + 1, :]
        decay = jnp.exp(jnp.where(lower, ac - ar, -jnp.inf))
        wmat = cb * decay * dt_r[r:r + 1, :]
        y = jnp.dot(wmat.astype(BF16), x_r.astype(BF16), preferred_element_type=F32)
        h_r = h_ref[r]
        y_off = lax.dot_general(cm, h_r.astype(BF16), NT_DIMS, preferred_element_type=F32)
        y = y + y_off * jnp.exp(ac)
        al = a_last[:, r:r + 1]
        w_end = jnp.exp(al - ac) * dt_c[:, r:r + 1]
        s_new = lax.dot_general((x_r * w_end).astype(BF16), bm, TN_DIMS, preferred_element_type=F32)
        h_ref[r] = h_r * jnp.exp(al) + s_new
        ys.append(y + x_r * dskip_ref[:, r:r + 1])
    yg = jnp.concatenate(ys, axis=1)
    zt = z_ref[...]
    yg = yg * (zt * _sigmoid(zt))
    y_ref[...] = _rms(yg, nw_ref[...]).astype(BF16)

    @pl.when(ci == pl.num_programs(2) - 1)
    def _():
        hl_ref[...] = h_ref[...]


def ssd_mix(xbc, proj, z_col0, row0, nb, seq_len, dt_raw, dt_bias, a_log, d_skip, norm_w, h0, *,
            groups, n_state, q):
    heads, p = h0.shape[1], h0.shape[2]
    hpg = heads // groups
    gw = hpg * p
    d_inner = heads * p
    nc = seq_len // q
    assert seq_len % q == 0 and row0 % q == 0 and z_col0 % gw == 0 and d_inner % n_state == 0
    dt_g = dt_raw.reshape(nb, seq_len, groups, hpg)
    dt_col = dt_g.transpose(0, 2, 1, 3)
    dt_row = dt_g.transpose(0, 2, 3, 1)
    blk0 = row0 // q
    zc = z_col0 // gw
    bcol = d_inner // n_state

    def per_group(arr, shape):
        return arr.reshape((groups,) + shape), pl.BlockSpec((None,) + shape, lambda b, g, c: (g, 0, 0))

    biasc, biasc_spec = per_group(dt_bias, (1, hpg))
    biasr, biasr_spec = per_group(dt_bias, (hpg, 1))
    alogc, alogc_spec = per_group(a_log, (1, hpg))
    alogr, alogr_spec = per_group(a_log, (hpg, 1))
    dsk, dsk_spec = per_group(d_skip, (1, hpg))
    h_spec = pl.BlockSpec((None, hpg, p, n_state), lambda b, g, c: (b, g, 0, 0))
    return pl.pallas_call(
        functools.partial(_ssd_kernel, q=q, hpg=hpg, p=p),
        grid=(nb, groups, nc),
        in_specs=[pl.BlockSpec((q, gw), lambda b, g, c: (b * nc + c, g)),
                  pl.BlockSpec((q, n_state), lambda b, g, c: (b * nc + c, bcol + g)),
                  pl.BlockSpec((q, n_state), lambda b, g, c: (b * nc + c, bcol + groups + g)),
                  pl.BlockSpec((q, gw), lambda b, g, c: (blk0 + b * nc + c, zc + g)),
                  pl.BlockSpec((None, None, q, hpg), lambda b, g, c: (b, g, c, 0)),
                  pl.BlockSpec((None, None, hpg, q), lambda b, g, c: (b, g, 0, c)),
                  biasc_spec, biasr_spec, alogc_spec, alogr_spec, dsk_spec,
                  pl.BlockSpec((1, gw), lambda b, g, c: (0, g)),
                  h_spec],
        out_specs=[pl.BlockSpec((q, gw), lambda b, g, c: (b * nc + c, g)), h_spec],
        out_shape=[jax.ShapeDtypeStruct((nb * seq_len, d_inner), BF16),
                   jax.ShapeDtypeStruct(h0.shape, F32)],
        scratch_shapes=[pltpu.VMEM((hpg, p, n_state), F32)],
        compiler_params=_params("parallel", "parallel", "arbitrary"),
    )(xbc, xbc, xbc, proj, dt_col, dt_row, biasc, biasr, alogc, alogr, dsk,
      norm_w.reshape(1, d_inner), h0)


def _lambda(lq1_ref, lk1_ref, lq2_ref, lk2_ref, lam_init):
    s1 = jnp.sum(lq1_ref[...] * lk1_ref[...], axis=-1, keepdims=True)
    s2 = jnp.sum(lq2_ref[...] * lk2_ref[...], axis=-1, keepdims=True)
    return jnp.exp(s1) - jnp.exp(s2) + lam_init


def _diff_prompt_kernel(q_ref, k_ref, v_ref, lq1_ref, lk1_ref, lq2_ref, lk2_ref, sub_ref, o_ref, *,
                        tq, tk, d, scale, lam_init):
    iq = pl.program_id(2)
    lam = _lambda(lq1_ref, lk1_ref, lq2_ref, lk2_ref, lam_init)
    q_chunk = (iq * tq + lax.broadcasted_iota(jnp.int32, (tq, 1), 0)) // CHUNK
    qs = [(q_ref[:, m * d:(m + 1) * d] * scale).astype(BF16) for m in range(2)]
    n_kv = (iq * tq + tq + tk - 1) // tk

    def body(j, carry):
        koff = pl.multiple_of(j * tk, tk)
        vt = v_ref[pl.ds(koff, tk), :].astype(BF16)
        k_chunk = (j * tk + lax.broadcasted_iota(jnp.int32, (1, tk), 1)) // CHUNK
        visible = k_chunk <= q_chunk
        new = []
        for m in range(2):
            mx, den, acc = carry[m]
            kt = k_ref[pl.ds(koff, tk), m * d:(m + 1) * d].astype(BF16)
            s = lax.dot_general(qs[m], kt, NT_DIMS, preferred_element_type=F32)
            s = jnp.where(visible, s, -jnp.inf)
            mx_new = jnp.maximum(mx, jnp.max(s, axis=-1, keepdims=True))
            alpha = jnp.exp(mx - mx_new)
            pr = jnp.exp(s - mx_new)
            den = alpha * den + jnp.sum(pr, axis=-1, keepdims=True)
            acc = alpha * acc + jnp.dot(pr.astype(BF16), vt, preferred_element_type=F32)
            new.append((mx_new, den, acc))
        return tuple(new)

    init = tuple((jnp.full((tq, 1), -jnp.inf, F32), jnp.zeros((tq, 1), F32), jnp.zeros((tq, 2 * d), F32))
                 for _ in range(2))
    (_, den0, acc0), (_, den1, acc1) = lax.fori_loop(0, n_kv, body, init)
    o = acc0 / den0 - lam * (acc1 / den1)
    o_ref[...] = (_rms(o, sub_ref[...]) * (1.0 - lam_init)).astype(BF16)


def _lam_specs(d, nargs):
    fixed = lambda *_: (0, 0)
    del nargs
    return [pl.BlockSpec((1, d), fixed)] * 4 + [pl.BlockSpec((1, 2 * d), fixed)]


def diff_attn_prompt(proj, q_col0, k_col0, v_col0, nb, seq_len, heads, d, lam_vecs, subln, lam_init, *,
                     tq_pref=256):
    hw = 2 * d
    tq = _tile(seq_len, tq_pref, CHUNK)
    tk = tq
    nq = seq_len // tq
    assert q_col0 % hw == 0 and k_col0 % hw == 0 and v_col0 % hw == 0
    qc, kc, vc = q_col0 // hw, k_col0 // hw, v_col0 // hw
    return pl.pallas_call(
        functools.partial(_diff_prompt_kernel, tq=tq, tk=tk, d=d, scale=1.0 / math.sqrt(d), lam_init=lam_init),
        grid=(nb, heads, nq),
        in_specs=[pl.BlockSpec((tq, hw), lambda b, h, i: (b * nq + i, qc + h)),
                  pl.BlockSpec((seq_len, hw), lambda b, h, i: (b, kc + h)),
                  pl.BlockSpec((seq_len, hw), lambda b, h, i: (b, vc + h))] + _lam_specs(d, 3),
        out_specs=pl.BlockSpec((tq, hw), lambda b, h, i: (b * nq + i, h)),
        out_shape=jax.ShapeDtypeStruct((nb * seq_len, heads * hw), BF16),
        compiler_params=_params("parallel", "parallel", "parallel"),
    )(proj, proj, proj, *[v.reshape(1, d) for v in lam_vecs], subln.reshape(1, hw))


def _diff_sample_kernel(q_ref, kn_ref, vn_ref, kp_ref, vp_ref, lq1_ref, lk1_ref, lq2_ref, lk2_ref, sub_ref,
                        o_ref, *, d, scale, lam_init):
    lam = _lambda(lq1_ref, lk1_ref, lq2_ref, lk2_ref, lam_init)
    vp = vp_ref[...].astype(BF16)
    vn = vn_ref[...].astype(BF16)
    outs = []
    for m in range(2):
        cols = slice(m * d, (m + 1) * d)
        qm = (q_ref[:, cols] * scale).astype(BF16)
        s_p = lax.dot_general(qm, kp_ref[:, cols].astype(BF16), NT_DIMS, preferred_element_type=F32)
        s_n = lax.dot_general(qm, kn_ref[:, cols].astype(BF16), NT_DIMS, preferred_element_type=F32)
        mx = jnp.maximum(jnp.max(s_p, axis=-1, keepdims=True), jnp.max(s_n, axis=-1, keepdims=True))
        p_p = jnp.exp(s_p - mx)
        p_n = jnp.exp(s_n - mx)
        den = jnp.sum(p_p, axis=-1, keepdims=True) + jnp.sum(p_n, axis=-1, keepdims=True)
        acc = jnp.dot(p_p.astype(BF16), vp, preferred_element_type=F32)
        acc += jnp.dot(p_n.astype(BF16), vn, preferred_element_type=F32)
        outs.append(acc / den)
    o = outs[0] - lam * outs[1]
    o_ref[...] = (_rms(o, sub_ref[...]) * (1.0 - lam_init)).astype(BF16)


def diff_attn_sample(proj, q_col0, k_col0, v_col0, row0, nb, seq_len, heads, d, k_cache, v_cache, layer,
                     lam_vecs, subln, lam_init):
    hw = 2 * d
    past = k_cache.shape[2]
    assert row0 % seq_len == 0
    rb = row0 // seq_len
    qc, kc, vc = q_col0 // hw, k_col0 // hw, v_col0 // hw
    cache_spec = pl.BlockSpec((None, None, past, hw), lambda b, h: (layer, b, 0, h))
    return pl.pallas_call(
        functools.partial(_diff_sample_kernel, d=d, scale=1.0 / math.sqrt(d), lam_init=lam_init),
        grid=(nb, heads),
        in_specs=[pl.BlockSpec((seq_len, hw), lambda b, h: (rb + b, qc + h)),
                  pl.BlockSpec((seq_len, hw), lambda b, h: (rb + b, kc + h)),
                  pl.BlockSpec((seq_len, hw), lambda b, h: (rb + b, vc + h)),
                  cache_spec, cache_spec] + _lam_specs(d, 2),
        out_specs=pl.BlockSpec((seq_len, hw), lambda b, h: (b, h)),
        out_shape=jax.ShapeDtypeStruct((nb * seq_len, heads * hw), BF16),
        compiler_params=_params("parallel", "parallel"),
    )(proj, proj, proj, k_cache, v_cache, *[v.reshape(1, d) for v in lam_vecs], subln.reshape(1, hw))


def _xattn_kernel(q_ref, k_ref, v_ref, o_ref, *, heads, hd, scale):
    for h in range(heads):
        cols = slice(h * hd, (h + 1) * hd)
        qh = (q_ref[:, cols] * scale).astype(BF16)
        s = lax.dot_general(qh, k_ref[:, cols].astype(BF16), NT_DIMS, preferred_element_type=F32)
        pr = jnp.exp(s - jnp.max(s, axis=-1, keepdims=True))
        den = jnp.sum(pr, axis=-1, keepdims=True)
        o = jnp.dot(pr.astype(BF16), v_ref[:, cols].astype(BF16), preferred_element_type=F32)
        o_ref[:, cols] = (o / den).astype(BF16)


def cross_attn(q_all, row0, nb, seq_len, k_arr, v_arr, kv_specs, heads, hd, *, tq_pref=512):
    width = heads * hd
    tq = _tile(seq_len, tq_pref, 16)
    assert row0 % tq == 0
    blk0, per_seq = row0 // tq, seq_len // tq
    return pl.pallas_call(
        functools.partial(_xattn_kernel, heads=heads, hd=hd, scale=1.0 / math.sqrt(hd)),
        grid=(nb, per_seq),
        in_specs=[pl.BlockSpec((tq, width), lambda b, t: (blk0 + b * per_seq + t, 0))] + kv_specs,
        out_specs=pl.BlockSpec((tq, width), lambda b, t: (b * per_seq + t, 0)),
        out_shape=jax.ShapeDtypeStruct((nb * seq_len, width), BF16),
        compiler_params=_params("parallel", "parallel"),
    )(q_all, k_arr, v_arr)


def kernel(x_prompt, x_sample, state_ssm, state_ssm_conv, state_pool, cache_diff_k, cache_diff_v, cache_mem_k, cache_mem_v, state_ffn_conv, mem_prompt, norm_mix_pre, norm_mix_post, w_in, ssm_conv_w, ssm_conv_b, ssm_dt_bias, ssm_a_log, ssm_d, ssm_norm, w_br_ssm, pool_w, pool_scale, w_br_pool, diff_lq1, diff_lk1, diff_lq2, diff_lk2, diff_subln, w_br_diff, w_o, norm_xa_pre, norm_xa_post, norm_mem, xa_wq, xa_wk, xa_wv, xa_wo, norm_ffn_pre, norm_ffn_post, ffn_w_up, ffn_conv_w, ffn_conv_b, ffn_w_down):
    bp, lp, dm = x_prompt.shape
    bs, ls, _ = x_sample.shape
    depth = w_in.shape[0]
    tp, ts = bp * lp, bs * ls
    heads, p_dim, n_state = state_ssm.shape[2:]
    d_inner = heads * p_dim
    xbc_w = state_ssm_conv.shape[3]
    groups = (xbc_w - d_inner) // (2 * n_state)
    k_ssm = ssm_conv_w.shape[1]
    pool_hist, pool_width = state_pool.shape[2:]
    past, dheads, _, dh = cache_diff_k.shape[2:]
    diff_w = dheads * 2 * dh
    n_mem, xheads, xhd = cache_mem_k.shape[2:]
    xa_w = xheads * xhd
    d_ff = ffn_w_down.shape[1]
    k_ffn = ffn_conv_w.shape[1]
    ffp = _round_up(d_ff, 1024)
    assert lp >= pool_hist and ls >= pool_hist and min(lp, ls) >= max(k_ssm, k_ffn) - 1

    o_z, o_xbc, o_dt = 0, d_inner, d_inner + xbc_w
    o_u = o_dt + heads
    o_q = o_u + pool_width
    o_k, o_v = o_q + diff_w, o_q + 2 * diff_w
    o_g = o_q + 3 * diff_w

    def seg(off, width):
        return w_in[:, :, off:off + width]

    w_main = jnp.concatenate(
        [seg(o_g, 3 * dm), seg(o_xbc, xbc_w), seg(o_z, d_inner), seg(o_q, 3 * diff_w), seg(o_u, pool_width)],
        axis=-1).astype(BF16)
    c_g, c_xbc = 0, 3 * dm
    c_z = c_xbc + xbc_w
    c_q = c_z + d_inner
    c_k, c_v = c_q + diff_w, c_q + 2 * diff_w
    c_u = c_q + 3 * diff_w
    dt_pad = _round_up(heads, LANES)
    w_dt = jnp.pad(seg(o_dt, heads), ((0, 0), (0, 0), (0, dt_pad - heads))).astype(BF16)
    w_bs, w_bp, w_bd = w_br_ssm.astype(BF16), w_br_pool.astype(BF16), w_br_diff.astype(BF16)
    w_out = w_o.astype(BF16)
    pool_wb = pool_w.astype(BF16)
    w_q = xa_wq.astype(BF16)
    w_kv = jnp.concatenate([xa_wk, xa_wv], axis=-1).astype(BF16)
    w_xo = xa_wo.astype(BF16)

    def pad_ff(a):
        pad = [(0, 0)] * (a.ndim - 1) + [(0, ffp - d_ff)]
        return jnp.concatenate([jnp.pad(a[..., :d_ff], pad), jnp.pad(a[..., d_ff:], pad)], axis=-1)

    w_up = pad_ff(ffn_w_up).astype(BF16)
    conv_w_ff = pad_ff(ffn_conv_w)
    conv_b_ff = pad_ff(ffn_conv_b)
    ffn_hist = pad_ff(state_ffn_conv)
    w_down = jnp.pad(ffn_w_down, ((0, 0), (0, ffp - d_ff), (0, 0))).astype(BF16)

    k_cache = cache_diff_k.reshape(depth, bs, past, diff_w)
    v_cache = cache_diff_v.reshape(depth, bs, past, diff_w)
    mem_k = cache_mem_k.reshape(depth, bs, n_mem, xa_w)
    mem_v = cache_mem_v.reshape(depth, bs, n_mem, xa_w)
    mem_rows = mem_prompt.reshape(bp * n_mem, dm)

    h = jnp.concatenate([x_prompt.reshape(tp, dm), x_sample.reshape(ts, dm)], axis=0)
    zeros_p = lambda *shape: jnp.zeros((bp,) + shape, F32)
    q_ssd = _tile(lp, 128, LANES)

    def split(arr, width):
        return arr[:tp].reshape(bp, lp, width), arr[tp:].reshape(bs, ls, width)

    def tail(arr, col0, width, n):
        a_p, a_s = split(arr, arr.shape[1])
        return a_p[:, lp - n:, col0:col0 + width], a_s[:, ls - n:, col0:col0 + width]

    outs = [[] for _ in range(14)]
    for l in range(depth):
        lam_init = 0.8 - 0.6 * math.exp(-0.3 * l)
        lam_vecs = (diff_lq1[l], diff_lk1[l], diff_lq2[l], diff_lk2[l])

        proj = norm_mm(h, norm_mix_pre[l], w_main[l])
        dt_raw = norm_mm(h, norm_mix_pre[l], w_dt[l])[:, :heads]
        dt_p, dt_s = split(dt_raw, heads)

        groups_io = []
        for (row0, nb, sl, dt_g, conv_hist, h0, p_hist, pos_base, q_chunk) in (
                (0, bp, lp, dt_p, zeros_p(k_ssm - 1, xbc_w), zeros_p(heads, p_dim, n_state),
                 zeros_p(pool_hist, pool_width), 0, q_ssd),
                (tp, bs, ls, dt_s, state_ssm_conv[l], state_ssm[l], state_pool[l], past, ls)):
            xbc = conv_silu(proj, c_xbc, xbc_w, row0, nb, sl, conv_hist, ssm_conv_w[l], ssm_conv_b[l])
            y_ssm, h_last = ssd_mix(xbc, proj, c_z, row0, nb, sl, dt_g, ssm_dt_bias[l], ssm_a_log[l], ssm_d[l],
                                    ssm_norm[l], h0, groups=groups, n_state=n_state, q=q_chunk)
            y_pool = pool_mix(proj, c_u, row0, nb, sl, p_hist, pos_base, pool_wb[l], pool_scale[l])
            groups_io.append((y_ssm, h_last, y_pool))
        y_diff_p = diff_attn_prompt(proj, c_q, c_k, c_v, bp, lp, dheads, dh, lam_vecs, diff_subln[l], lam_init)
        y_diff_s = diff_attn_sample(proj, c_q, c_k, c_v, tp, bs, ls, dheads, dh, k_cache, v_cache, l,
                                    lam_vecs, diff_subln[l], lam_init)
        (ys_p, hl_p, yp_p), (ys_s, hl_s, yp_s) = groups_io
        merged = branch_merge(jnp.concatenate([ys_p, ys_s]), jnp.concatenate([yp_p, yp_s]),
                              jnp.concatenate([y_diff_p, y_diff_s]), proj, c_g, w_bs[l], w_bp[l], w_bd[l])
        h = mm_post(merged, w_out[l], h, norm_mix_post[l])

        mem_kv = norm_mm(mem_rows, norm_mem[l], w_kv[l])
        q_xa = norm_mm(h, norm_xa_pre[l], w_q[l])
        kv_p = [pl.BlockSpec((n_mem, xa_w), lambda b, t: (b, 0)), pl.BlockSpec((n_mem, xa_w), lambda b, t: (b, 1))]
        kv_s = [pl.BlockSpec((None, None, n_mem, xa_w), lambda b, t, l=l: (l, b, 0, 0))] * 2
        o_p = cross_attn(q_xa, 0, bp, lp, mem_kv, mem_kv, kv_p, xheads, xhd)
        o_s = cross_attn(q_xa, tp, bs, ls, mem_k, mem_v, kv_s, xheads, xhd)
        h = mm_post(jnp.concatenate([o_p, o_s]), w_xo[l], h, norm_xa_post[l])

        up = norm_mm(h, norm_ffn_pre[l], w_up[l])
        hid_p = conv_geglu(up, ffp, 0, bp, lp, zeros_p(k_ffn - 1, 2 * ffp), conv_w_ff[l], conv_b_ff[l])
        hid_s = conv_geglu(up, ffp, tp, bs, ls, ffn_hist[l], conv_w_ff[l], conv_b_ff[l])
        h = mm_post(jnp.concatenate([hid_p, hid_s]), w_down[l], h, norm_ffn_post[l])

        conv_p, conv_s = tail(proj, c_xbc, xbc_w, k_ssm - 1)
        pool_p, pool_s = tail(proj, c_u, pool_width, pool_hist)
        kk_p, kk_s = split(proj[:, c_k:c_k + diff_w], diff_w)
        vv_p, vv_s = split(proj[:, c_v:c_v + diff_w], diff_w)
        ffg_p, ffg_s = tail(up, 0, d_ff, k_ffn - 1)
        ffv_p, ffv_s = tail(up, ffp, d_ff, k_ffn - 1)
        layer_out = (hl_p, hl_s, conv_p, conv_s, pool_p, pool_s,
                     kk_p.reshape(bp, lp, dheads, 2, dh), kk_s.reshape(bs, ls, dheads, 2, dh),
                     vv_p.reshape(bp, lp, dheads, 2 * dh), vv_s.reshape(bs, ls, dheads, 2 * dh),
                     mem_kv[:, :xa_w].reshape(bp, n_mem, xheads, xhd),
                     mem_kv[:, xa_w:].reshape(bp, n_mem, xheads, xhd),
                     jnp.concatenate([ffg_p, ffv_p], axis=-1), jnp.concatenate([ffg_s, ffv_s], axis=-1))
        for acc, val in zip(outs, layer_out):
            acc.append(val)

    return (h[:tp].reshape(bp, lp, dm), h[tp:].reshape(bs, ls, dm)) + tuple(jnp.stack(o) for o in outs)
```

```python
import functools
import math

import jax
import jax.numpy as jnp
from jax import lax
from jax.experimental import pallas as pl
from jax.experimental.pallas import tpu as pltpu

F32 = jnp.float32
BF16 = jnp.bfloat16
EPS = 1e-6
CHUNK = 64
POOL_WINDOWS = (2, 4, 8, 16)
SUBLANES = 8
PACKED_ROWS = 16
LANES = 128
VMEM_LIMIT_BYTES = 56 * 1024 * 1024
NT_DIMS = (((1,), (1,)), ((), ()))
TN_DIMS = (((0,), (0,)), ((), ()))
SINGLE = pl.Buffered(1)


def _tile(n, pref, align):
    t = (min(pref, n) // align) * align
    while t >= align:
        if n % t == 0:
            return t
        t -= align
    return n


def _round_up(n, m):
    return (n + m - 1) // m * m


def _params(*sem):
    return pltpu.CompilerParams(dimension_semantics=sem, vmem_limit_bytes=VMEM_LIMIT_BYTES)


def _sigmoid(x):
    return 1.0 / (1.0 + jnp.exp(-x))


def _softplus(x):
    return jnp.maximum(x, 0.0) + jnp.log(1.0 + jnp.exp(-jnp.abs(x)))


def _gelu_tanh(x):
    return 0.5 * x * (1.0 + jnp.tanh(math.sqrt(2.0 / math.pi) * (x + 0.044715 * (x * x * x))))


def _rms(x, g):
    return x * lax.rsqrt(jnp.mean(x * x, axis=-1, keepdims=True) + EPS) * g


def _drop_ref(body, idx):
    def wrapped(*refs):
        return body(*refs[:idx], *refs[idx + 1:])
    return wrapped


def _into(joint, in_specs, operands):
    if joint is None:
        return in_specs, operands, {}
    return (in_specs + [pl.BlockSpec(memory_space=pl.ANY)], operands + [joint], {len(operands): 0})


def _norm_kernel(x_ref, g_ref, o_ref, *, rows):
    def body(r, carry):
        sl = pl.ds(pl.multiple_of(r * rows, rows), rows)
        o_ref[sl, :] = _rms(x_ref[sl, :], g_ref[...]).astype(BF16)
        return carry
    lax.fori_loop(0, x_ref.shape[0] // rows, body, 0)


def rmsnorm_bf16(x, g, *, tm_pref=256):
    t, d = x.shape
    tm = _tile(t, tm_pref, PACKED_ROWS)
    return pl.pallas_call(
        functools.partial(_norm_kernel, rows=PACKED_ROWS),
        grid=(t // tm,),
        in_specs=[pl.BlockSpec((tm, d), lambda i: (i, 0)), pl.BlockSpec((1, d), lambda i: (0, 0))],
        out_specs=pl.BlockSpec((tm, d), lambda i: (i, 0)),
        out_shape=jax.ShapeDtypeStruct((t, d), BF16),
        compiler_params=_params("parallel"),
    )(x, g.reshape(1, d))


def _mm_kernel(x_ref, w_ref, o_ref):
    o_ref[...] = jnp.dot(x_ref[...], w_ref[...], preferred_element_type=F32)


def mm(x, w_all, layer, *, row0=0, nrows=None, tm_pref=768, tn_pref=1024):
    k = x.shape[1]
    n = w_all.shape[2]
    nrows = x.shape[0] - row0 if nrows is None else nrows
    tm = _tile(math.gcd(nrows, row0) if row0 else nrows, tm_pref, PACKED_ROWS)
    tn = _tile(n, tn_pref, LANES)
    rb = row0 // tm
    return pl.pallas_call(
        _mm_kernel,
        grid=(nrows // tm, n // tn),
        in_specs=[pl.BlockSpec((tm, k), lambda i, j: (rb + i, 0)),
                  pl.BlockSpec((None, k, tn), lambda i, j: (layer, 0, j))],
        out_specs=pl.BlockSpec((tm, tn), lambda i, j: (i, j)),
        out_shape=jax.ShapeDtypeStruct((nrows, n), F32),
        compiler_params=_params("parallel", "parallel"),
    )(x, w_all)


def _mm_post_kernel(x_ref, w_ref, res_ref, g_ref, gn_ref, o_ref, *maybe_xn_ref, rows, tn, nk):
    j, kk = pl.program_id(1), pl.program_id(2)
    part = jnp.dot(x_ref[...], w_ref[...], preferred_element_type=F32)
    cols = pl.ds(pl.multiple_of(j * tn, tn), tn)
    if nk == 1:
        o_ref[:, cols] = part
    else:
        @pl.when(kk == 0)
        def _():
            o_ref[:, cols] = part

        @pl.when(kk > 0)
        def _():
            o_ref[:, cols] += part

    @pl.when((j == pl.num_programs(1) - 1) & (kk == nk - 1))
    def _():
        def body(r, carry):
            sl = pl.ds(pl.multiple_of(r * rows, rows), rows)
            h_new = res_ref[sl, :] + _rms(o_ref[sl, :], g_ref[...])
            o_ref[sl, :] = h_new
            for xn_ref in maybe_xn_ref:
                xn_ref[sl, :] = _rms(h_new, gn_ref[...]).astype(BF16)
            return carry
        lax.fori_loop(0, o_ref.shape[0] // rows, body, 0)


def mm_post(x, w_all, layer, res, g, g_next, *, tn_pref, tk_pref, tm_pref=512):
    t, k = x.shape
    d = w_all.shape[2]
    tm = _tile(t, tm_pref, PACKED_ROWS)
    tn = _tile(d, tn_pref, LANES)
    tk = _tile(k, tk_pref, LANES)
    nk = k // tk
    row_spec = pl.BlockSpec((tm, d), lambda i, j, kk: (i, 0))
    vec_spec = pl.BlockSpec((1, d), lambda i, j, kk: (0, 0))
    emit = g_next is not None
    out = pl.pallas_call(
        functools.partial(_mm_post_kernel, rows=PACKED_ROWS, tn=tn, nk=nk),
        grid=(t // tm, d // tn, nk),
        in_specs=[pl.BlockSpec((tm, tk), lambda i, j, kk: (i, kk)),
                  pl.BlockSpec((None, tk, tn), lambda i, j, kk: (layer, kk, j)),
                  pl.BlockSpec((tm, d), lambda i, j, kk: (i, 0), pipeline_mode=SINGLE),
                  vec_spec, vec_spec],
        out_specs=[row_spec, row_spec] if emit else [row_spec],
        out_shape=[jax.ShapeDtypeStruct((t, d), F32)] + ([jax.ShapeDtypeStruct((t, d), BF16)] if emit else []),
        compiler_params=_params("parallel", "arbitrary", "arbitrary"),
    )(x, w_all, res, g.reshape(1, d), (g_next if emit else g).reshape(1, d))
    return tuple(out) if emit else out[0]


def _merge_kernel(ys_ref, yp_ref, yd_ref, g0_ref, g1_ref, g2_ref, ws_ref, wp_ref, wd_ref, o_ref):
    acc = _sigmoid(g0_ref[...]) * jnp.dot(ys_ref[...], ws_ref[...], preferred_element_type=F32)
    acc += _sigmoid(g1_ref[...]) * jnp.dot(yp_ref[...], wp_ref[...], preferred_element_type=F32)
    acc += _sigmoid(g2_ref[...]) * jnp.dot(yd_ref[...], wd_ref[...], preferred_element_type=F32)
    o_ref[...] = acc.astype(BF16)


def branch_merge(y_ssm, y_pool, y_diff, proj, gate_col0, w_s, w_p, w_d, layer, *, tm_pref=512, tn_pref=1024):
    t = y_ssm.shape[0]
    d = w_s.shape[2]
    tm = _tile(t, tm_pref, PACKED_ROWS)
    tn = _tile(d, tn_pref, LANES)
    assert gate_col0 % tn == 0
    nj = d // tn
    gj = gate_col0 // tn

    def gate_spec(br):
        return pl.BlockSpec((tm, tn), lambda i, j: (i, gj + br * nj + j))

    def x_spec(kdim):
        return pl.BlockSpec((tm, kdim), lambda i, j: (i, 0))

    def w_spec(kdim):
        return pl.BlockSpec((None, kdim, tn), lambda i, j: (layer, 0, j))

    return pl.pallas_call(
        _merge_kernel,
        grid=(t // tm, nj),
        in_specs=[x_spec(y_ssm.shape[1]), x_spec(y_pool.shape[1]), x_spec(y_diff.shape[1]),
                  gate_spec(0), gate_spec(1), gate_spec(2),
                  w_spec(w_s.shape[1]), w_spec(w_p.shape[1]), w_spec(w_d.shape[1])],
        out_specs=pl.BlockSpec((tm, tn), lambda i, j: (i, j)),
        out_shape=jax.ShapeDtypeStruct((t, d), BF16),
        compiler_params=_params("parallel", "parallel"),
    )(y_ssm, y_pool, y_diff, proj, proj, proj, w_s, w_p, w_d)


def _conv_rows(ext_ref, u, halo, w_ref, b_ref, k):
    n = u.shape[0]
    ext_ref[0:SUBLANES, :] = halo
    ext_ref[SUBLANES:SUBLANES + n, :] = u
    acc = u * w_ref[k - 1:k, :]
    for i in range(k - 1):
        acc = acc + ext_ref[pl.ds(SUBLANES - (k - 1) + i, n), :] * w_ref[i:i + 1, :]
    return acc + b_ref[...]


def _halo(prev_ref, hist_ref, s, nseq):
    if nseq > 1:
        return hist_ref[s]
    return jnp.where(pl.program_id(1) == 0, hist_ref[0], prev_ref[...])


def _conv_silu_kernel(u_ref, prev_ref, hist_ref, w_ref, b_ref, o_ref, ext_ref, *, k, sl, nseq):
    for s in range(nseq):
        rows = slice(s * sl, (s + 1) * sl)
        y = _conv_rows(ext_ref, u_ref[rows, :], _halo(prev_ref, hist_ref, s, nseq), w_ref, b_ref, k)
        o_ref[rows, :] = y * _sigmoid(y)


def _seq_tiling(nb, seq_len, tl_pref, align):
    if seq_len >= tl_pref:
        return _tile(seq_len, tl_pref, align), 1
    nseq = _tile(nb, max(tl_pref // seq_len, 1), 1)
    return nseq * seq_len, nseq


def _seq_specs(row0, rows_per_b, tl, tc, col_blk0):
    blk0 = row0 // tl
    per_b = rows_per_b // tl
    sub0 = row0 // SUBLANES
    sub_per_b = rows_per_b // SUBLANES
    sub_per_tile = tl // SUBLANES
    cur = pl.BlockSpec((tl, tc), lambda b, t, c: (blk0 + b * per_b + t, col_blk0 + c))
    prev = pl.BlockSpec(
        (SUBLANES, tc),
        lambda b, t, c: (jnp.maximum(sub0 + b * sub_per_b + t * sub_per_tile - 1, 0), col_blk0 + c))
    return cur, prev


def _pad_hist(hist, rows):
    return jnp.pad(hist, ((0, 0), (rows - hist.shape[1], 0), (0, 0)))


def conv_silu(proj, col0, width, row0, nb, seq_len, hist, w, bias, *, tl_pref=512, tc_pref=512):
    k = w.shape[0]
    tl, nseq = _seq_tiling(nb, seq_len, tl_pref, SUBLANES)
    tc = _tile(width, tc_pref, LANES)
    rows_per_b = max(tl, seq_len)
    assert row0 % tl == 0 and col0 % tc == 0
    cur, prev = _seq_specs(row0, rows_per_b, tl, tc, col0 // tc)
    per_b = rows_per_b // tl
    return pl.pallas_call(
        functools.partial(_conv_silu_kernel, k=k, sl=tl // nseq, nseq=nseq),
        grid=(nb // nseq, per_b, width // tc),
        in_specs=[cur, prev,
                  pl.BlockSpec((nseq, SUBLANES, tc), lambda b, t, c: (b, 0, c)),
                  pl.BlockSpec((k, tc), lambda b, t, c: (0, c)),
                  pl.BlockSpec((1, tc), lambda b, t, c: (0, c))],
        out_specs=pl.BlockSpec((tl, tc), lambda b, t, c: (b * per_b + t, c)),
        out_shape=jax.ShapeDtypeStruct((nb * seq_len, width), F32),
        scratch_shapes=[pltpu.VMEM((SUBLANES + tl // nseq, tc), F32)],
        compiler_params=_params("parallel", "parallel", "parallel"),
    )(proj, proj, _pad_hist(hist, SUBLANES), w, bias.reshape(1, width))


def _conv_geglu_kernel(ug_ref, pg_ref, hg_ref, wg_ref, bg_ref, uv_ref, pv_ref, hv_ref, wv_ref, bv_ref,
                       o_ref, extg_ref, extv_ref, *, k, sl, nseq):
    for s in range(nseq):
        rows = slice(s * sl, (s + 1) * sl)
        gate = _conv_rows(extg_ref, ug_ref[rows, :], _halo(pg_ref, hg_ref, s, nseq), wg_ref, bg_ref, k)
        val = _conv_rows(extv_ref, uv_ref[rows, :], _halo(pv_ref, hv_ref, s, nseq), wv_ref, bv_ref, k)
        o_ref[rows, :] = (_gelu_tanh(gate) * val).astype(BF16)


def conv_geglu(up, half, nb, seq_len, hist, w, bias, joint, out_row0, *, tl_pref=512, tc_pref=1024):
    k = w.shape[0]
    tl, nseq = _seq_tiling(nb, seq_len, tl_pref, PACKED_ROWS)
    tc = _tile(half, tc_pref, LANES)
    rows_per_b = max(tl, seq_len)
    per_b = rows_per_b // tl
    ncb = half // tc
    assert out_row0 % tl == 0
    ob = out_row0 // tl
    cur_g, prev_g = _seq_specs(0, rows_per_b, tl, tc, 0)
    cur_v, prev_v = _seq_specs(0, rows_per_b, tl, tc, ncb)
    hist8 = _pad_hist(hist, SUBLANES)
    bias2 = bias.reshape(1, 2 * half)

    def side(off):
        return [pl.BlockSpec((nseq, SUBLANES, tc), lambda b, t, c: (b, 0, off + c)),
                pl.BlockSpec((k, tc), lambda b, t, c: (0, off + c)),
                pl.BlockSpec((1, tc), lambda b, t, c: (0, off + c))]

    in_specs, operands, aliases = _into(
        joint, [cur_g, prev_g] + side(0) + [cur_v, prev_v] + side(ncb),
        [up, up, hist8, w, bias2, up, up, hist8, w, bias2])
    body = functools.partial(_conv_geglu_kernel, k=k, sl=tl // nseq, nseq=nseq)
    return pl.pallas_call(
        _drop_ref(body, 10) if aliases else body,
        grid=(nb // nseq, per_b, ncb),
        in_specs=in_specs,
        out_specs=pl.BlockSpec((tl, tc), lambda b, t, c: (ob + b * per_b + t, c)),
        out_shape=jax.ShapeDtypeStruct(joint.shape, BF16),
        input_output_aliases=aliases,
        scratch_shapes=[pltpu.VMEM((SUBLANES + tl // nseq, tc), F32)] * 2,
        compiler_params=_params("parallel", "parallel", "parallel"),
    )(*operands)


def _ffn_up_kernel(xc_ref, xp_ref, wg_ref, wv_ref, hg_ref, hv_ref, cwg_ref, cbg_ref, cwv_ref, cbv_ref,
                   hid_ref, tg_ref, tv_ref, xe_ref, eg_ref, ev_ref, *, k, tm, tiles_per_seq):
    halo = PACKED_ROWS

    @pl.when(pl.program_id(1) == 0)
    def _():
        xe_ref[0:halo, :] = xp_ref[...]
        xe_ref[halo:halo + tm, :] = xc_ref[...]

    first = pl.program_id(0) % tiles_per_seq == 0
    outs = []
    for w_ref, h_ref, e_ref, cw_ref, cb_ref, t_ref in ((wg_ref, hg_ref, eg_ref, cwg_ref, cbg_ref, tg_ref),
                                                       (wv_ref, hv_ref, ev_ref, cwv_ref, cbv_ref, tv_ref)):
        e_ref[...] = jnp.dot(xe_ref[...], w_ref[...], preferred_element_type=F32)

        @pl.when(first)
        def _():
            e_ref[0:halo, :] = h_ref[...]

        t_ref[...] = e_ref[tm:tm + halo, :]
        acc = e_ref[halo:halo + tm, :] * cw_ref[k - 1:k, :]
        for i in range(k - 1):
            acc = acc + e_ref[pl.ds(halo - (k - 1) + i, tm), :] * cw_ref[i:i + 1, :]
        outs.append(acc + cb_ref[...])
    hid_ref[...] = (_gelu_tanh(outs[0]) * outs[1]).astype(BF16)


def ffn_up_geglu(xn, nb, seq_len, w_up_all, layer, half, hist, conv_w, conv_b, *, tm_pref=1024, tf_pref=512):
    t, d = xn.shape
    k = conv_w.shape[0]
    halo = PACKED_ROWS
    tm = _tile(seq_len, tm_pref, halo)
    tf = _tile(half, tf_pref, LANES)
    tiles_per_seq = seq_len // tm
    ncb = half // tf
    hist16 = _pad_hist(hist, halo)
    bias2 = conv_b.reshape(1, 2 * half)

    def w_spec(off):
        return pl.BlockSpec((None, d, tf), lambda i, j: (layer, 0, off + j))

    def hist_spec(off):
        return pl.BlockSpec((None, halo, tf), lambda i, j: (i // tiles_per_seq, 0, off + j))

    def cw_spec(off):
        return pl.BlockSpec((k, tf), lambda i, j: (0, off + j))

    def cb_spec(off):
        return pl.BlockSpec((1, tf), lambda i, j: (0, off + j))

    tail_spec = pl.BlockSpec((None, halo, tf), lambda i, j: (i, 0, j))
    tail_shape = jax.ShapeDtypeStruct((nb * tiles_per_seq, halo, half), F32)
    hid, tail_g, tail_v = pl.pallas_call(
        functools.partial(_ffn_up_kernel, k=k, tm=tm, tiles_per_seq=tiles_per_seq),
        grid=(nb * tiles_per_seq, ncb),
        in_specs=[pl.BlockSpec((tm, d), lambda i, j: (i, 0)),
                  pl.BlockSpec((halo, d), lambda i, j: (jnp.maximum(i * (tm // halo) - 1, 0), 0)),
                  w_spec(0), w_spec(ncb), hist_spec(0), hist_spec(ncb),
                  cw_spec(0), cb_spec(0), cw_spec(ncb), cb_spec(ncb)],
        out_specs=[pl.BlockSpec((tm, tf), lambda i, j: (i, j)), tail_spec, tail_spec],
        out_shape=[jax.ShapeDtypeStruct((t, half), BF16), tail_shape, tail_shape],
        scratch_shapes=[pltpu.VMEM((halo + tm, d), BF16), pltpu.VMEM((halo + tm, tf), F32),
                        pltpu.VMEM((halo + tm, tf), F32)],
        compiler_params=_params("parallel", "arbitrary"),
    )(xn, xn, w_up_all, w_up_all, hist16, hist16, conv_w, bias2, conv_w, bias2)
    last = slice(tiles_per_seq - 1, None, tiles_per_seq)
    return hid, tail_g[last], tail_v[last]


def _pool_kernel(u_ref, prev_ref, hist_ref, pw_ref, ps_ref, o_ref, ext_ref, *, tl, halo, pos_base, gd):
    t = pl.program_id(1)
    ext_ref[0:halo, :] = jnp.where(t == 0, hist_ref[...], prev_ref[...])
    ext_ref[halo:halo + tl, :] = u_ref[...]
    pos = pos_base + t * tl + lax.broadcasted_iota(jnp.int32, (tl, 1), 0)
    for gi, win in enumerate(POOL_WINDOWS):
        cols = slice(gi * gd, (gi + 1) * gd)
        cur = u_ref[:, cols]
        wsum = cur
        for i in range(1, win):
            wsum = wsum + ext_ref[pl.ds(halo - i, tl), cols]
        cnt = jnp.minimum(pos + 1, win).astype(F32)
        pooled = wsum / cnt - cur
        y = jnp.dot(pooled.astype(BF16), pw_ref[gi], preferred_element_type=F32) * ps_ref[:, cols]
        o_ref[:, cols] = y.astype(BF16)


def pool_mix(proj, col0, row0, nb, seq_len, hist, pos_base, pool_w, pool_scale, joint, *, tl_pref=512):
    ng, gd, _ = pool_w.shape
    width = ng * gd
    halo = 2 * SUBLANES
    assert ng == len(POOL_WINDOWS) and max(POOL_WINDOWS) <= halo
    tl = _tile(seq_len, tl_pref, halo)
    assert row0 % tl == 0 and col0 % width == 0
    blk0, per_seq = row0 // tl, seq_len // tl
    h0, h_per_seq, h_per_tile = row0 // halo, seq_len // halo, tl // halo
    cb = col0 // width
    in_specs, operands, aliases = _into(
        joint,
        [pl.BlockSpec((tl, width), lambda b, t: (blk0 + b * per_seq + t, cb)),
         pl.BlockSpec((halo, width),
                      lambda b, t: (jnp.maximum(h0 + b * h_per_seq + t * h_per_tile - 1, 0), cb)),
         pl.BlockSpec((None, halo, width), lambda b, t: (b, 0, 0)),
         pl.BlockSpec((ng, gd, gd), lambda b, t: (0, 0, 0)),
         pl.BlockSpec((1, width), lambda b, t: (0, 0))],
        [proj, proj, _pad_hist(hist, halo), pool_w, pool_scale.reshape(1, width)])
    body = functools.partial(_pool_kernel, tl=tl, halo=halo, pos_base=pos_base, gd=gd)
    return pl.pallas_call(
        _drop_ref(body, 5) if aliases else body,
        grid=(nb, per_seq),
        in_specs=in_specs,
        out_specs=pl.BlockSpec((tl, width), lambda b, t: (blk0 + b * per_seq + t, 0)),
        out_shape=jax.ShapeDtypeStruct((proj.shape[0], width), BF16),
        input_output_aliases=aliases,
        scratch_shapes=[pltpu.VMEM((halo + tl, width), F32)],
        compiler_params=_params("parallel", "parallel"),
    )(*operands)


def _ssd_kernel(xs_ref, b_ref, c_ref, z_ref, dtc_ref, dtr_ref, biasc_ref, biasr_ref, alogc_ref, alogr_ref,
                dskip_ref, nw_ref, h0_ref, y_ref, hl_ref, h_ref, *, q, nsub, hpg, p):
    ti = pl.program_id(2)

    @pl.when(ti == 0)
    def _():
        h_ref[...] = h0_ref[...]

    dt_c_all = _softplus(dtc_ref[...] + biasc_ref[...])
    dt_r_all = _softplus(dtr_ref[...] + biasr_ref[...])
    a_c = -jnp.exp(alogc_ref[...])
    a_r = -jnp.exp(alogr_ref[...])
    row = lax.broadcasted_iota(jnp.int32, (q, q), 0)
    col = lax.broadcasted_iota(jnp.int32, (q, q), 1)
    lower = row >= col
    tril = lower.astype(F32)
    triu = (row <= col).astype(F32)
    h = [h_ref[r] for r in range(hpg)]
    for s in range(nsub):
        rows = slice(s * q, (s + 1) * q)
        dt_c = dt_c_all[rows, :]
        dt_r = dt_r_all[:, rows]
        acum_c = jnp.dot(tril, dt_c * a_c, precision=lax.Precision.HIGHEST, preferred_element_type=F32)
        acum_r = jnp.dot(dt_r * a_r, triu, precision=lax.Precision.HIGHEST, preferred_element_type=F32)
        bm = b_ref[rows, :].astype(BF16)
        cm = c_ref[rows, :].astype(BF16)
        cb = lax.dot_general(cm, bm, NT_DIMS, preferred_element_type=F32)
        a_last = acum_c[q - 1:q, :]
        ys = []
        for r in range(hpg):
            x_r = xs_ref[rows, r * p:(r + 1) * p]
            ac = acum_c[:, r:r + 1]
            ar = acum_r[r:r + 1, :]
            decay = jnp.exp(jnp.where(lower, ac - ar, -jnp.inf))
            wmat = cb * decay * dt_r[r:r + 1, :]
            y = jnp.dot(wmat.astype(BF16), x_r.astype(BF16), preferred_element_type=F32)
            y_off = lax.dot_general(cm, h[r].astype(BF16), NT_DIMS, preferred_element_type=F32)
            y = y + y_off * jnp.exp(ac)
            al = a_last[:, r:r + 1]
            w_end = jnp.exp(al - ac) * dt_c[:, r:r + 1]
            s_new = lax.dot_general((x_r * w_end).astype(BF16), bm, TN_DIMS, preferred_element_type=F32)
            h[r] = h[r] * jnp.exp(al) + s_new
            ys.append(y + x_r * dskip_ref[:, r:r + 1])
        yg = jnp.concatenate(ys, axis=1)
        zt = z_ref[rows, :]
        yg = yg * (zt * _sigmoid(zt))
        y_ref[rows, :] = _rms(yg, nw_ref[...]).astype(BF16)
    for r in range(hpg):
        h_ref[r] = h[r]

    @pl.when(ti == pl.num_programs(2) - 1)
    def _():
        hl_ref[...] = h_ref[...]


def ssd_mix(xbc, proj, z_col0, row0, nb, seq_len, dt_raw, dt_bias, a_log, d_skip, norm_w, h0, joint, *,
            groups, n_state, q, nsub):
    heads, p = h0.shape[1], h0.shape[2]
    hpg = heads // groups
    gw = hpg * p
    d_inner = heads * p
    tl = q * nsub
    nt = seq_len // tl
    assert seq_len % tl == 0 and row0 % tl == 0 and z_col0 % gw == 0 and d_inner % n_state == 0
    dt_g = dt_raw.reshape(nb, seq_len, groups, hpg)
    dt_col = dt_g.transpose(0, 2, 1, 3)
    dt_row = dt_g.transpose(0, 2, 3, 1)
    blk0 = row0 // tl
    zc = z_col0 // gw
    bcol = d_inner // n_state

    def per_group(arr, shape):
        return arr.reshape((groups,) + shape), pl.BlockSpec((None,) + shape, lambda b, g, t: (g, 0, 0))

    biasc, biasc_spec = per_group(dt_bias, (1, hpg))
    biasr, biasr_spec = per_group(dt_bias, (hpg, 1))
    alogc, alogc_spec = per_group(a_log, (1, hpg))
    alogr, alogr_spec = per_group(a_log, (hpg, 1))
    dsk, dsk_spec = per_group(d_skip, (1, hpg))
    h_spec = pl.BlockSpec((None, hpg, p, n_state), lambda b, g, t: (b, g, 0, 0))
    in_specs, operands, aliases = _into(
        joint,
        [pl.BlockSpec((tl, gw), lambda b, g, t: (b * nt + t, g)),
         pl.BlockSpec((tl, n_state), lambda b, g, t: (b * nt + t, bcol + g)),
         pl.BlockSpec((tl, n_state), lambda b, g, t: (b * nt + t, bcol + groups + g)),
         pl.BlockSpec((tl, gw), lambda b, g, t: (blk0 + b * nt + t, zc + g)),
         pl.BlockSpec((None, None, tl, hpg), lambda b, g, t: (b, g, t, 0)),
         pl.BlockSpec((None, None, hpg, tl), lambda b, g, t: (b, g, 0, t)),
         biasc_spec, biasr_spec, alogc_spec, alogr_spec, dsk_spec,
         pl.BlockSpec((1, gw), lambda b, g, t: (0, g)),
         h_spec],
        [xbc, xbc, xbc, proj, dt_col, dt_row, biasc, biasr, alogc, alogr, dsk,
         norm_w.reshape(1, d_inner), h0])
    body = functools.partial(_ssd_kernel, q=q, nsub=nsub, hpg=hpg, p=p)
    return pl.pallas_call(
        _drop_ref(body, 13) if aliases else body,
        grid=(nb, groups, nt),
        in_specs=in_specs,
        out_specs=[pl.BlockSpec((tl, gw), lambda b, g, t: (blk0 + b * nt + t, g)), h_spec],
        out_shape=[jax.ShapeDtypeStruct((proj.shape[0], d_inner), BF16),
                   jax.ShapeDtypeStruct(h0.shape, F32)],
        input_output_aliases=aliases,
        scratch_shapes=[pltpu.VMEM((hpg, p, n_state), F32)],
        compiler_params=_params("parallel", "parallel", "arbitrary"),
    )(*operands)


def _lambda(lq1_ref, lk1_ref, lq2_ref, lk2_ref, lam_init):
    s1 = jnp.sum(lq1_ref[...] * lk1_ref[...], axis=-1, keepdims=True)
    s2 = jnp.sum(lq2_ref[...] * lk2_ref[...], axis=-1, keepdims=True)
    return jnp.exp(s1) - jnp.exp(s2) + lam_init


def _diff_prompt_kernel(q_ref, k_ref, v_ref, lq1_ref, lk1_ref, lq2_ref, lk2_ref, sub_ref, o_ref,
                        mx_ref, den_ref, acc_ref, *, t, d, scale, lam_init):
    iq = pl.program_id(2)
    lam = _lambda(lq1_ref, lk1_ref, lq2_ref, lk2_ref, lam_init)
    qs = [(q_ref[:, m * d:(m + 1) * d] * scale).astype(BF16) for m in range(2)]
    mx_ref[...] = jnp.full(mx_ref.shape, -jnp.inf, F32)
    den_ref[...] = jnp.zeros(den_ref.shape, F32)
    acc_ref[...] = jnp.zeros(acc_ref.shape, F32)

    def step(j, visible):
        koff = pl.multiple_of(j * t, t)
        vt = v_ref[pl.ds(koff, t), :].astype(BF16)
        for m in range(2):
            kt = k_ref[pl.ds(koff, t), m * d:(m + 1) * d].astype(BF16)
            s = lax.dot_general(qs[m], kt, NT_DIMS, preferred_element_type=F32)
            if visible is not None:
                s = jnp.where(visible, s, -jnp.inf)
            mx_old = mx_ref[m]
            mx_new = jnp.maximum(mx_old, jnp.max(s, axis=-1, keepdims=True))
            alpha = jnp.exp(mx_old - mx_new)
            pr = jnp.exp(s - mx_new)
            mx_ref[m] = mx_new
            den_ref[m] = alpha * den_ref[m] + jnp.sum(pr, axis=-1, keepdims=True)
            acc_ref[m] = alpha * acc_ref[m] + jnp.dot(pr.astype(BF16), vt, preferred_element_type=F32)

    def body(j, carry):
        step(j, None)
        return carry

    lax.fori_loop(0, iq, body, 0)
    q_chunk = lax.broadcasted_iota(jnp.int32, (t, 1), 0) // CHUNK
    k_chunk = lax.broadcasted_iota(jnp.int32, (1, t), 1) // CHUNK
    step(iq, k_chunk <= q_chunk)
    o = acc_ref[0] / den_ref[0] - lam * (acc_ref[1] / den_ref[1])
    o_ref[...] = (_rms(o, sub_ref[...]) * (1.0 - lam_init)).astype(BF16)


def _lam_specs(d):
    fixed = lambda *_: (0, 0)
    return [pl.BlockSpec((1, d), fixed)] * 4 + [pl.BlockSpec((1, 2 * d), fixed)]


def diff_attn_prompt(proj, q_col0, k_col0, v_col0, nb, seq_len, heads, d, lam_vecs, subln, lam_init, *,
                     tq_pref=256):
    hw = 2 * d
    tq = _tile(seq_len, tq_pref, CHUNK)
    nq = seq_len // tq
    assert q_col0 % hw == 0 and k_col0 % hw == 0 and v_col0 % hw == 0
    qc, kc, vc = q_col0 // hw, k_col0 // hw, v_col0 // hw
    return pl.pallas_call(
        functools.partial(_diff_prompt_kernel, t=tq, d=d, scale=1.0 / math.sqrt(d), lam_init=lam_init),
        grid=(nb, heads, nq),
        in_specs=[pl.BlockSpec((tq, hw), lambda b, h, i: (b * nq + i, qc + h)),
                  pl.BlockSpec((seq_len, hw), lambda b, h, i: (b, kc + h)),
                  pl.BlockSpec((seq_len, hw), lambda b, h, i: (b, vc + h))] + _lam_specs(d),
        out_specs=pl.BlockSpec((tq, hw), lambda b, h, i: (b * nq + i, h)),
        out_shape=jax.ShapeDtypeStruct((proj.shape[0], heads * hw), BF16),
        scratch_shapes=[pltpu.VMEM((2, tq, 1), F32), pltpu.VMEM((2, tq, 1), F32),
                        pltpu.VMEM((2, tq, hw), F32)],
        compiler_params=_params("parallel", "parallel", "parallel"),
    )(proj, proj, proj, *[v.reshape(1, d) for v in lam_vecs], subln.reshape(1, hw))


def _diff_sample_kernel(q_ref, kn_ref, vn_ref, kp_ref, vp_ref, lq1_ref, lk1_ref, lq2_ref, lk2_ref, sub_ref,
                        o_ref, *, d, scale, lam_init):
    lam = _lambda(lq1_ref, lk1_ref, lq2_ref, lk2_ref, lam_init)
    vp = vp_ref[...].astype(BF16)
    vn = vn_ref[...].astype(BF16)
    outs = []
    for m in range(2):
        cols = slice(m * d, (m + 1) * d)
        qm = (q_ref[:, cols] * scale).astype(BF16)
        s_p = lax.dot_general(qm, kp_ref[:, cols].astype(BF16), NT_DIMS, preferred_element_type=F32)
        s_n = lax.dot_general(qm, kn_ref[:, cols].astype(BF16), NT_DIMS, preferred_element_type=F32)
        mx = jnp.maximum(jnp.max(s_p, axis=-1, keepdims=True), jnp.max(s_n, axis=-1, keepdims=True))
        p_p = jnp.exp(s_p - mx)
        p_n = jnp.exp(s_n - mx)
        den = jnp.sum(p_p, axis=-1, keepdims=True) + jnp.sum(p_n, axis=-1, keepdims=True)
        acc = jnp.dot(p_p.astype(BF16), vp, preferred_element_type=F32)
        acc += jnp.dot(p_n.astype(BF16), vn, preferred_element_type=F32)
        outs.append(acc / den)
    o = outs[0] - lam * outs[1]
    o_ref[...] = (_rms(o, sub_ref[...]) * (1.0 - lam_init)).astype(BF16)


def diff_attn_sample(proj, q_col0, k_col0, v_col0, row0, nb, seq_len, heads, d, k_cache, v_cache, layer,
                     lam_vecs, subln, lam_init, joint):
    hw = 2 * d
    past = k_cache.shape[2]
    assert row0 % seq_len == 0
    rb = row0 // seq_len
    qc, kc, vc = q_col0 // hw, k_col0 // hw, v_col0 // hw
    cache_spec = pl.BlockSpec((None, None, past, hw), lambda b, h: (layer, b, 0, h))
    in_specs, operands, aliases = _into(
        joint,
        [pl.BlockSpec((seq_len, hw), lambda b, h: (rb + b, qc + h)),
         pl.BlockSpec((seq_len, hw), lambda b, h: (rb + b, kc + h)),
         pl.BlockSpec((seq_len, hw), lambda b, h: (rb + b, vc + h)),
         cache_spec, cache_spec] + _lam_specs(d),
        [proj, proj, proj, k_cache, v_cache] + [v.reshape(1, d) for v in lam_vecs] + [subln.reshape(1, hw)])
    body = functools.partial(_diff_sample_kernel, d=d, scale=1.0 / math.sqrt(d), lam_init=lam_init)
    return pl.pallas_call(
        _drop_ref(body, 10),
        grid=(nb, heads),
        in_specs=in_specs,
        out_specs=pl.BlockSpec((seq_len, hw), lambda b, h: (rb + b, h)),
        out_shape=jax.ShapeDtypeStruct(joint.shape, BF16),
        input_output_aliases=aliases,
        compiler_params=_params("parallel", "parallel"),
    )(*operands)


def _xattn_kernel(q_ref, k_ref, v_ref, o_ref, *, heads, hd, scale):
    for h in range(heads):
        cols = slice(h * hd, (h + 1) * hd)
        qh = (q_ref[:, cols] * scale).astype(BF16)
        s = lax.dot_general(qh, k_ref[:, cols].astype(BF16), NT_DIMS, preferred_element_type=F32)
        pr = jnp.exp(s - jnp.max(s, axis=-1, keepdims=True))
        den = jnp.sum(pr, axis=-1, keepdims=True)
        o = jnp.dot(pr.astype(BF16), v_ref[:, cols].astype(BF16), preferred_element_type=F32)
        o_ref[:, cols] = (o / den).astype(BF16)


def cross_attn(q_all, row0, nb, seq_len, k_arr, v_arr, kv_specs, heads, hd, joint, *, tq_pref=512):
    width = heads * hd
    tq = _tile(seq_len, tq_pref, PACKED_ROWS)
    assert row0 % tq == 0
    blk0, per_seq = row0 // tq, seq_len // tq
    in_specs, operands, aliases = _into(
        joint, [pl.BlockSpec((tq, width), lambda b, t: (blk0 + b * per_seq + t, 0))] + kv_specs,
        [q_all, k_arr, v_arr])
    body = functools.partial(_xattn_kernel, heads=heads, hd=hd, scale=1.0 / math.sqrt(hd))
    return pl.pallas_call(
        _drop_ref(body, 3) if aliases else body,
        grid=(nb, per_seq),
        in_specs=in_specs,
        out_specs=pl.BlockSpec((tq, width), lambda b, t: (blk0 + b * per_seq + t, 0)),
        out_shape=jax.ShapeDtypeStruct((q_all.shape[0], width), BF16),
        input_output_aliases=aliases,
        compiler_params=_params("parallel", "parallel"),
    )(*operands)


def kernel(x_prompt, x_sample, state_ssm, state_ssm_conv, state_pool, cache_diff_k, cache_diff_v, cache_mem_k, cache_mem_v, state_ffn_conv, mem_prompt, norm_mix_pre, norm_mix_post, w_in, ssm_conv_w, ssm_conv_b, ssm_dt_bias, ssm_a_log, ssm_d, ssm_norm, w_br_ssm, pool_w, pool_scale, w_br_pool, diff_lq1, diff_lk1, diff_lq2, diff_lk2, diff_subln, w_br_diff, w_o, norm_xa_pre, norm_xa_post, norm_mem, xa_wq, xa_wk, xa_wv, xa_wo, norm_ffn_pre, norm_ffn_post, ffn_w_up, ffn_conv_w, ffn_conv_b, ffn_w_down):
    bp, lp, dm = x_prompt.shape
    bs, ls, _ = x_sample.shape
    depth = w_in.shape[0]
    tp, ts = bp * lp, bs * ls
    heads, p_dim, n_state = state_ssm.shape[2:]
    d_inner = heads * p_dim
    xbc_w = state_ssm_conv.shape[3]
    groups = (xbc_w - d_inner) // (2 * n_state)
    k_ssm = ssm_conv_w.shape[1]
    pool_hist, pool_width = state_pool.shape[2:]
    past, dheads, _, dh = cache_diff_k.shape[2:]
    diff_w = dheads * 2 * dh
    n_mem, xheads, xhd = cache_mem_k.shape[2:]
    xa_w = xheads * xhd
    d_ff = ffn_w_down.shape[1]
    k_ffn = ffn_conv_w.shape[1]
    ffp = _round_up(d_ff, 1024)
    assert lp >= pool_hist and ls >= pool_hist and min(lp, ls) >= max(k_ssm, k_ffn) - 1

    o_z, o_xbc, o_dt = 0, d_inner, d_inner + xbc_w
    o_u = o_dt + heads
    o_q = o_u + pool_width
    o_g = o_q + 3 * diff_w

    def seg(off, width):
        return w_in[:, :, off:off + width]

    w_main = jnp.concatenate(
        [seg(o_g, 3 * dm), seg(o_xbc, xbc_w), seg(o_z, d_inner), seg(o_q, 3 * diff_w), seg(o_u, pool_width)],
        axis=-1).astype(BF16)
    c_g, c_xbc = 0, 3 * dm
    c_z = c_xbc + xbc_w
    c_q = c_z + d_inner
    c_k, c_v = c_q + diff_w, c_q + 2 * diff_w
    c_u = c_q + 3 * diff_w
    dt_pad = _round_up(heads, LANES)
    w_dt = jnp.pad(seg(o_dt, heads), ((0, 0), (0, 0), (0, dt_pad - heads))).astype(BF16)
    w_bs, w_bp, w_bd = w_br_ssm.astype(BF16), w_br_pool.astype(BF16), w_br_diff.astype(BF16)
    w_out = w_o.astype(BF16)
    pool_wb = pool_w.astype(BF16)
    w_q = xa_wq.astype(BF16)
    w_kv = jnp.concatenate([xa_wk, xa_wv], axis=-1).astype(BF16)
    w_xo = xa_wo.astype(BF16)

    def pad_ff(a):
        pad = [(0, 0)] * (a.ndim - 1) + [(0, ffp - d_ff)]
        return jnp.concatenate([jnp.pad(a[..., :d_ff], pad), jnp.pad(a[..., d_ff:], pad)], axis=-1)

    w_up = pad_ff(ffn_w_up.astype(BF16))
    conv_w_ff = pad_ff(ffn_conv_w)
    conv_b_ff = pad_ff(ffn_conv_b)
    ffn_hist = pad_ff(state_ffn_conv)
    w_down = jnp.pad(ffn_w_down.astype(BF16), ((0, 0), (0, ffp - d_ff), (0, 0)))

    k_cache = cache_diff_k.reshape(depth, bs, past, diff_w)
    v_cache = cache_diff_v.reshape(depth, bs, past, diff_w)
    mem_k = cache_mem_k.reshape(depth, bs, n_mem, xa_w)
    mem_v = cache_mem_v.reshape(depth, bs, n_mem, xa_w)
    mem_rows = mem_prompt.reshape(bp * n_mem, dm)

    h = jnp.concatenate([x_prompt.reshape(tp, dm), x_sample.reshape(ts, dm)], axis=0)
    xn = rmsnorm_bf16(h, norm_mix_pre[0])
    zeros_p = lambda *shape: jnp.zeros((bp,) + shape, F32)
    q_ssd = _tile(lp, 128, LANES)
    nsub_ssd = _tile(lp // q_ssd, 4, 1)

    def tails(arr, col0, width, n):
        a_p = jnp.stack([arr[(b + 1) * lp - n:(b + 1) * lp, col0:col0 + width] for b in range(bp)])
        a_s = arr[tp:, col0:col0 + width].reshape(bs, ls, width)[:, ls - n:]
        return a_p, a_s

    def both(arr, col0, width):
        return arr[:tp, col0:col0 + width], arr[tp:, col0:col0 + width]

    outs = [[] for _ in range(14)]
    for l in range(depth):
        lam_init = 0.8 - 0.6 * math.exp(-0.3 * l)
        lam_vecs = (diff_lq1[l], diff_lk1[l], diff_lq2[l], diff_lk2[l])

        proj = mm(xn, w_main, l)
        dt_raw = mm(xn, w_dt, l)[:, :heads]
        y_ssm = y_pool = None
        h_last = []
        for (row0, nb, sl, conv_hist, h0, p_hist, pos_base, q_chunk, nsub) in (
                (0, bp, lp, zeros_p(k_ssm - 1, xbc_w), zeros_p(heads, p_dim, n_state),
                 zeros_p(pool_hist, pool_width), 0, q_ssd, nsub_ssd),
                (tp, bs, ls, state_ssm_conv[l], state_ssm[l], state_pool[l], past, ls, 1)):
            xbc = conv_silu(proj, c_xbc, xbc_w, row0, nb, sl, conv_hist, ssm_conv_w[l], ssm_conv_b[l])
            dt_g = dt_raw[row0:row0 + nb * sl].reshape(nb, sl, heads)
            y_ssm, hl = ssd_mix(xbc, proj, c_z, row0, nb, sl, dt_g, ssm_dt_bias[l], ssm_a_log[l], ssm_d[l],
                                ssm_norm[l], h0, y_ssm, groups=groups, n_state=n_state, q=q_chunk, nsub=nsub)
            h_last.append(hl)
            y_pool = pool_mix(proj, c_u, row0, nb, sl, p_hist, pos_base, pool_wb[l], pool_scale[l], y_pool)
        y_diff = diff_attn_prompt(proj, c_q, c_k, c_v, bp, lp, dheads, dh, lam_vecs, diff_subln[l], lam_init)
        y_diff = diff_attn_sample(proj, c_q, c_k, c_v, tp, bs, ls, dheads, dh, k_cache, v_cache, l,
                                  lam_vecs, diff_subln[l], lam_init, y_diff)
        merged = branch_merge(y_ssm, y_pool, y_diff, proj, c_g, w_bs, w_bp, w_bd, l)
        h, xn = mm_post(merged, w_out, l, h, norm_mix_post[l], norm_xa_pre[l], tn_pref=512, tk_pref=dm)

        mem_kv = mm(rmsnorm_bf16(mem_rows, norm_mem[l]), w_kv, l)
        q_xa = mm(xn, w_q, l)
        kv_p = [pl.BlockSpec((n_mem, xa_w), lambda b, t: (b, 0)), pl.BlockSpec((n_mem, xa_w), lambda b, t: (b, 1))]
        kv_s = [pl.BlockSpec((None, None, n_mem, xa_w), lambda b, t, l=l: (l, b, 0, 0))] * 2
        o_xa = cross_attn(q_xa, 0, bp, lp, mem_kv, mem_kv, kv_p, xheads, xhd, None)
        o_xa = cross_attn(q_xa, tp, bs, ls, mem_k, mem_v, kv_s, xheads, xhd, o_xa)
        h, xn = mm_post(o_xa, w_xo, l, h, norm_xa_post[l], norm_ffn_pre[l], tn_pref=dm, tk_pref=xa_w)

        hid, tail_g, tail_v = ffn_up_geglu(xn, bp, lp, w_up, l, ffp, zeros_p(k_ffn - 1, 2 * ffp),
                                           conv_w_ff[l], conv_b_ff[l])
        up_s = mm(xn, w_up, l, row0=tp, tm_pref=512)
        hid = conv_geglu(up_s, ffp, bs, ls, ffn_hist[l], conv_w_ff[l], conv_b_ff[l], hid, tp)
        g_next = norm_mix_pre[l + 1] if l + 1 < depth else None
        res = mm_post(hid, w_down, l, h, norm_ffn_post[l], g_next, tn_pref=1024, tk_pref=2816)
        h, xn = res if g_next is not None else (res, None)

        conv_p, conv_s = tails(proj, c_xbc, xbc_w, k_ssm - 1)
        pool_p, pool_s = tails(proj, c_u, pool_width, pool_hist)
        kk_p, kk_s = both(proj, c_k, diff_w)
        vv_p, vv_s = both(proj, c_v, diff_w)
        ups = up_s.reshape(bs, ls, 2 * ffp)[:, ls - (k_ffn - 1):]
        ffn_p = jnp.concatenate([tail_g[:, -(k_ffn - 1):, :d_ff], tail_v[:, -(k_ffn - 1):, :d_ff]], axis=-1)
        ffn_s = jnp.concatenate([ups[..., :d_ff], ups[..., ffp:ffp + d_ff]], axis=-1)
        layer_out = (h_last[0], h_last[1], conv_p, conv_s, pool_p, pool_s,
                     kk_p.reshape(bp, lp, dheads, 2, dh), kk_s.reshape(bs, ls, dheads, 2, dh),
                     vv_p.reshape(bp, lp, dheads, 2 * dh), vv_s.reshape(bs, ls, dheads, 2 * dh),
                     mem_kv[:, :xa_w].reshape(bp, n_mem, xheads, xhd),
                     mem_kv[:, xa_w:].reshape(bp, n_mem, xheads, xhd),
                     ffn_p, ffn_s)
        for acc, val in zip(outs, layer_out):
            acc.append(val)

    return (h[:tp].reshape(bp, lp, dm), h[tp:].reshape(bs, ls, dm)) + tuple(jnp.stack(o) for o in outs)
```

```python
import functools
import math

import jax
import jax.numpy as jnp
from jax import lax
from jax.experimental import pallas as pl
from jax.experimental.pallas import tpu as pltpu

F32 = jnp.float32
BF16 = jnp.bfloat16
EPS = 1e-6
CHUNK = 64
POOL_WINDOWS = (2, 4, 8, 16)
SUBLANES = 8
PACKED_ROWS = 16
LANES = 128
VMEM_LIMIT_BYTES = 56 * 1024 * 1024
NT_DIMS = (((1,), (1,)), ((), ()))
TN_DIMS = (((0,), (0,)), ((), ()))
SINGLE = pl.Buffered(1)
FFN_ROW_CHUNK = 64
NORM_UNROLL = 8


def _tile(n, pref, align):
    t = (min(pref, n) // align) * align
    while t >= align:
        if n % t == 0:
            return t
        t -= align
    return n


def _round_up(n, m):
    return (n + m - 1) // m * m


def _params(*sem):
    return pltpu.CompilerParams(dimension_semantics=sem, vmem_limit_bytes=VMEM_LIMIT_BYTES)


def _sigmoid(x):
    return 1.0 / (1.0 + jnp.exp(-x))


def _softplus(x):
    return jnp.maximum(x, 0.0) + jnp.log(1.0 + jnp.exp(-jnp.abs(x)))


def _gelu_tanh(x):
    return 0.5 * x * (1.0 + jnp.tanh(math.sqrt(2.0 / math.pi) * (x + 0.044715 * (x * x * x))))


def _rms(x, g):
    return x * lax.rsqrt(jnp.mean(x * x, axis=-1, keepdims=True) + EPS) * g


def _drop_ref(body, idx):
    def wrapped(*refs):
        return body(*refs[:idx], *refs[idx + 1:])
    return wrapped


def _into(joint, in_specs, operands):
    if joint is None:
        return in_specs, operands, {}
    return (in_specs + [pl.BlockSpec(memory_space=pl.ANY)], operands + [joint], {len(operands): 0})


def _norm_kernel(x_ref, g_ref, o_ref, *, rows):
    def body(r, carry):
        sl = pl.ds(pl.multiple_of(r * rows, rows), rows)
        o_ref[sl, :] = _rms(x_ref[sl, :], g_ref[...]).astype(BF16)
        return carry
    lax.fori_loop(0, x_ref.shape[0] // rows, body, 0, unroll=NORM_UNROLL)


def rmsnorm_bf16(x, g, *, tm_pref=256):
    t, d = x.shape
    tm = _tile(t, tm_pref, PACKED_ROWS)
    return pl.pallas_call(
        functools.partial(_norm_kernel, rows=PACKED_ROWS),
        grid=(t // tm,),
        in_specs=[pl.BlockSpec((tm, d), lambda i: (i, 0)), pl.BlockSpec((1, d), lambda i: (0, 0))],
        out_specs=pl.BlockSpec((tm, d), lambda i: (i, 0)),
        out_shape=jax.ShapeDtypeStruct((t, d), BF16),
        compiler_params=_params("parallel"),
    )(x, g.reshape(1, d))


def _mm_kernel(x_ref, w_ref, o_ref):
    o_ref[...] = jnp.dot(x_ref[...], w_ref[...], preferred_element_type=F32)


def mm(x, w_all, layer, *, row0=0, nrows=None, tm_pref=768, tn_pref=1024):
    k = x.shape[1]
    n = w_all.shape[2]
    nrows = x.shape[0] - row0 if nrows is None else nrows
    tm = _tile(math.gcd(nrows, row0) if row0 else nrows, tm_pref, PACKED_ROWS)
    tn = _tile(n, tn_pref, LANES)
    rb = row0 // tm
    return pl.pallas_call(
        _mm_kernel,
        grid=(nrows // tm, n // tn),
        in_specs=[pl.BlockSpec((tm, k), lambda i, j: (rb + i, 0)),
                  pl.BlockSpec((None, k, tn), lambda i, j: (layer, 0, j))],
        out_specs=pl.BlockSpec((tm, tn), lambda i, j: (i, j)),
        out_shape=jax.ShapeDtypeStruct((nrows, n), F32),
        compiler_params=_params("parallel", "parallel"),
    )(x, w_all)


def _mm_post_kernel(x_ref, w_ref, res_ref, g_ref, gn_ref, o_ref, *maybe_xn_ref, rows, tn, nk):
    j, kk = pl.program_id(1), pl.program_id(2)
    part = jnp.dot(x_ref[...], w_ref[...], preferred_element_type=F32)
    cols = pl.ds(pl.multiple_of(j * tn, tn), tn)
    if nk == 1:
        o_ref[:, cols] = part
    else:
        @pl.when(kk == 0)
        def _():
            o_ref[:, cols] = part

        @pl.when(kk > 0)
        def _():
            o_ref[:, cols] += part

    @pl.when((j == pl.num_programs(1) - 1) & (kk == nk - 1))
    def _():
        def body(r, carry):
            sl = pl.ds(pl.multiple_of(r * rows, rows), rows)
            h_new = res_ref[sl, :] + _rms(o_ref[sl, :], g_ref[...])
            o_ref[sl, :] = h_new
            for xn_ref in maybe_xn_ref:
                xn_ref[sl, :] = _rms(h_new, gn_ref[...]).astype(BF16)
            return carry
        lax.fori_loop(0, o_ref.shape[0] // rows, body, 0, unroll=NORM_UNROLL)


def mm_post(x, w_all, layer, res, g, g_next, *, tn_pref, tk_pref, tm_pref=512):
    t, k = x.shape
    d = w_all.shape[2]
    tm = _tile(t, tm_pref, PACKED_ROWS)
    tn = _tile(d, tn_pref, LANES)
    tk = _tile(k, tk_pref, LANES)
    nk = k // tk
    row_spec = pl.BlockSpec((tm, d), lambda i, j, kk: (i, 0))
    vec_spec = pl.BlockSpec((1, d), lambda i, j, kk: (0, 0))
    emit = g_next is not None
    out = pl.pallas_call(
        functools.partial(_mm_post_kernel, rows=PACKED_ROWS, tn=tn, nk=nk),
        grid=(t // tm, d // tn, nk),
        in_specs=[pl.BlockSpec((tm, tk), lambda i, j, kk: (i, kk)),
                  pl.BlockSpec((None, tk, tn), lambda i, j, kk: (layer, kk, j)),
                  pl.BlockSpec((tm, d), lambda i, j, kk: (i, 0), pipeline_mode=SINGLE),
                  vec_spec, vec_spec],
        out_specs=[row_spec, row_spec] if emit else [row_spec],
        out_shape=[jax.ShapeDtypeStruct((t, d), F32)] + ([jax.ShapeDtypeStruct((t, d), BF16)] if emit else []),
        compiler_params=_params("parallel", "arbitrary", "arbitrary"),
    )(x, w_all, res, g.reshape(1, d), (g_next if emit else g).reshape(1, d))
    return tuple(out) if emit else out[0]


def _merge_kernel(ys_ref, yp_ref, yd_ref, g0_ref, g1_ref, g2_ref, ws_ref, wp_ref, wd_ref, o_ref):
    acc = _sigmoid(g0_ref[...]) * jnp.dot(ys_ref[...], ws_ref[...], preferred_element_type=F32)
    acc += _sigmoid(g1_ref[...]) * jnp.dot(yp_ref[...], wp_ref[...], preferred_element_type=F32)
    acc += _sigmoid(g2_ref[...]) * jnp.dot(yd_ref[...], wd_ref[...], preferred_element_type=F32)
    o_ref[...] = acc.astype(BF16)


def branch_merge(y_ssm, y_pool, y_diff, proj, gate_col0, w_s, w_p, w_d, layer, *, tm_pref=512, tn_pref=1024):
    t = y_ssm.shape[0]
    d = w_s.shape[2]
    tm = _tile(t, tm_pref, PACKED_ROWS)
    tn = _tile(d, tn_pref, LANES)
    assert gate_col0 % tn == 0
    nj = d // tn
    gj = gate_col0 // tn

    def gate_spec(br):
        return pl.BlockSpec((tm, tn), lambda i, j: (i, gj + br * nj + j))

    def x_spec(kdim):
        return pl.BlockSpec((tm, kdim), lambda i, j: (i, 0))

    def w_spec(kdim):
        return pl.BlockSpec((None, kdim, tn), lambda i, j: (layer, 0, j))

    return pl.pallas_call(
        _merge_kernel,
        grid=(t // tm, nj),
        in_specs=[x_spec(y_ssm.shape[1]), x_spec(y_pool.shape[1]), x_spec(y_diff.shape[1]),
                  gate_spec(0), gate_spec(1), gate_spec(2),
                  w_spec(w_s.shape[1]), w_spec(w_p.shape[1]), w_spec(w_d.shape[1])],
        out_specs=pl.BlockSpec((tm, tn), lambda i, j: (i, j)),
        out_shape=jax.ShapeDtypeStruct((t, d), BF16),
        compiler_params=_params("parallel", "parallel"),
    )(y_ssm, y_pool, y_diff, proj, proj, proj, w_s, w_p, w_d)


def _conv_rows(ext_ref, u, halo, w_ref, b_ref, k):
    n = u.shape[0]
    ext_ref[0:SUBLANES, :] = halo
    ext_ref[SUBLANES:SUBLANES + n, :] = u
    acc = u * w_ref[k - 1:k, :]
    for i in range(k - 1):
        acc = acc + ext_ref[pl.ds(SUBLANES - (k - 1) + i, n), :] * w_ref[i:i + 1, :]
    return acc + b_ref[...]


def _halo(prev_ref, hist_ref, s, nseq):
    if nseq > 1:
        return hist_ref[s]
    return jnp.where(pl.program_id(1) == 0, hist_ref[0], prev_ref[...])


def _conv_silu_kernel(u_ref, prev_ref, hist_ref, w_ref, b_ref, o_ref, ext_ref, *, k, sl, nseq):
    for s in range(nseq):
        rows = slice(s * sl, (s + 1) * sl)
        y = _conv_rows(ext_ref, u_ref[rows, :], _halo(prev_ref, hist_ref, s, nseq), w_ref, b_ref, k)
        o_ref[rows, :] = y * _sigmoid(y)


def _seq_tiling(nb, seq_len, tl_pref, align):
    if seq_len >= tl_pref:
        return _tile(seq_len, tl_pref, align), 1
    nseq = _tile(nb, max(tl_pref // seq_len, 1), 1)
    return nseq * seq_len, nseq


def _seq_specs(row0, rows_per_b, tl, tc, col_blk0):
    blk0 = row0 // tl
    per_b = rows_per_b // tl
    sub0 = row0 // SUBLANES
    sub_per_b = rows_per_b // SUBLANES
    sub_per_tile = tl // SUBLANES
    cur = pl.BlockSpec((tl, tc), lambda b, t, c: (blk0 + b * per_b + t, col_blk0 + c))
    prev = pl.BlockSpec(
        (SUBLANES, tc),
        lambda b, t, c: (jnp.maximum(sub0 + b * sub_per_b + t * sub_per_tile - 1, 0), col_blk0 + c))
    return cur, prev


def _pad_hist(hist, rows):
    return jnp.pad(hist, ((0, 0), (rows - hist.shape[1], 0), (0, 0)))


def conv_silu(proj, col0, width, row0, nb, seq_len, hist, w, bias, *, tl_pref=512, tc_pref=512):
    k = w.shape[0]
    tl, nseq = _seq_tiling(nb, seq_len, tl_pref, SUBLANES)
    tc = _tile(width, tc_pref, LANES)
    rows_per_b = max(tl, seq_len)
    assert row0 % tl == 0 and col0 % tc == 0
    cur, prev = _seq_specs(row0, rows_per_b, tl, tc, col0 // tc)
    per_b = rows_per_b // tl
    return pl.pallas_call(
        functools.partial(_conv_silu_kernel, k=k, sl=tl // nseq, nseq=nseq),
        grid=(nb // nseq, per_b, width // tc),
        in_specs=[cur, prev,
                  pl.BlockSpec((nseq, SUBLANES, tc), lambda b, t, c: (b, 0, c)),
                  pl.BlockSpec((k, tc), lambda b, t, c: (0, c)),
                  pl.BlockSpec((1, tc), lambda b, t, c: (0, c))],
        out_specs=pl.BlockSpec((tl, tc), lambda b, t, c: (b * per_b + t, c)),
        out_shape=jax.ShapeDtypeStruct((nb * seq_len, width), F32),
        scratch_shapes=[pltpu.VMEM((SUBLANES + tl // nseq, tc), F32)],
        compiler_params=_params("parallel", "parallel", "parallel"),
    )(proj, proj, _pad_hist(hist, SUBLANES), w, bias.reshape(1, width))


def _conv_geglu_kernel(ug_ref, pg_ref, hg_ref, wg_ref, bg_ref, uv_ref, pv_ref, hv_ref, wv_ref, bv_ref,
                       o_ref, extg_ref, extv_ref, *, k, sl, nseq):
    for s in range(nseq):
        rows = slice(s * sl, (s + 1) * sl)
        gate = _conv_rows(extg_ref, ug_ref[rows, :], _halo(pg_ref, hg_ref, s, nseq), wg_ref, bg_ref, k)
        val = _conv_rows(extv_ref, uv_ref[rows, :], _halo(pv_ref, hv_ref, s, nseq), wv_ref, bv_ref, k)
        o_ref[rows, :] = (_gelu_tanh(gate) * val).astype(BF16)


def conv_geglu(up, half, nb, seq_len, hist, w, bias, joint, out_row0, *, tl_pref=512, tc_pref=1024):
    k = w.shape[0]
    tl, nseq = _seq_tiling(nb, seq_len, tl_pref, PACKED_ROWS)
    tc = _tile(half, tc_pref, LANES)
    rows_per_b = max(tl, seq_len)
    per_b = rows_per_b // tl
    ncb = half // tc
    assert out_row0 % tl == 0
    ob = out_row0 // tl
    cur_g, prev_g = _seq_specs(0, rows_per_b, tl, tc, 0)
    cur_v, prev_v = _seq_specs(0, rows_per_b, tl, tc, ncb)
    hist8 = _pad_hist(hist, SUBLANES)
    bias2 = bias.reshape(1, 2 * half)

    def side(off):
        return [pl.BlockSpec((nseq, SUBLANES, tc), lambda b, t, c: (b, 0, off + c)),
                pl.BlockSpec((k, tc), lambda b, t, c: (0, off + c)),
                pl.BlockSpec((1, tc), lambda b, t, c: (0, off + c))]

    in_specs, operands, aliases = _into(
        joint, [cur_g, prev_g] + side(0) + [cur_v, prev_v] + side(ncb),
        [up, up, hist8, w, bias2, up, up, hist8, w, bias2])
    body = functools.partial(_conv_geglu_kernel, k=k, sl=tl // nseq, nseq=nseq)
    return pl.pallas_call(
        _drop_ref(body, 10) if aliases else body,
        grid=(nb // nseq, per_b, ncb),
        in_specs=in_specs,
        out_specs=pl.BlockSpec((tl, tc), lambda b, t, c: (ob + b * per_b + t, c)),
        out_shape=jax.ShapeDtypeStruct(joint.shape, BF16),
        input_output_aliases=aliases,
        scratch_shapes=[pltpu.VMEM((SUBLANES + tl // nseq, tc), F32)] * 2,
        compiler_params=_params("parallel", "parallel", "parallel"),
    )(*operands)


def _ffn_up_kernel(xc_ref, xp_ref, wg_ref, wv_ref, hg_ref, hv_ref, cwg_ref, cbg_ref, cwv_ref, cbv_ref,
                   hid_ref, tg_ref, tv_ref, xe_ref, eg_ref, ev_ref, gg_ref, *, k, tm, tiles_per_seq):
    halo = PACKED_ROWS

    @pl.when(pl.program_id(1) == 0)
    def _():
        xe_ref[0:halo, :] = xp_ref[...]
        xe_ref[halo:halo + tm, :] = xc_ref[...]

    first = pl.program_id(0) % tiles_per_seq == 0

    def project(w_ref, h_ref, e_ref, t_ref):
        e_ref[...] = jnp.dot(xe_ref[...], w_ref[...], preferred_element_type=F32)
        e_ref[0:halo, :] = jnp.where(first, h_ref[...], e_ref[0:halo, :])
        t_ref[...] = e_ref[tm:tm + halo, :]

    def conv(e_ref, cw_ref, cb_ref, r0, n):
        acc = e_ref[halo + r0:halo + r0 + n, :] * cw_ref[k - 1:k, :]
        for i in range(k - 1):
            acc = acc + e_ref[pl.ds(halo + r0 - (k - 1) + i, n), :] * cw_ref[i:i + 1, :]
        return acc + cb_ref[...]

    chunk = _tile(tm, FFN_ROW_CHUNK, PACKED_ROWS)
    project(wg_ref, hg_ref, eg_ref, tg_ref)
    for r0 in range(0, tm, chunk):
        gg_ref[r0:r0 + chunk, :] = _gelu_tanh(conv(eg_ref, cwg_ref, cbg_ref, r0, chunk))
    project(wv_ref, hv_ref, ev_ref, tv_ref)
    for r0 in range(0, tm, chunk):
        val = conv(ev_ref, cwv_ref, cbv_ref, r0, chunk)
        hid_ref[r0:r0 + chunk, :] = (gg_ref[r0:r0 + chunk, :] * val).astype(BF16)


def ffn_up_geglu(xn, nb, seq_len, w_up_all, layer, half, hist, conv_w, conv_b, *, tm_pref=1024, tf_pref=512):
    t, d = xn.shape
    k = conv_w.shape[0]
    halo = PACKED_ROWS
    tm = _tile(seq_len, tm_pref, halo)
    tf = _tile(half, tf_pref, LANES)
    tiles_per_seq = seq_len // tm
    ncb = half // tf
    hist16 = _pad_hist(hist, halo)
    bias2 = conv_b.reshape(1, 2 * half)

    def w_spec(off):
        return pl.BlockSpec((None, d, tf), lambda i, j: (layer, 0, off + j))

    def hist_spec(off):
        return pl.BlockSpec((None, halo, tf), lambda i, j: (i // tiles_per_seq, 0, off + j))

    def cw_spec(off):
        return pl.BlockSpec((k, tf), lambda i, j: (0, off + j))

    def cb_spec(off):
        return pl.BlockSpec((1, tf), lambda i, j: (0, off + j))

    tail_spec = pl.BlockSpec((None, halo, tf), lambda i, j: (i, 0, j))
    tail_shape = jax.ShapeDtypeStruct((nb * tiles_per_seq, halo, half), F32)
    hid, tail_g, tail_v = pl.pallas_call(
        functools.partial(_ffn_up_kernel, k=k, tm=tm, tiles_per_seq=tiles_per_seq),
        grid=(nb * tiles_per_seq, ncb),
        in_specs=[pl.BlockSpec((tm, d), lambda i, j: (i, 0), pipeline_mode=SINGLE),
                  pl.BlockSpec((halo, d), lambda i, j: (jnp.maximum(i * (tm // halo) - 1, 0), 0)),
                  w_spec(0), w_spec(ncb), hist_spec(0), hist_spec(ncb),
                  cw_spec(0), cb_spec(0), cw_spec(ncb), cb_spec(ncb)],
        out_specs=[pl.BlockSpec((tm, tf), lambda i, j: (i, j)), tail_spec, tail_spec],
        out_shape=[jax.ShapeDtypeStruct((t, half), BF16), tail_shape, tail_shape],
        scratch_shapes=[pltpu.VMEM((halo + tm, d), BF16), pltpu.VMEM((halo + tm, tf), F32),
                        pltpu.VMEM((halo + tm, tf), F32), pltpu.VMEM((tm, tf), F32)],
        compiler_params=_params("parallel", "arbitrary"),
    )(xn, xn, w_up_all, w_up_all, hist16, hist16, conv_w, bias2, conv_w, bias2)
    last = slice(tiles_per_seq - 1, None, tiles_per_seq)
    return hid, tail_g[last], tail_v[last]


def _pool_kernel(u_ref, prev_ref, hist_ref, pw_ref, ps_ref, o_ref, ext_ref, *, tl, halo, pos_base, gd):
    t = pl.program_id(1)
    ext_ref[0:halo, :] = jnp.where(t == 0, hist_ref[...], prev_ref[...])
    ext_ref[halo:halo + tl, :] = u_ref[...]
    pos = pos_base + t * tl + lax.broadcasted_iota(jnp.int32, (tl, 1), 0)
    for gi, win in enumerate(POOL_WINDOWS):
        cols = slice(gi * gd, (gi + 1) * gd)
        cur = u_ref[:, cols]
        wsum = cur
        for i in range(1, win):
            wsum = wsum + ext_ref[pl.ds(halo - i, tl), cols]
        cnt = jnp.minimum(pos + 1, win).astype(F32)
        pooled = wsum / cnt - cur
        y = jnp.dot(pooled.astype(BF16), pw_ref[gi], preferred_element_type=F32) * ps_ref[:, cols]
        o_ref[:, cols] = y.astype(BF16)


def pool_mix(proj, col0, row0, nb, seq_len, hist, pos_base, pool_w, pool_scale, joint, *, tl_pref=512):
    ng, gd, _ = pool_w.shape
    width = ng * gd
    halo = 2 * SUBLANES
    assert ng == len(POOL_WINDOWS) and max(POOL_WINDOWS) <= halo
    tl = _tile(seq_len, tl_pref, halo)
    assert row0 % tl == 0 and col0 % width == 0
    blk0, per_seq = row0 // tl, seq_len // tl
    h0, h_per_seq, h_per_tile = row0 // halo, seq_len // halo, tl // halo
    cb = col0 // width
    in_specs, operands, aliases = _into(
        joint,
        [pl.BlockSpec((tl, width), lambda b, t: (blk0 + b * per_seq + t, cb)),
         pl.BlockSpec((halo, width),
                      lambda b, t: (jnp.maximum(h0 + b * h_per_seq + t * h_per_tile - 1, 0), cb)),
         pl.BlockSpec((None, halo, width), lambda b, t: (b, 0, 0)),
         pl.BlockSpec((ng, gd, gd), lambda b, t: (0, 0, 0)),
         pl.BlockSpec((1, width), lambda b, t: (0, 0))],
        [proj, proj, _pad_hist(hist, halo), pool_w, pool_scale.reshape(1, width)])
    body = functools.partial(_pool_kernel, tl=tl, halo=halo, pos_base=pos_base, gd=gd)
    return pl.pallas_call(
        _drop_ref(body, 5) if aliases else body,
        grid=(nb, per_seq),
        in_specs=in_specs,
        out_specs=pl.BlockSpec((tl, width), lambda b, t: (blk0 + b * per_seq + t, 0)),
        out_shape=jax.ShapeDtypeStruct((proj.shape[0], width), BF16),
        input_output_aliases=aliases,
        scratch_shapes=[pltpu.VMEM((halo + tl, width), F32)],
        compiler_params=_params("parallel", "parallel"),
    )(*operands)


def _ssd_kernel(xs_ref, b_ref, c_ref, z_ref, dtc_ref, dtr_ref, biasc_ref, biasr_ref, alogc_ref, alogr_ref,
                dskip_ref, nw_ref, h0_ref, y_ref, hl_ref, ht_ref, *, q, nsub, hpg, p):
    ti = pl.program_id(2)
    gw = hpg * p
    n = b_ref.shape[1]
    exact = dict(precision=lax.Precision.HIGHEST, preferred_element_type=F32)

    @pl.when(ti == 0)
    def _():
        ht_ref[...] = h0_ref[...].reshape(gw, n).T

    dt_c_all = _softplus(dtc_ref[...] + biasc_ref[...])
    dt_r_all = _softplus(dtr_ref[...] + biasr_ref[...])
    a_c = -jnp.exp(alogc_ref[...])
    a_r = -jnp.exp(alogr_ref[...])
    row = lax.broadcasted_iota(jnp.int32, (q, q), 0)
    col = lax.broadcasted_iota(jnp.int32, (q, q), 1)
    lower = row >= col
    tril = lower.astype(F32)
    triu = (row <= col).astype(F32)

    def spread(width, rep):
        head = lax.broadcasted_iota(jnp.int32, (hpg, width), 0)
        lane = lax.broadcasted_iota(jnp.int32, (hpg, width), 1)
        return (lane // rep == head).astype(F32)

    to_lanes = spread(gw, p)
    to_keys = spread(hpg * q, q)
    lane_head = lax.broadcasted_iota(jnp.int32, (1, gw), 1) // p
    d_skip = jnp.dot(dskip_ref[...], to_lanes, **exact)
    ht = ht_ref[...]
    for s in range(nsub):
        rows = slice(s * q, (s + 1) * q)
        dt_c = dt_c_all[rows, :]
        dt_r = dt_r_all[:, rows]
        acum_c = jnp.dot(tril, dt_c * a_c, **exact)
        acum_r = jnp.dot(dt_r * a_r, triu, **exact)
        a_last = acum_c[q - 1:q, :]
        per_head = jnp.concatenate([jnp.exp(acum_c), jnp.exp(a_last - acum_c) * dt_c], axis=0)
        per_lane = jnp.dot(per_head, to_lanes, **exact)
        exp_a, w_end = per_lane[:q], per_lane[q:]
        chunk_decay = jnp.dot(jnp.exp(a_last), to_lanes, **exact)
        ac_keys = jnp.dot(acum_c, to_keys, **exact)
        x = xs_ref[rows, :]
        bm = b_ref[rows, :].astype(BF16)
        cm = c_ref[rows, :].astype(BF16)
        cb = lax.dot_general(cm, bm, NT_DIMS, preferred_element_type=F32)
        wmats = []
        for r in range(hpg):
            seg = ac_keys[:, r * q:(r + 1) * q] - acum_r[r:r + 1, :]
            decay = jnp.exp(jnp.where(lower, seg, -jnp.inf))
            wmats.append((cb * decay * dt_r[r:r + 1, :]).astype(BF16))
        y_all = jnp.dot(jnp.concatenate(wmats, axis=0), x.astype(BF16), preferred_element_type=F32)
        y = y_all[:q]
        for r in range(1, hpg):
            y = jnp.where(lane_head == r, y_all[r * q:(r + 1) * q], y)
        y = y + jnp.dot(cm, ht.astype(BF16), preferred_element_type=F32) * exp_a
        s_new = lax.dot_general(bm, (x * w_end).astype(BF16), TN_DIMS, preferred_element_type=F32)
        ht = ht * chunk_decay + s_new
        zt = z_ref[rows, :]
        yg = (y + x * d_skip) * (zt * _sigmoid(zt))
        y_ref[rows, :] = _rms(yg, nw_ref[...]).astype(BF16)
    ht_ref[...] = ht

    @pl.when(ti == pl.num_programs(2) - 1)
    def _():
        hl_ref[...] = ht.T.reshape(hpg, p, n)


def ssd_mix(xbc, proj, z_col0, row0, nb, seq_len, dt_raw, dt_bias, a_log, d_skip, norm_w, h0, joint, *,
            groups, n_state, q, nsub):
    heads, p = h0.shape[1], h0.shape[2]
    hpg = heads // groups
    gw = hpg * p
    d_inner = heads * p
    tl = q * nsub
    nt = seq_len // tl
    assert seq_len % tl == 0 and row0 % tl == 0 and z_col0 % gw == 0 and d_inner % n_state == 0
    dt_g = dt_raw.reshape(nb, seq_len, groups, hpg)
    dt_col = dt_g.transpose(0, 2, 1, 3)
    dt_row = dt_g.transpose(0, 2, 3, 1)
    blk0 = row0 // tl
    zc = z_col0 // gw
    bcol = d_inner // n_state

    def per_group(arr, shape):
        return arr.reshape((groups,) + shape), pl.BlockSpec((None,) + shape, lambda b, g, t: (g, 0, 0))

    biasc, biasc_spec = per_group(dt_bias, (1, hpg))
    biasr, biasr_spec = per_group(dt_bias, (hpg, 1))
    alogc, alogc_spec = per_group(a_log, (1, hpg))
    alogr, alogr_spec = per_group(a_log, (hpg, 1))
    dsk, dsk_spec = per_group(d_skip, (1, hpg))
    h_spec = pl.BlockSpec((None, hpg, p, n_state), lambda b, g, t: (b, g, 0, 0))
    in_specs, operands, aliases = _into(
        joint,
        [pl.BlockSpec((tl, gw), lambda b, g, t: (b * nt + t, g)),
         pl.BlockSpec((tl, n_state), lambda b, g, t: (b * nt + t, bcol + g)),
         pl.BlockSpec((tl, n_state), lambda b, g, t: (b * nt + t, bcol + groups + g)),
         pl.BlockSpec((tl, gw), lambda b, g, t: (blk0 + b * nt + t, zc + g)),
         pl.BlockSpec((None, None, tl, hpg), lambda b, g, t: (b, g, t, 0)),
         pl.BlockSpec((None, None, hpg, tl), lambda b, g, t: (b, g, 0, t)),
         biasc_spec, biasr_spec, alogc_spec, alogr_spec, dsk_spec,
         pl.BlockSpec((1, gw), lambda b, g, t: (0, g)),
         h_spec],
        [xbc, xbc, xbc, proj, dt_col, dt_row, biasc, biasr, alogc, alogr, dsk,
         norm_w.reshape(1, d_inner), h0])
    body = functools.partial(_ssd_kernel, q=q, nsub=nsub, hpg=hpg, p=p)
    return pl.pallas_call(
        _drop_ref(body, 13) if aliases else body,
        grid=(nb, groups, nt),
        in_specs=in_specs,
        out_specs=[pl.BlockSpec((tl, gw), lambda b, g, t: (blk0 + b * nt + t, g)), h_spec],
        out_shape=[jax.ShapeDtypeStruct((proj.shape[0], d_inner), BF16),
                   jax.ShapeDtypeStruct(h0.shape, F32)],
        input_output_aliases=aliases,
        scratch_shapes=[pltpu.VMEM((n_state, gw), F32)],
        compiler_params=_params("parallel", "parallel", "arbitrary"),
    )(*operands)


def _lambda(lq1_ref, lk1_ref, lq2_ref, lk2_ref, lam_init):
    s1 = jnp.sum(lq1_ref[...] * lk1_ref[...], axis=-1, keepdims=True)
    s2 = jnp.sum(lq2_ref[...] * lk2_ref[...], axis=-1, keepdims=True)
    return jnp.exp(s1) - jnp.exp(s2) + lam_init


def _fold_lanes(a, op):
    if a.shape[1] % LANES:
        return a
    parts = [a[:, i:i + LANES] for i in range(0, a.shape[1], LANES)]
    while len(parts) > 1:
        parts = [op(parts[i], parts[i + 1]) for i in range(0, len(parts) - 1, 2)] + parts[len(parts) & ~1:]
    return parts[0]


def _diff_prompt_kernel(q_ref, k_ref, v_ref, lq1_ref, lk1_ref, lq2_ref, lk2_ref, sub_ref, o_ref,
                        s_ref, acc_ref, *, t, tk, d, scale, lam_init):
    iq = pl.program_id(2)
    lam = _lambda(lq1_ref, lk1_ref, lq2_ref, lk2_ref, lam_init)
    n_full = (iq * t) // tk
    q_chunk = (iq * t + lax.broadcasted_iota(jnp.int32, (t, 1), 0)) // CHUNK
    k_chunk = (n_full * tk + lax.broadcasted_iota(jnp.int32, (1, tk), 1)) // CHUNK
    visible = k_chunk <= q_chunk
    width = LANES if tk % LANES == 0 else tk

    def tile(j):
        return pl.ds(pl.multiple_of(j * tk, tk), tk)

    outs = []
    for m in range(2):
        cols = slice(m * d, (m + 1) * d)
        qm = (q_ref[:, cols] * scale).astype(BF16)

        def scores(j, qm=qm, cols=cols):
            kt = k_ref[tile(j), cols].astype(BF16)
            return lax.dot_general(qm, kt, NT_DIMS, preferred_element_type=F32)

        def pass1(j, mx, scores=scores):
            s = scores(j)
            s_ref[:, tile(j)] = s
            return jnp.maximum(mx, _fold_lanes(s, jnp.maximum))

        mx = lax.fori_loop(0, n_full, pass1, jnp.full((t, width), -jnp.inf, F32))
        s_last = jnp.where(visible, scores(n_full), -jnp.inf)
        s_ref[:, tile(n_full)] = s_last
        mx = jnp.max(jnp.maximum(mx, _fold_lanes(s_last, jnp.maximum)), axis=-1, keepdims=True)
        acc_ref[...] = jnp.zeros(acc_ref.shape, F32)

        def pass2(j, den, mx=mx):
            pr = jnp.exp(s_ref[:, tile(j)] - mx)
            vt = v_ref[tile(j), :].astype(BF16)
            acc_ref[...] += jnp.dot(pr.astype(BF16), vt, preferred_element_type=F32)
            return den + _fold_lanes(pr, jnp.add)

        den = lax.fori_loop(0, n_full + 1, pass2, jnp.zeros((t, width), F32))
        outs.append(acc_ref[...] / jnp.sum(den, axis=-1, keepdims=True))
    o = outs[0] - lam * outs[1]
    o_ref[...] = (_rms(o, sub_ref[...]) * (1.0 - lam_init)).astype(BF16)


def _lam_specs(d):
    fixed = lambda *_: (0, 0)
    return [pl.BlockSpec((1, d), fixed)] * 4 + [pl.BlockSpec((1, 2 * d), fixed)]


def diff_attn_prompt(proj, q_col0, k_col0, v_col0, nb, seq_len, heads, d, lam_vecs, subln, lam_init, *,
                     tq_pref=256):
    hw = 2 * d
    tq = _tile(seq_len, tq_pref, CHUNK)
    nq = seq_len // tq
    tk = 2 * tq if nq % 2 == 0 else tq
    assert q_col0 % hw == 0 and k_col0 % hw == 0 and v_col0 % hw == 0
    qc, kc, vc = q_col0 // hw, k_col0 // hw, v_col0 // hw
    return pl.pallas_call(
        functools.partial(_diff_prompt_kernel, t=tq, tk=tk, d=d, scale=1.0 / math.sqrt(d), lam_init=lam_init),
        grid=(nb, heads, nq),
        in_specs=[pl.BlockSpec((tq, hw), lambda b, h, i: (b * nq + i, qc + h)),
                  pl.BlockSpec((seq_len, hw), lambda b, h, i: (b, kc + h)),
                  pl.BlockSpec((seq_len, hw), lambda b, h, i: (b, vc + h))] + _lam_specs(d),
        out_specs=pl.BlockSpec((tq, hw), lambda b, h, i: (b * nq + i, h)),
        out_shape=jax.ShapeDtypeStruct((proj.shape[0], heads * hw), BF16),
        scratch_shapes=[pltpu.VMEM((tq, seq_len), F32), pltpu.VMEM((tq, hw), F32)],
        compiler_params=_params("parallel", "parallel", "parallel"),
    )(proj, proj, proj, *[v.reshape(1, d) for v in lam_vecs], subln.reshape(1, hw))


def _diff_sample_kernel(q_ref, kn_ref, vn_ref, kp_ref, vp_ref, lq1_ref, lk1_ref, lq2_ref, lk2_ref, sub_ref,
                        o_ref, *, d, scale, lam_init):
    lam = _lambda(lq1_ref, lk1_ref, lq2_ref, lk2_ref, lam_init)
    vp = vp_ref[...].astype(BF16)
    vn = vn_ref[...].astype(BF16)
    outs = []
    for m in range(2):
        cols = slice(m * d, (m + 1) * d)
        qm = (q_ref[:, cols] * scale).astype(BF16)
        s_p = lax.dot_general(qm, kp_ref[:, cols].astype(BF16), NT_DIMS, preferred_element_type=F32)
        s_n = lax.dot_general(qm, kn_ref[:, cols].astype(BF16), NT_DIMS, preferred_element_type=F32)
        mx = jnp.maximum(jnp.max(s_p, axis=-1, keepdims=True), jnp.max(s_n, axis=-1, keepdims=True))
        p_p = jnp.exp(s_p - mx)
        p_n = jnp.exp(s_n - mx)
        den = jnp.sum(p_p, axis=-1, keepdims=True) + jnp.sum(p_n, axis=-1, keepdims=True)
        acc = jnp.dot(p_p.astype(BF16), vp, preferred_element_type=F32)
        acc += jnp.dot(p_n.astype(BF16), vn, preferred_element_type=F32)
        outs.append(acc / den)
    o = outs[0] - lam * outs[1]
    o_ref[...] = (_rms(o, sub_ref[...]) * (1.0 - lam_init)).astype(BF16)


def diff_attn_sample(proj, q_col0, k_col0, v_col0, row0, nb, seq_len, heads, d, k_cache, v_cache, layer,
                     lam_vecs, subln, lam_init, joint):
    hw = 2 * d
    past = k_cache.shape[2]
    assert row0 % seq_len == 0
    rb = row0 // seq_len
    qc, kc, vc = q_col0 // hw, k_col0 // hw, v_col0 // hw
    cache_spec = pl.BlockSpec((None, None, past, hw), lambda b, h: (layer, b, 0, h))
    in_specs, operands, aliases = _into(
        joint,
        [pl.BlockSpec((seq_len, hw), lambda b, h: (rb + b, qc + h)),
         pl.BlockSpec((seq_len, hw), lambda b, h: (rb + b, kc + h)),
         pl.BlockSpec((seq_len, hw), lambda b, h: (rb + b, vc + h)),
         cache_spec, cache_spec] + _lam_specs(d),
        [proj, proj, proj, k_cache, v_cache] + [v.reshape(1, d) for v in lam_vecs] + [subln.reshape(1, hw)])
    body = functools.partial(_diff_sample_kernel, d=d, scale=1.0 / math.sqrt(d), lam_init=lam_init)
    return pl.pallas_call(
        _drop_ref(body, 10),
        grid=(nb, heads),
        in_specs=in_specs,
        out_specs=pl.BlockSpec((seq_len, hw), lambda b, h: (rb + b, h)),
        out_shape=jax.ShapeDtypeStruct(joint.shape, BF16),
        input_output_aliases=aliases,
        compiler_params=_params("parallel", "parallel"),
    )(*operands)


def _xattn_kernel(q_ref, k_ref, v_ref, o_ref, *, heads, hd, scale):
    for h in range(heads):
        cols = slice(h * hd, (h + 1) * hd)
        qh = (q_ref[:, cols] * scale).astype(BF16)
        s = lax.dot_general(qh, k_ref[:, cols].astype(BF16), NT_DIMS, preferred_element_type=F32)
        pr = jnp.exp(s - jnp.max(s, axis=-1, keepdims=True))
        den = jnp.sum(pr, axis=-1, keepdims=True)
        o = jnp.dot(pr.astype(BF16), v_ref[:, cols].astype(BF16), preferred_element_type=F32)
        o_ref[:, cols] = (o / den).astype(BF16)


def cross_attn(q_all, row0, nb, seq_len, k_arr, v_arr, kv_specs, heads, hd, joint, *, tq_pref=512):
    width = heads * hd
    tq = _tile(seq_len, tq_pref, PACKED_ROWS)
    assert row0 % tq == 0
    blk0, per_seq = row0 // tq, seq_len // tq
    in_specs, operands, aliases = _into(
        joint, [pl.BlockSpec((tq, width), lambda b, t: (blk0 + b * per_seq + t, 0))] + kv_specs,
        [q_all, k_arr, v_arr])
    body = functools.partial(_xattn_kernel, heads=heads, hd=hd, scale=1.0 / math.sqrt(hd))
    return pl.pallas_call(
        _drop_ref(body, 3) if aliases else body,
        grid=(nb, per_seq),
        in_specs=in_specs,
        out_specs=pl.BlockSpec((tq, width), lambda b, t: (blk0 + b * per_seq + t, 0)),
        out_shape=jax.ShapeDtypeStruct((q_all.shape[0], width), BF16),
        input_output_aliases=aliases,
        compiler_params=_params("parallel", "parallel"),
    )(*operands)


def kernel(x_prompt, x_sample, state_ssm, state_ssm_conv, state_pool, cache_diff_k, cache_diff_v, cache_mem_k, cache_mem_v, state_ffn_conv, mem_prompt, norm_mix_pre, norm_mix_post, w_in, ssm_conv_w, ssm_conv_b, ssm_dt_bias, ssm_a_log, ssm_d, ssm_norm, w_br_ssm, pool_w, pool_scale, w_br_pool, diff_lq1, diff_lk1, diff_lq2, diff_lk2, diff_subln, w_br_diff, w_o, norm_xa_pre, norm_xa_post, norm_mem, xa_wq, xa_wk, xa_wv, xa_wo, norm_ffn_pre, norm_ffn_post, ffn_w_up, ffn_conv_w, ffn_conv_b, ffn_w_down):
    bp, lp, dm = x_prompt.shape
    bs, ls, _ = x_sample.shape
    depth = w_in.shape[0]
    tp, ts = bp * lp, bs * ls
    heads, p_dim, n_state = state_ssm.shape[2:]
    d_inner = heads * p_dim
    xbc_w = state_ssm_conv.shape[3]
    groups = (xbc_w - d_inner) // (2 * n_state)
    k_ssm = ssm_conv_w.shape[1]
    pool_hist, pool_width = state_pool.shape[2:]
    past, dheads, _, dh = cache_diff_k.shape[2:]
    diff_w = dheads * 2 * dh
    n_mem, xheads, xhd = cache_mem_k.shape[2:]
    xa_w = xheads * xhd
    d_ff = ffn_w_down.shape[1]
    k_ffn = ffn_conv_w.shape[1]
    ffp = _round_up(d_ff, 1024)
    assert lp >= pool_hist and ls >= pool_hist and min(lp, ls) >= max(k_ssm, k_ffn) - 1

    o_z, o_xbc, o_dt = 0, d_inner, d_inner + xbc_w
    o_u = o_dt + heads
    o_q = o_u + pool_width
    o_g = o_q + 3 * diff_w

    def seg(off, width):
        return w_in[:, :, off:off + width]

    w_main = jnp.concatenate(
        [seg(o_g, 3 * dm), seg(o_xbc, xbc_w), seg(o_z, d_inner), seg(o_q, 3 * diff_w), seg(o_u, pool_width)],
        axis=-1).astype(BF16)
    c_g, c_xbc = 0, 3 * dm
    c_z = c_xbc + xbc_w
    c_q = c_z + d_inner
    c_k, c_v = c_q + diff_w, c_q + 2 * diff_w
    c_u = c_q + 3 * diff_w
    dt_pad = _round_up(heads, LANES)
    w_dt = jnp.pad(seg(o_dt, heads), ((0, 0), (0, 0), (0, dt_pad - heads))).astype(BF16)
    w_bs, w_bp, w_bd = w_br_ssm.astype(BF16), w_br_pool.astype(BF16), w_br_diff.astype(BF16)
    w_out = w_o.astype(BF16)
    pool_wb = pool_w.astype(BF16)
    w_q = xa_wq.astype(BF16)
    w_kv = jnp.concatenate([xa_wk, xa_wv], axis=-1).astype(BF16)
    w_xo = xa_wo.astype(BF16)

    def pad_ff(a):
        pad = [(0, 0)] * (a.ndim - 1) + [(0, ffp - d_ff)]
        return jnp.concatenate([jnp.pad(a[..., :d_ff], pad), jnp.pad(a[..., d_ff:], pad)], axis=-1)

    w_up = pad_ff(ffn_w_up.astype(BF16))
    conv_w_ff = pad_ff(ffn_conv_w)
    conv_b_ff = pad_ff(ffn_conv_b)
    ffn_hist = pad_ff(state_ffn_conv)
    w_down = jnp.pad(ffn_w_down.astype(BF16), ((0, 0), (0, ffp - d_ff), (0, 0)))

    k_cache = cache_diff_k.reshape(depth, bs, past, diff_w)
    v_cache = cache_diff_v.reshape(depth, bs, past, diff_w)
    mem_k = cache_mem_k.reshape(depth, bs, n_mem, xa_w)
    mem_v = cache_mem_v.reshape(depth, bs, n_mem, xa_w)
    mem_rows = mem_prompt.reshape(bp * n_mem, dm)

    h = jnp.concatenate([x_prompt.reshape(tp, dm), x_sample.reshape(ts, dm)], axis=0)
    xn = rmsnorm_bf16(h, norm_mix_pre[0])
    zeros_p = lambda *shape: jnp.zeros((bp,) + shape, F32)
    q_ssd = _tile(lp, 128, LANES)
    nsub_ssd = _tile(lp // q_ssd, 4, 1)

    def tails(arr, col0, width, n):
        a_p = jnp.stack([arr[(b + 1) * lp - n:(b + 1) * lp, col0:col0 + width] for b in range(bp)])
        a_s = arr[tp:, col0:col0 + width].reshape(bs, ls, width)[:, ls - n:]
        return a_p, a_s

    def both(arr, col0, width):
        return arr[:tp, col0:col0 + width], arr[tp:, col0:col0 + width]

    outs = [[] for _ in range(14)]
    for l in range(depth):
        lam_init = 0.8 - 0.6 * math.exp(-0.3 * l)
        lam_vecs = (diff_lq1[l], diff_lk1[l], diff_lq2[l], diff_lk2[l])

        proj = mm(xn, w_main, l)
        dt_raw = mm(xn, w_dt, l)[:, :heads]
        y_ssm = y_pool = None
        h_last = []
        for (row0, nb, sl, conv_hist, h0, p_hist, pos_base, q_chunk, nsub) in (
                (0, bp, lp, zeros_p(k_ssm - 1, xbc_w), zeros_p(heads, p_dim, n_state),
                 zeros_p(pool_hist, pool_width), 0, q_ssd, nsub_ssd),
                (tp, bs, ls, state_ssm_conv[l], state_ssm[l], state_pool[l], past, ls, 1)):
            xbc = conv_silu(proj, c_xbc, xbc_w, row0, nb, sl, conv_hist, ssm_conv_w[l], ssm_conv_b[l])
            dt_g = dt_raw[row0:row0 + nb * sl].reshape(nb, sl, heads)
            y_ssm, hl = ssd_mix(xbc, proj, c_z, row0, nb, sl, dt_g, ssm_dt_bias[l], ssm_a_log[l], ssm_d[l],
                                ssm_norm[l], h0, y_ssm, groups=groups, n_state=n_state, q=q_chunk, nsub=nsub)
            h_last.append(hl)
            y_pool = pool_mix(proj, c_u, row0, nb, sl, p_hist, pos_base, pool_wb[l], pool_scale[l], y_pool)
        y_diff = diff_attn_prompt(proj, c_q, c_k, c_v, bp, lp, dheads, dh, lam_vecs, diff_subln[l], lam_init)
        y_diff = diff_attn_sample(proj, c_q, c_k, c_v, tp, bs, ls, dheads, dh, k_cache, v_cache, l,
                                  lam_vecs, diff_subln[l], lam_init, y_diff)
        merged = branch_merge(y_ssm, y_pool, y_diff, proj, c_g, w_bs, w_bp, w_bd, l)
        h, xn = mm_post(merged, w_out, l, h, norm_mix_post[l], norm_xa_pre[l], tn_pref=512, tk_pref=dm)

        mem_kv = mm(rmsnorm_bf16(mem_rows, norm_mem[l]), w_kv, l)
        q_xa = mm(xn, w_q, l)
        kv_p = [pl.BlockSpec((n_mem, xa_w), lambda b, t: (b, 0)), pl.BlockSpec((n_mem, xa_w), lambda b, t: (b, 1))]
        kv_s = [pl.BlockSpec((None, None, n_mem, xa_w), lambda b, t, l=l: (l, b, 0, 0))] * 2
        o_xa = cross_attn(q_xa, 0, bp, lp, mem_kv, mem_kv, kv_p, xheads, xhd, None)
        o_xa = cross_attn(q_xa, tp, bs, ls, mem_k, mem_v, kv_s, xheads, xhd, o_xa)
        h, xn = mm_post(o_xa, w_xo, l, h, norm_xa_post[l], norm_ffn_pre[l], tn_pref=dm, tk_pref=xa_w)

        hid, tail_g, tail_v = ffn_up_geglu(xn, bp, lp, w_up, l, ffp, zeros_p(k_ffn - 1, 2 * ffp),
                                           conv_w_ff[l], conv_b_ff[l])
        up_s = mm(xn, w_up, l, row0=tp, tm_pref=512)
        hid = conv_geglu(up_s, ffp, bs, ls, ffn_hist[l], conv_w_ff[l], conv_b_ff[l], hid, tp)
        g_next = norm_mix_pre[l + 1] if l + 1 < depth else None
        res = mm_post(hid, w_down, l, h, norm_ffn_post[l], g_next, tn_pref=1024, tk_pref=2816)
        h, xn = res if g_next is not None else (res, None)

        conv_p, conv_s = tails(proj, c_xbc, xbc_w, k_ssm - 1)
        pool_p, pool_s = tails(proj, c_u, pool_width, pool_hist)
        kk_p, kk_s = both(proj, c_k, diff_w)
        vv_p, vv_s = both(proj, c_v, diff_w)
        ups = up_s.reshape(bs, ls, 2 * ffp)[:, ls - (k_ffn - 1):]
        ffn_p = jnp.concatenate([tail_g[:, -(k_ffn - 1):, :d_ff], tail_v[:, -(k_ffn - 1):, :d_ff]], axis=-1)
        ffn_s = jnp.concatenate([ups[..., :d_ff], ups[..., ffp:ffp + d_ff]], axis=-1)
        layer_out = (h_last[0], h_last[1], conv_p, conv_s, pool_p, pool_s,
                     kk_p.reshape(bp, lp, dheads, 2, dh), kk_s.reshape(bs, ls, dheads, 2, dh),
                     vv_p.reshape(bp, lp, dheads, 2 * dh), vv_s.reshape(bs, ls, dheads, 2 * dh),
                     mem_kv[:, :xa_w].reshape(bp, n_mem, xheads, xhd),
                     mem_kv[:, xa_w:].reshape(bp, n_mem, xheads, xhd),
                     ffn_p, ffn_s)
        for acc, val in zip(outs, layer_out):
            acc.append(val)

    return (h[:tp].reshape(bp, lp, dm), h[tp:].reshape(bs, ls, dm)) + tuple(jnp.stack(o) for o in outs)
```

```python
import functools
import math

import jax
import jax.numpy as jnp
from jax import lax
from jax.experimental import pallas as pl
from jax.experimental.pallas import tpu as pltpu

F32 = jnp.float32
BF16 = jnp.bfloat16
EPS = 1e-6
CHUNK = 64
POOL_WINDOWS = (2, 4, 8, 16)
SUBLANES = 8
PACKED_ROWS = 16
LANES = 128
VMEM_LIMIT_BYTES = 56 * 1024 * 1024
NT_DIMS = (((1,), (1,)), ((), ()))
TN_DIMS = (((0,), (0,)), ((), ()))
SINGLE = pl.Buffered(1)
FFN_ROW_CHUNK = 64
NORM_UNROLL = 8


def _tile(n, pref, align):
    t = (min(pref, n) // align) * align
    while t >= align:
        if n % t == 0:
            return t
        t -= align
    return n


def _round_up(n, m):
    return (n + m - 1) // m * m


def _params(*sem):
    return pltpu.CompilerParams(dimension_semantics=sem, vmem_limit_bytes=VMEM_LIMIT_BYTES)


def _sigmoid(x):
    return 1.0 / (1.0 + jnp.exp(-x))


def _softplus(x):
    return jnp.maximum(x, 0.0) + jnp.log(1.0 + jnp.exp(-jnp.abs(x)))


def _gelu_tanh(x):
    return 0.5 * x * (1.0 + jnp.tanh(math.sqrt(2.0 / math.pi) * (x + 0.044715 * (x * x * x))))


def _rms(x, g):
    return x * lax.rsqrt(jnp.mean(x * x, axis=-1, keepdims=True) + EPS) * g


def _dot_sel(x, sel, sel_first=False):
    hi = x.astype(BF16)
    rest = x - hi.astype(F32)
    mid = rest.astype(BF16)
    lo = (rest - mid.astype(F32)).astype(BF16)
    sel = sel.astype(BF16)
    dots = [jnp.dot(sel, t, preferred_element_type=F32) if sel_first else
            jnp.dot(t, sel, preferred_element_type=F32) for t in (hi, mid, lo)]
    return dots[0] + dots[1] + dots[2]


def _drop_ref(body, idx):
    def wrapped(*refs):
        return body(*refs[:idx], *refs[idx + 1:])
    return wrapped


def _into(joint, in_specs, operands):
    if joint is None:
        return in_specs, operands, {}
    return (in_specs + [pl.BlockSpec(memory_space=pl.ANY)], operands + [joint], {len(operands): 0})


def _norm_kernel(x_ref, g_ref, o_ref, *, rows):
    def body(r, carry):
        sl = pl.ds(pl.multiple_of(r * rows, rows), rows)
        o_ref[sl, :] = _rms(x_ref[sl, :], g_ref[...]).astype(BF16)
        return carry
    lax.fori_loop(0, x_ref.shape[0] // rows, body, 0, unroll=NORM_UNROLL)


def rmsnorm_bf16(x, g, *, tm_pref=256):
    t, d = x.shape
    tm = _tile(t, tm_pref, PACKED_ROWS)
    return pl.pallas_call(
        functools.partial(_norm_kernel, rows=PACKED_ROWS),
        grid=(t // tm,),
        in_specs=[pl.BlockSpec((tm, d), lambda i: (i, 0)), pl.BlockSpec((1, d), lambda i: (0, 0))],
        out_specs=pl.BlockSpec((tm, d), lambda i: (i, 0)),
        out_shape=jax.ShapeDtypeStruct((t, d), BF16),
        compiler_params=_params("parallel"),
    )(x, g.reshape(1, d))


def _mm_kernel(x_ref, w_ref, o_ref):
    o_ref[...] = jnp.dot(x_ref[...], w_ref[...], preferred_element_type=F32)


def mm(x, w_all, layer, *, row0=0, nrows=None, tm_pref=768, tn_pref=1024):
    k = x.shape[1]
    n = w_all.shape[2]
    nrows = x.shape[0] - row0 if nrows is None else nrows
    tm = _tile(math.gcd(nrows, row0) if row0 else nrows, tm_pref, PACKED_ROWS)
    tn = _tile(n, tn_pref, LANES)
    rb = row0 // tm
    return pl.pallas_call(
        _mm_kernel,
        grid=(nrows // tm, n // tn),
        in_specs=[pl.BlockSpec((tm, k), lambda i, j: (rb + i, 0)),
                  pl.BlockSpec((None, k, tn), lambda i, j: (layer, 0, j))],
        out_specs=pl.BlockSpec((tm, tn), lambda i, j: (i, j)),
        out_shape=jax.ShapeDtypeStruct((nrows, n), F32),
        compiler_params=_params("parallel", "parallel"),
    )(x, w_all)


def _post_norm_kernel(y_ref, res_ref, g_ref, gn_ref, h_ref, *maybe_xn_ref, rows):
    def body(r, carry):
        sl = pl.ds(pl.multiple_of(r * rows, rows), rows)
        h_new = res_ref[sl, :] + _rms(y_ref[sl, :], g_ref[...])
        h_ref[sl, :] = h_new
        for xn_ref in maybe_xn_ref:
            xn_ref[sl, :] = _rms(h_new, gn_ref[...]).astype(BF16)
        return carry
    lax.fori_loop(0, y_ref.shape[0] // rows, body, 0, unroll=NORM_UNROLL)


def post_norm(y, res, g, g_next, *, tm_pref=256):
    t, d = y.shape
    tm = _tile(t, tm_pref, PACKED_ROWS)
    row_spec = pl.BlockSpec((tm, d), lambda i: (i, 0))
    vec_spec = pl.BlockSpec((1, d), lambda i: (0, 0))
    emit = g_next is not None
    out = pl.pallas_call(
        functools.partial(_post_norm_kernel, rows=PACKED_ROWS),
        grid=(t // tm,),
        in_specs=[row_spec, row_spec, vec_spec, vec_spec],
        out_specs=[row_spec, row_spec] if emit else [row_spec],
        out_shape=[jax.ShapeDtypeStruct((t, d), F32)] + ([jax.ShapeDtypeStruct((t, d), BF16)] if emit else []),
        compiler_params=_params("parallel"),
    )(y, res, g.reshape(1, d), (g_next if emit else g).reshape(1, d))
    return tuple(out) if emit else out[0]


def _merge_kernel(ys_ref, yp_ref, yd_ref, g0_ref, g1_ref, g2_ref, ws_ref, wp_ref, wd_ref, o_ref):
    acc = _sigmoid(g0_ref[...]) * jnp.dot(ys_ref[...], ws_ref[...], preferred_element_type=F32)
    acc += _sigmoid(g1_ref[...]) * jnp.dot(yp_ref[...], wp_ref[...], preferred_element_type=F32)
    acc += _sigmoid(g2_ref[...]) * jnp.dot(yd_ref[...], wd_ref[...], preferred_element_type=F32)
    o_ref[...] = acc.astype(BF16)


def branch_merge(y_ssm, y_pool, y_diff, proj, gate_col0, w_s, w_p, w_d, layer, *, tm_pref=512, tn_pref=1024):
    t = y_ssm.shape[0]
    d = w_s.shape[2]
    tm = _tile(t, tm_pref, PACKED_ROWS)
    tn = _tile(d, tn_pref, LANES)
    assert gate_col0 % tn == 0
    nj = d // tn
    gj = gate_col0 // tn

    def gate_spec(br):
        return pl.BlockSpec((tm, tn), lambda i, j: (i, gj + br * nj + j))

    def x_spec(kdim):
        return pl.BlockSpec((tm, kdim), lambda i, j: (i, 0))

    def w_spec(kdim):
        return pl.BlockSpec((None, kdim, tn), lambda i, j: (layer, 0, j))

    return pl.pallas_call(
        _merge_kernel,
        grid=(t // tm, nj),
        in_specs=[x_spec(y_ssm.shape[1]), x_spec(y_pool.shape[1]), x_spec(y_diff.shape[1]),
                  gate_spec(0), gate_spec(1), gate_spec(2),
                  w_spec(w_s.shape[1]), w_spec(w_p.shape[1]), w_spec(w_d.shape[1])],
        out_specs=pl.BlockSpec((tm, tn), lambda i, j: (i, j)),
        out_shape=jax.ShapeDtypeStruct((t, d), BF16),
        compiler_params=_params("parallel", "parallel"),
    )(y_ssm, y_pool, y_diff, proj, proj, proj, w_s, w_p, w_d)


def _conv_rows(ext_ref, u, halo, w_ref, b_ref, k):
    n = u.shape[0]
    ext_ref[0:SUBLANES, :] = halo
    ext_ref[SUBLANES:SUBLANES + n, :] = u
    acc = u * w_ref[k - 1:k, :]
    for i in range(k - 1):
        acc = acc + ext_ref[pl.ds(SUBLANES - (k - 1) + i, n), :] * w_ref[i:i + 1, :]
    return acc + b_ref[...]


def _halo(prev_ref, hist_ref, s, nseq):
    if nseq > 1:
        return hist_ref[s]
    return jnp.where(pl.program_id(1) == 0, hist_ref[0], prev_ref[...])


def _conv_silu_kernel(u_ref, prev_ref, hist_ref, w_ref, b_ref, o_ref, ext_ref, *, k, sl, nseq):
    for s in range(nseq):
        rows = slice(s * sl, (s + 1) * sl)
        y = _conv_rows(ext_ref, u_ref[rows, :], _halo(prev_ref, hist_ref, s, nseq), w_ref, b_ref, k)
        o_ref[rows, :] = y * _sigmoid(y)


def _seq_tiling(nb, seq_len, tl_pref, align):
    if seq_len >= tl_pref:
        return _tile(seq_len, tl_pref, align), 1
    nseq = _tile(nb, max(tl_pref // seq_len, 1), 1)
    return nseq * seq_len, nseq


def _seq_specs(row0, rows_per_b, tl, tc, col_blk0):
    blk0 = row0 // tl
    per_b = rows_per_b // tl
    sub0 = row0 // SUBLANES
    sub_per_b = rows_per_b // SUBLANES
    sub_per_tile = tl // SUBLANES
    cur = pl.BlockSpec((tl, tc), lambda b, t, c: (blk0 + b * per_b + t, col_blk0 + c))
    prev = pl.BlockSpec(
        (SUBLANES, tc),
        lambda b, t, c: (jnp.maximum(sub0 + b * sub_per_b + t * sub_per_tile - 1, 0), col_blk0 + c))
    return cur, prev


def _pad_hist(hist, rows):
    return jnp.pad(hist, ((0, 0), (rows - hist.shape[1], 0), (0, 0)))


def conv_silu(proj, col0, width, row0, nb, seq_len, hist, w, bias, *, tl_pref=512, tc_pref=512):
    k = w.shape[0]
    tl, nseq = _seq_tiling(nb, seq_len, tl_pref, SUBLANES)
    tc = _tile(width, tc_pref, LANES)
    rows_per_b = max(tl, seq_len)
    assert row0 % tl == 0 and col0 % tc == 0
    cur, prev = _seq_specs(row0, rows_per_b, tl, tc, col0 // tc)
    per_b = rows_per_b // tl
    return pl.pallas_call(
        functools.partial(_conv_silu_kernel, k=k, sl=tl // nseq, nseq=nseq),
        grid=(nb // nseq, per_b, width // tc),
        in_specs=[cur, prev,
                  pl.BlockSpec((nseq, SUBLANES, tc), lambda b, t, c: (b, 0, c)),
                  pl.BlockSpec((k, tc), lambda b, t, c: (0, c)),
                  pl.BlockSpec((1, tc), lambda b, t, c: (0, c))],
        out_specs=pl.BlockSpec((tl, tc), lambda b, t, c: (b * per_b + t, c)),
        out_shape=jax.ShapeDtypeStruct((nb * seq_len, width), F32),
        scratch_shapes=[pltpu.VMEM((SUBLANES + tl // nseq, tc), F32)],
        compiler_params=_params("parallel", "parallel", "parallel"),
    )(proj, proj, _pad_hist(hist, SUBLANES), w, bias.reshape(1, width))


def _conv_geglu_kernel(ug_ref, pg_ref, hg_ref, wg_ref, bg_ref, uv_ref, pv_ref, hv_ref, wv_ref, bv_ref,
                       o_ref, extg_ref, extv_ref, *, k, sl, nseq):
    for s in range(nseq):
        rows = slice(s * sl, (s + 1) * sl)
        gate = _conv_rows(extg_ref, ug_ref[rows, :], _halo(pg_ref, hg_ref, s, nseq), wg_ref, bg_ref, k)
        val = _conv_rows(extv_ref, uv_ref[rows, :], _halo(pv_ref, hv_ref, s, nseq), wv_ref, bv_ref, k)
        o_ref[rows, :] = (_gelu_tanh(gate) * val).astype(BF16)


def conv_geglu(up, half, nb, seq_len, hist, w, bias, joint, out_row0, *, tl_pref=512, tc_pref=1024):
    k = w.shape[0]
    tl, nseq = _seq_tiling(nb, seq_len, tl_pref, PACKED_ROWS)
    tc = _tile(half, tc_pref, LANES)
    rows_per_b = max(tl, seq_len)
    per_b = rows_per_b // tl
    ncb = half // tc
    assert out_row0 % tl == 0
    ob = out_row0 // tl
    cur_g, prev_g = _seq_specs(0, rows_per_b, tl, tc, 0)
    cur_v, prev_v = _seq_specs(0, rows_per_b, tl, tc, ncb)
    hist8 = _pad_hist(hist, SUBLANES)
    bias2 = bias.reshape(1, 2 * half)

    def side(off):
        return [pl.BlockSpec((nseq, SUBLANES, tc), lambda b, t, c: (b, 0, off + c)),
                pl.BlockSpec((k, tc), lambda b, t, c: (0, off + c)),
                pl.BlockSpec((1, tc), lambda b, t, c: (0, off + c))]

    in_specs, operands, aliases = _into(
        joint, [cur_g, prev_g] + side(0) + [cur_v, prev_v] + side(ncb),
        [up, up, hist8, w, bias2, up, up, hist8, w, bias2])
    body = functools.partial(_conv_geglu_kernel, k=k, sl=tl // nseq, nseq=nseq)
    return pl.pallas_call(
        _drop_ref(body, 10) if aliases else body,
        grid=(nb // nseq, per_b, ncb),
        in_specs=in_specs,
        out_specs=pl.BlockSpec((tl, tc), lambda b, t, c: (ob + b * per_b + t, c)),
        out_shape=jax.ShapeDtypeStruct(joint.shape, BF16),
        input_output_aliases=aliases,
        scratch_shapes=[pltpu.VMEM((SUBLANES + tl // nseq, tc), F32)] * 2,
        compiler_params=_params("parallel", "parallel", "parallel"),
    )(*operands)


def _ffn_up_kernel(xc_ref, xp_ref, wg_ref, wv_ref, hg_ref, hv_ref, cwg_ref, cbg_ref, cwv_ref, cbv_ref,
                   hid_ref, tg_ref, tv_ref, xe_ref, eg_ref, ev_ref, gg_ref, *, k, tm, tiles_per_seq):
    halo = PACKED_ROWS

    @pl.when(pl.program_id(1) == 0)
    def _():
        xe_ref[0:halo, :] = xp_ref[...]
        xe_ref[halo:halo + tm, :] = xc_ref[...]

    first = pl.program_id(0) % tiles_per_seq == 0

    def project(w_ref, h_ref, e_ref, t_ref):
        e_ref[...] = jnp.dot(xe_ref[...], w_ref[...], preferred_element_type=F32)
        e_ref[0:halo, :] = jnp.where(first, h_ref[...], e_ref[0:halo, :])
        t_ref[...] = e_ref[tm:tm + halo, :]

    def conv(e_ref, cw_ref, cb_ref, r0, n):
        acc = e_ref[halo + r0:halo + r0 + n, :] * cw_ref[k - 1:k, :]
        for i in range(k - 1):
            acc = acc + e_ref[pl.ds(halo + r0 - (k - 1) + i, n), :] * cw_ref[i:i + 1, :]
        return acc + cb_ref[...]

    chunk = _tile(tm, FFN_ROW_CHUNK, PACKED_ROWS)
    project(wg_ref, hg_ref, eg_ref, tg_ref)
    for r0 in range(0, tm, chunk):
        gg_ref[r0:r0 + chunk, :] = _gelu_tanh(conv(eg_ref, cwg_ref, cbg_ref, r0, chunk))
    project(wv_ref, hv_ref, ev_ref, tv_ref)
    for r0 in range(0, tm, chunk):
        val = conv(ev_ref, cwv_ref, cbv_ref, r0, chunk)
        hid_ref[r0:r0 + chunk, :] = (gg_ref[r0:r0 + chunk, :] * val).astype(BF16)


def ffn_up_geglu(xn, nb, seq_len, w_up_all, layer, half, hist, conv_w, conv_b, *, tm_pref=1024, tf_pref=512):
    t, d = xn.shape
    k = conv_w.shape[0]
    halo = PACKED_ROWS
    tm = _tile(seq_len, tm_pref, halo)
    tf = _tile(half, tf_pref, LANES)
    tiles_per_seq = seq_len // tm
    ncb = half // tf
    hist16 = _pad_hist(hist, halo)
    bias2 = conv_b.reshape(1, 2 * half)

    def w_spec(off):
        return pl.BlockSpec((None, d, tf), lambda i, j: (layer, 0, off + j))

    def hist_spec(off):
        return pl.BlockSpec((None, halo, tf), lambda i, j: (i // tiles_per_seq, 0, off + j))

    def cw_spec(off):
        return pl.BlockSpec((k, tf), lambda i, j: (0, off + j))

    def cb_spec(off):
        return pl.BlockSpec((1, tf), lambda i, j: (0, off + j))

    tail_spec = pl.BlockSpec((None, halo, tf), lambda i, j: (i, 0, j))
    tail_shape = jax.ShapeDtypeStruct((nb * tiles_per_seq, halo, half), F32)
    hid, tail_g, tail_v = pl.pallas_call(
        functools.partial(_ffn_up_kernel, k=k, tm=tm, tiles_per_seq=tiles_per_seq),
        grid=(nb * tiles_per_seq, ncb),
        in_specs=[pl.BlockSpec((tm, d), lambda i, j: (i, 0), pipeline_mode=SINGLE),
                  pl.BlockSpec((halo, d), lambda i, j: (jnp.maximum(i * (tm // halo) - 1, 0), 0)),
                  w_spec(0), w_spec(ncb), hist_spec(0), hist_spec(ncb),
                  cw_spec(0), cb_spec(0), cw_spec(ncb), cb_spec(ncb)],
        out_specs=[pl.BlockSpec((tm, tf), lambda i, j: (i, j)), tail_spec, tail_spec],
        out_shape=[jax.ShapeDtypeStruct((t, half), BF16), tail_shape, tail_shape],
        scratch_shapes=[pltpu.VMEM((halo + tm, d), BF16), pltpu.VMEM((halo + tm, tf), F32),
                        pltpu.VMEM((halo + tm, tf), F32), pltpu.VMEM((tm, tf), F32)],
        compiler_params=_params("parallel", "arbitrary"),
    )(xn, xn, w_up_all, w_up_all, hist16, hist16, conv_w, bias2, conv_w, bias2)
    last = slice(tiles_per_seq - 1, None, tiles_per_seq)
    return hid, tail_g[last], tail_v[last]


def _pool_kernel(u_ref, prev_ref, hist_ref, pw_ref, ps_ref, o_ref, ext_ref, *, tl, halo, pos_base, gd):
    t = pl.program_id(1)
    ext_ref[0:halo, :] = jnp.where(t == 0, hist_ref[...], prev_ref[...])
    ext_ref[halo:halo + tl, :] = u_ref[...]
    pos = pos_base + t * tl + lax.broadcasted_iota(jnp.int32, (tl, 1), 0)
    for gi, win in enumerate(POOL_WINDOWS):
        cols = slice(gi * gd, (gi + 1) * gd)
        cur = u_ref[:, cols]
        wsum = cur
        for i in range(1, win):
            wsum = wsum + ext_ref[pl.ds(halo - i, tl), cols]
        cnt = jnp.minimum(pos + 1, win).astype(F32)
        pooled = wsum / cnt - cur
        y = jnp.dot(pooled.astype(BF16), pw_ref[gi], preferred_element_type=F32) * ps_ref[:, cols]
        o_ref[:, cols] = y.astype(BF16)


def pool_mix(proj, col0, row0, nb, seq_len, hist, pos_base, pool_w, pool_scale, joint, *, tl_pref=512):
    ng, gd, _ = pool_w.shape
    width = ng * gd
    halo = 2 * SUBLANES
    assert ng == len(POOL_WINDOWS) and max(POOL_WINDOWS) <= halo
    tl = _tile(seq_len, tl_pref, halo)
    assert row0 % tl == 0 and col0 % width == 0
    blk0, per_seq = row0 // tl, seq_len // tl
    h0, h_per_seq, h_per_tile = row0 // halo, seq_len // halo, tl // halo
    cb = col0 // width
    in_specs, operands, aliases = _into(
        joint,
        [pl.BlockSpec((tl, width), lambda b, t: (blk0 + b * per_seq + t, cb)),
         pl.BlockSpec((halo, width),
                      lambda b, t: (jnp.maximum(h0 + b * h_per_seq + t * h_per_tile - 1, 0), cb)),
         pl.BlockSpec((None, halo, width), lambda b, t: (b, 0, 0)),
         pl.BlockSpec((ng, gd, gd), lambda b, t: (0, 0, 0)),
         pl.BlockSpec((1, width), lambda b, t: (0, 0))],
        [proj, proj, _pad_hist(hist, halo), pool_w, pool_scale.reshape(1, width)])
    body = functools.partial(_pool_kernel, tl=tl, halo=halo, pos_base=pos_base, gd=gd)
    return pl.pallas_call(
        _drop_ref(body, 5) if aliases else body,
        grid=(nb, per_seq),
        in_specs=in_specs,
        out_specs=pl.BlockSpec((tl, width), lambda b, t: (blk0 + b * per_seq + t, 0)),
        out_shape=jax.ShapeDtypeStruct((proj.shape[0], width), BF16),
        input_output_aliases=aliases,
        scratch_shapes=[pltpu.VMEM((halo + tl, width), F32)],
        compiler_params=_params("parallel", "parallel"),
    )(*operands)


def _ssd_kernel(xs_ref, b_ref, c_ref, z_ref, dtc_ref, dtr_ref, biasc_ref, biasr_ref, alogc_ref, alogr_ref,
                dskip_ref, nw_ref, h0_ref, y_ref, hl_ref, ht_ref, *, q, nsub, hpg, p):
    ti = pl.program_id(2)
    gw = hpg * p
    n = b_ref.shape[1]

    @pl.when(ti == 0)
    def _():
        ht_ref[...] = h0_ref[...].reshape(gw, n).T

    dt_c_all = _softplus(dtc_ref[...] + biasc_ref[...])
    dt_r_all = _softplus(dtr_ref[...] + biasr_ref[...])
    a_c = -jnp.exp(alogc_ref[...])
    a_r = -jnp.exp(alogr_ref[...])
    row = lax.broadcasted_iota(jnp.int32, (q, q), 0)
    col = lax.broadcasted_iota(jnp.int32, (q, q), 1)
    lower = row >= col
    upper = row <= col

    def spread(width, rep):
        head = lax.broadcasted_iota(jnp.int32, (hpg, width), 0)
        lane = lax.broadcasted_iota(jnp.int32, (hpg, width), 1)
        return lane // rep == head

    to_lanes = spread(gw, p)
    to_keys = spread(hpg * q, q)
    lane_head = lax.broadcasted_iota(jnp.int32, (1, gw), 1) // p
    d_skip = _dot_sel(dskip_ref[...], to_lanes)
    ht = ht_ref[...]
    for s in range(nsub):
        rows = slice(s * q, (s + 1) * q)
        dt_c = dt_c_all[rows, :]
        dt_r = dt_r_all[:, rows]
        acum_c = _dot_sel(dt_c * a_c, lower, sel_first=True)
        acum_r = _dot_sel(dt_r * a_r, upper)
        a_last = acum_c[q - 1:q, :]
        per_head = jnp.concatenate([jnp.exp(acum_c), jnp.exp(a_last - acum_c) * dt_c], axis=0)
        per_lane = _dot_sel(per_head, to_lanes)
        exp_a, w_end = per_lane[:q], per_lane[q:]
        chunk_decay = _dot_sel(jnp.exp(a_last), to_lanes)
        ac_keys = _dot_sel(acum_c, to_keys)
        x = xs_ref[rows, :]
        bm = b_ref[rows, :].astype(BF16)
        cm = c_ref[rows, :].astype(BF16)
        cb = lax.dot_general(cm, bm, NT_DIMS, preferred_element_type=F32)
        wmats = []
        for r in range(hpg):
            seg = ac_keys[:, r * q:(r + 1) * q] - acum_r[r:r + 1, :]
            decay = jnp.exp(jnp.where(lower, seg, -jnp.inf))
            wmats.append((cb * decay * dt_r[r:r + 1, :]).astype(BF16))
        y_all = jnp.dot(jnp.concatenate(wmats, axis=0), x.astype(BF16), preferred_element_type=F32)
        y = y_all[:q]
        for r in range(1, hpg):
            y = jnp.where(lane_head == r, y_all[r * q:(r + 1) * q], y)
        y = y + jnp.dot(cm, ht.astype(BF16), preferred_element_type=F32) * exp_a
        s_new = lax.dot_general(bm, (x * w_end).astype(BF16), TN_DIMS, preferred_element_type=F32)
        ht = ht * chunk_decay + s_new
        zt = z_ref[rows, :]
        yg = (y + x * d_skip) * (zt * _sigmoid(zt))
        y_ref[rows, :] = _rms(yg, nw_ref[...]).astype(BF16)
    ht_ref[...] = ht

    @pl.when(ti == pl.num_programs(2) - 1)
    def _():
        hl_ref[...] = ht.T.reshape(hpg, p, n)


def ssd_mix(xbc, proj, z_col0, row0, nb, seq_len, dt_raw, dt_bias, a_log, d_skip, norm_w, h0, joint, *,
            groups, n_state, q, nsub):
    heads, p = h0.shape[1], h0.shape[2]
    hpg = heads // groups
    gw = hpg * p
    d_inner = heads * p
    tl = q * nsub
    nt = seq_len // tl
    assert seq_len % tl == 0 and row0 % tl == 0 and z_col0 % gw == 0 and d_inner % n_state == 0
    dt_g = dt_raw.reshape(nb, seq_len, groups, hpg)
    dt_col = dt_g.transpose(0, 2, 1, 3)
    dt_row = dt_g.transpose(0, 2, 3, 1)
    blk0 = row0 // tl
    zc = z_col0 // gw
    bcol = d_inner // n_state

    def per_group(arr, shape):
        return arr.reshape((groups,) + shape), pl.BlockSpec((None,) + shape, lambda b, g, t: (g, 0, 0))

    biasc, biasc_spec = per_group(dt_bias, (1, hpg))
    biasr, biasr_spec = per_group(dt_bias, (hpg, 1))
    alogc, alogc_spec = per_group(a_log, (1, hpg))
    alogr, alogr_spec = per_group(a_log, (hpg, 1))
    dsk, dsk_spec = per_group(d_skip, (1, hpg))
    h_spec = pl.BlockSpec((None, hpg, p, n_state), lambda b, g, t: (b, g, 0, 0))
    in_specs, operands, aliases = _into(
        joint,
        [pl.BlockSpec((tl, gw), lambda b, g, t: (b * nt + t, g)),
         pl.BlockSpec((tl, n_state), lambda b, g, t: (b * nt + t, bcol + g)),
         pl.BlockSpec((tl, n_state), lambda b, g, t: (b * nt + t, bcol + groups + g)),
         pl.BlockSpec((tl, gw), lambda b, g, t: (blk0 + b * nt + t, zc + g)),
         pl.BlockSpec((None, None, tl, hpg), lambda b, g, t: (b, g, t, 0)),
         pl.BlockSpec((None, None, hpg, tl), lambda b, g, t: (b, g, 0, t)),
         biasc_spec, biasr_spec, alogc_spec, alogr_spec, dsk_spec,
         pl.BlockSpec((1, gw), lambda b, g, t: (0, g)),
         h_spec],
        [xbc, xbc, xbc, proj, dt_col, dt_row, biasc, biasr, alogc, alogr, dsk,
         norm_w.reshape(1, d_inner), h0])
    body = functools.partial(_ssd_kernel, q=q, nsub=nsub, hpg=hpg, p=p)
    return pl.pallas_call(
        _drop_ref(body, 13) if aliases else body,
        grid=(nb, groups, nt),
        in_specs=in_specs,
        out_specs=[pl.BlockSpec((tl, gw), lambda b, g, t: (blk0 + b * nt + t, g)), h_spec],
        out_shape=[jax.ShapeDtypeStruct((proj.shape[0], d_inner), BF16),
                   jax.ShapeDtypeStruct(h0.shape, F32)],
        input_output_aliases=aliases,
        scratch_shapes=[pltpu.VMEM((n_state, gw), F32)],
        compiler_params=_params("parallel", "parallel", "arbitrary"),
    )(*operands)


def _lambda(lq1_ref, lk1_ref, lq2_ref, lk2_ref, lam_init):
    s1 = jnp.sum(lq1_ref[...] * lk1_ref[...], axis=-1, keepdims=True)
    s2 = jnp.sum(lq2_ref[...] * lk2_ref[...], axis=-1, keepdims=True)
    return jnp.exp(s1) - jnp.exp(s2) + lam_init


def _fold_lanes(a, op):
    if a.shape[1] % LANES:
        return a
    parts = [a[:, i:i + LANES] for i in range(0, a.shape[1], LANES)]
    while len(parts) > 1:
        parts = [op(parts[i], parts[i + 1]) for i in range(0, len(parts) - 1, 2)] + parts[len(parts) & ~1:]
    return parts[0]


def _diff_prompt_kernel(q_ref, k_ref, v_ref, lq1_ref, lk1_ref, lq2_ref, lk2_ref, sub_ref, o_ref,
                        s_ref, acc_ref, *, t, tk, d, scale, lam_init):
    iq = pl.program_id(2)
    lam = _lambda(lq1_ref, lk1_ref, lq2_ref, lk2_ref, lam_init)
    n_full = (iq * t) // tk
    q_chunk = (iq * t + lax.broadcasted_iota(jnp.int32, (t, 1), 0)) // CHUNK
    k_chunk = (n_full * tk + lax.broadcasted_iota(jnp.int32, (1, tk), 1)) // CHUNK
    visible = k_chunk <= q_chunk
    width = LANES if tk % LANES == 0 else tk

    def tile(j):
        return pl.ds(pl.multiple_of(j * tk, tk), tk)

    outs = []
    for m in range(2):
        cols = slice(m * d, (m + 1) * d)
        qm = (q_ref[:, cols] * scale).astype(BF16)

        def scores(j, qm=qm, cols=cols):
            kt = k_ref[tile(j), cols].astype(BF16)
            return lax.dot_general(qm, kt, NT_DIMS, preferred_element_type=F32)

        def pass1(j, mx, scores=scores):
            s = scores(j)
            s_ref[:, tile(j)] = s
            return jnp.maximum(mx, _fold_lanes(s, jnp.maximum))

        mx = lax.fori_loop(0, n_full, pass1, jnp.full((t, width), -jnp.inf, F32))
        s_last = jnp.where(visible, scores(n_full), -jnp.inf)
        s_ref[:, tile(n_full)] = s_last
        mx = jnp.max(jnp.maximum(mx, _fold_lanes(s_last, jnp.maximum)), axis=-1, keepdims=True)
        acc_ref[...] = jnp.zeros(acc_ref.shape, F32)

        def pass2(j, den, mx=mx):
            pr = jnp.exp(s_ref[:, tile(j)] - mx)
            vt = v_ref[tile(j), :].astype(BF16)
            acc_ref[...] += jnp.dot(pr.astype(BF16), vt, preferred_element_type=F32)
            return den + _fold_lanes(pr, jnp.add)

        den = lax.fori_loop(0, n_full + 1, pass2, jnp.zeros((t, width), F32))
        outs.append(acc_ref[...] / jnp.sum(den, axis=-1, keepdims=True))
    o = outs[0] - lam * outs[1]
    o_ref[...] = (_rms(o, sub_ref[...]) * (1.0 - lam_init)).astype(BF16)


def _lam_specs(d):
    fixed = lambda *_: (0, 0)
    return [pl.BlockSpec((1, d), fixed)] * 4 + [pl.BlockSpec((1, 2 * d), fixed)]


def diff_attn_prompt(proj, q_col0, k_col0, v_col0, nb, seq_len, heads, d, lam_vecs, subln, lam_init, *,
                     tq_pref=256):
    hw = 2 * d
    tq = _tile(seq_len, tq_pref, CHUNK)
    nq = seq_len // tq
    tk = 2 * tq if nq % 2 == 0 else tq
    assert q_col0 % hw == 0 and k_col0 % hw == 0 and v_col0 % hw == 0
    qc, kc, vc = q_col0 // hw, k_col0 // hw, v_col0 // hw
    return pl.pallas_call(
        functools.partial(_diff_prompt_kernel, t=tq, tk=tk, d=d, scale=1.0 / math.sqrt(d), lam_init=lam_init),
        grid=(nb, heads, nq),
        in_specs=[pl.BlockSpec((tq, hw), lambda b, h, i: (b * nq + i, qc + h)),
                  pl.BlockSpec((seq_len, hw), lambda b, h, i: (b, kc + h)),
                  pl.BlockSpec((seq_len, hw), lambda b, h, i: (b, vc + h))] + _lam_specs(d),
        out_specs=pl.BlockSpec((tq, hw), lambda b, h, i: (b * nq + i, h)),
        out_shape=jax.ShapeDtypeStruct((proj.shape[0], heads * hw), BF16),
        scratch_shapes=[pltpu.VMEM((tq, seq_len), F32), pltpu.VMEM((tq, hw), F32)],
        compiler_params=_params("parallel", "parallel", "parallel"),
    )(proj, proj, proj, *[v.reshape(1, d) for v in lam_vecs], subln.reshape(1, hw))


def _diff_sample_kernel(q_ref, kn_ref, vn_ref, kp_ref, vp_ref, lq1_ref, lk1_ref, lq2_ref, lk2_ref, sub_ref,
                        o_ref, *, d, scale, lam_init):
    lam = _lambda(lq1_ref, lk1_ref, lq2_ref, lk2_ref, lam_init)
    vp = vp_ref[...].astype(BF16)
    vn = vn_ref[...].astype(BF16)
    outs = []
    for m in range(2):
        cols = slice(m * d, (m + 1) * d)
        qm = (q_ref[:, cols] * scale).astype(BF16)
        s_p = lax.dot_general(qm, kp_ref[:, cols].astype(BF16), NT_DIMS, preferred_element_type=F32)
        s_n = lax.dot_general(qm, kn_ref[:, cols].astype(BF16), NT_DIMS, preferred_element_type=F32)
        mx = jnp.maximum(jnp.max(s_p, axis=-1, keepdims=True), jnp.max(s_n, axis=-1, keepdims=True))
        p_p = jnp.exp(s_p - mx)
        p_n = jnp.exp(s_n - mx)
        den = jnp.sum(p_p, axis=-1, keepdims=True) + jnp.sum(p_n, axis=-1, keepdims=True)
        acc = jnp.dot(p_p.astype(BF16), vp, preferred_element_type=F32)
        acc += jnp.dot(p_n.astype(BF16), vn, preferred_element_type=F32)
        outs.append(acc / den)
    o = outs[0] - lam * outs[1]
    o_ref[...] = (_rms(o, sub_ref[...]) * (1.0 - lam_init)).astype(BF16)


def diff_attn_sample(proj, q_col0, k_col0, v_col0, row0, nb, seq_len, heads, d, k_cache, v_cache, layer,
                     lam_vecs, subln, lam_init, joint):
    hw = 2 * d
    past = k_cache.shape[2]
    assert row0 % seq_len == 0
    rb = row0 // seq_len
    qc, kc, vc = q_col0 // hw, k_col0 // hw, v_col0 // hw
    cache_spec = pl.BlockSpec((None, None, past, hw), lambda b, h: (layer, b, 0, h))
    in_specs, operands, aliases = _into(
        joint,
        [pl.BlockSpec((seq_len, hw), lambda b, h: (rb + b, qc + h)),
         pl.BlockSpec((seq_len, hw), lambda b, h: (rb + b, kc + h)),
         pl.BlockSpec((seq_len, hw), lambda b, h: (rb + b, vc + h)),
         cache_spec, cache_spec] + _lam_specs(d),
        [proj, proj, proj, k_cache, v_cache] + [v.reshape(1, d) for v in lam_vecs] + [subln.reshape(1, hw)])
    body = functools.partial(_diff_sample_kernel, d=d, scale=1.0 / math.sqrt(d), lam_init=lam_init)
    return pl.pallas_call(
        _drop_ref(body, 10),
        grid=(nb, heads),
        in_specs=in_specs,
        out_specs=pl.BlockSpec((seq_len, hw), lambda b, h: (rb + b, h)),
        out_shape=jax.ShapeDtypeStruct(joint.shape, BF16),
        input_output_aliases=aliases,
        compiler_params=_params("parallel", "parallel"),
    )(*operands)


def _xattn_kernel(q_ref, k_ref, v_ref, o_ref, *, heads, hd, scale):
    for h in range(heads):
        cols = slice(h * hd, (h + 1) * hd)
        qh = (q_ref[:, cols] * scale).astype(BF16)
        s = lax.dot_general(qh, k_ref[:, cols].astype(BF16), NT_DIMS, preferred_element_type=F32)
        pr = jnp.exp(s - jnp.max(s, axis=-1, keepdims=True))
        den = jnp.sum(pr, axis=-1, keepdims=True)
        o = jnp.dot(pr.astype(BF16), v_ref[:, cols].astype(BF16), preferred_element_type=F32)
        o_ref[:, cols] = (o / den).astype(BF16)


def cross_attn(q_all, row0, nb, seq_len, k_arr, v_arr, kv_specs, heads, hd, joint, *, tq_pref=512):
    width = heads * hd
    tq = _tile(seq_len, tq_pref, PACKED_ROWS)
    assert row0 % tq == 0
    blk0, per_seq = row0 // tq, seq_len // tq
    in_specs, operands, aliases = _into(
        joint, [pl.BlockSpec((tq, width), lambda b, t: (blk0 + b * per_seq + t, 0))] + kv_specs,
        [q_all, k_arr, v_arr])
    body = functools.partial(_xattn_kernel, heads=heads, hd=hd, scale=1.0 / math.sqrt(hd))
    return pl.pallas_call(
        _drop_ref(body, 3) if aliases else body,
        grid=(nb, per_seq),
        in_specs=in_specs,
        out_specs=pl.BlockSpec((tq, width), lambda b, t: (blk0 + b * per_seq + t, 0)),
        out_shape=jax.ShapeDtypeStruct((q_all.shape[0], width), BF16),
        input_output_aliases=aliases,
        compiler_params=_params("parallel", "parallel"),
    )(*operands)


def kernel(x_prompt, x_sample, state_ssm, state_ssm_conv, state_pool, cache_diff_k, cache_diff_v, cache_mem_k, cache_mem_v, state_ffn_conv, mem_prompt, norm_mix_pre, norm_mix_post, w_in, ssm_conv_w, ssm_conv_b, ssm_dt_bias, ssm_a_log, ssm_d, ssm_norm, w_br_ssm, pool_w, pool_scale, w_br_pool, diff_lq1, diff_lk1, diff_lq2, diff_lk2, diff_subln, w_br_diff, w_o, norm_xa_pre, norm_xa_post, norm_mem, xa_wq, xa_wk, xa_wv, xa_wo, norm_ffn_pre, norm_ffn_post, ffn_w_up, ffn_conv_w, ffn_conv_b, ffn_w_down):
    bp, lp, dm = x_prompt.shape
    bs, ls, _ = x_sample.shape
    depth = w_in.shape[0]
    tp, ts = bp * lp, bs * ls
    heads, p_dim, n_state = state_ssm.shape[2:]
    d_inner = heads * p_dim
    xbc_w = state_ssm_conv.shape[3]
    groups = (xbc_w - d_inner) // (2 * n_state)
    k_ssm = ssm_conv_w.shape[1]
    pool_hist, pool_width = state_pool.shape[2:]
    past, dheads, _, dh = cache_diff_k.shape[2:]
    diff_w = dheads * 2 * dh
    n_mem, xheads, xhd = cache_mem_k.shape[2:]
    xa_w = xheads * xhd
    d_ff = ffn_w_down.shape[1]
    k_ffn = ffn_conv_w.shape[1]
    ffp = _round_up(d_ff, 1024)
    assert lp >= pool_hist and ls >= pool_hist and min(lp, ls) >= max(k_ssm, k_ffn) - 1

    o_z, o_xbc, o_dt = 0, d_inner, d_inner + xbc_w
    o_u = o_dt + heads
    o_q = o_u + pool_width
    o_g = o_q + 3 * diff_w

    def seg(off, width):
        return w_in[:, :, off:off + width]

    w_main = jnp.concatenate(
        [seg(o_g, 3 * dm), seg(o_xbc, xbc_w), seg(o_z, d_inner), seg(o_q, 3 * diff_w), seg(o_u, pool_width)],
        axis=-1).astype(BF16)
    c_g, c_xbc = 0, 3 * dm
    c_z = c_xbc + xbc_w
    c_q = c_z + d_inner
    c_k, c_v = c_q + diff_w, c_q + 2 * diff_w
    c_u = c_q + 3 * diff_w
    dt_pad = _round_up(heads, LANES)
    w_dt = jnp.pad(seg(o_dt, heads), ((0, 0), (0, 0), (0, dt_pad - heads))).astype(BF16)
    w_bs, w_bp, w_bd = w_br_ssm.astype(BF16), w_br_pool.astype(BF16), w_br_diff.astype(BF16)
    w_out = w_o.astype(BF16)
    pool_wb = pool_w.astype(BF16)
    w_q = xa_wq.astype(BF16)
    w_kv = jnp.concatenate([xa_wk, xa_wv], axis=-1).astype(BF16)
    w_xo = xa_wo.astype(BF16)

    def pad_ff(a):
        pad = [(0, 0)] * (a.ndim - 1) + [(0, ffp - d_ff)]
        return jnp.concatenate([jnp.pad(a[..., :d_ff], pad), jnp.pad(a[..., d_ff:], pad)], axis=-1)

    w_up = pad_ff(ffn_w_up.astype(BF16))
    conv_w_ff = pad_ff(ffn_conv_w)
    conv_b_ff = pad_ff(ffn_conv_b)
    ffn_hist = pad_ff(state_ffn_conv)
    w_down = jnp.pad(ffn_w_down.astype(BF16), ((0, 0), (0, ffp - d_ff), (0, 0)))

    k_cache = cache_diff_k.reshape(depth, bs, past, diff_w)
    v_cache = cache_diff_v.reshape(depth, bs, past, diff_w)
    mem_k = cache_mem_k.reshape(depth, bs, n_mem, xa_w)
    mem_v = cache_mem_v.reshape(depth, bs, n_mem, xa_w)
    mem_rows = mem_prompt.reshape(bp * n_mem, dm)

    h = jnp.concatenate([x_prompt.reshape(tp, dm), x_sample.reshape(ts, dm)], axis=0)
    xn = rmsnorm_bf16(h, norm_mix_pre[0])
    zeros_p = lambda *shape: jnp.zeros((bp,) + shape, F32)
    q_ssd = _tile(lp, 128, LANES)
    nsub_ssd = _tile(lp // q_ssd, 4, 1)

    def tails(arr, col0, width, n):
        a_p = jnp.stack([arr[(b + 1) * lp - n:(b + 1) * lp, col0:col0 + width] for b in range(bp)])
        a_s = arr[tp:, col0:col0 + width].reshape(bs, ls, width)[:, ls - n:]
        return a_p, a_s

    def both(arr, col0, width):
        return arr[:tp, col0:col0 + width], arr[tp:, col0:col0 + width]

    outs = [[] for _ in range(14)]
    for l in range(depth):
        lam_init = 0.8 - 0.6 * math.exp(-0.3 * l)
        lam_vecs = (diff_lq1[l], diff_lk1[l], diff_lq2[l], diff_lk2[l])

        proj = mm(xn, w_main, l)
        dt_raw = mm(xn, w_dt, l)[:, :heads]
        y_ssm = y_pool = None
        h_last = []
        for (row0, nb, sl, conv_hist, h0, p_hist, pos_base, q_chunk, nsub) in (
                (0, bp, lp, zeros_p(k_ssm - 1, xbc_w), zeros_p(heads, p_dim, n_state),
                 zeros_p(pool_hist, pool_width), 0, q_ssd, nsub_ssd),
                (tp, bs, ls, state_ssm_conv[l], state_ssm[l], state_pool[l], past, ls, 1)):
            xbc = conv_silu(proj, c_xbc, xbc_w, row0, nb, sl, conv_hist, ssm_conv_w[l], ssm_conv_b[l])
            dt_g = dt_raw[row0:row0 + nb * sl].reshape(nb, sl, heads)
            y_ssm, hl = ssd_mix(xbc, proj, c_z, row0, nb, sl, dt_g, ssm_dt_bias[l], ssm_a_log[l], ssm_d[l],
                                ssm_norm[l], h0, y_ssm, groups=groups, n_state=n_state, q=q_chunk, nsub=nsub)
            h_last.append(hl)
            y_pool = pool_mix(proj, c_u, row0, nb, sl, p_hist, pos_base, pool_wb[l], pool_scale[l], y_pool)
        y_diff = diff_attn_prompt(proj, c_q, c_k, c_v, bp, lp, dheads, dh, lam_vecs, diff_subln[l], lam_init)
        y_diff = diff_attn_sample(proj, c_q, c_k, c_v, tp, bs, ls, dheads, dh, k_cache, v_cache, l,
                                  lam_vecs, diff_subln[l], lam_init, y_diff)
        merged = branch_merge(y_ssm, y_pool, y_diff, proj, c_g, w_bs, w_bp, w_bd, l)
        h, xn = post_norm(mm(merged, w_out, l), h, norm_mix_post[l], norm_xa_pre[l])

        mem_kv = mm(rmsnorm_bf16(mem_rows, norm_mem[l]), w_kv, l)
        q_xa = mm(xn, w_q, l)
        kv_p = [pl.BlockSpec((n_mem, xa_w), lambda b, t: (b, 0)), pl.BlockSpec((n_mem, xa_w), lambda b, t: (b, 1))]
        kv_s = [pl.BlockSpec((None, None, n_mem, xa_w), lambda b, t, l=l: (l, b, 0, 0))] * 2
        o_xa = cross_attn(q_xa, 0, bp, lp, mem_kv, mem_kv, kv_p, xheads, xhd, None)
        o_xa = cross_attn(q_xa, tp, bs, ls, mem_k, mem_v, kv_s, xheads, xhd, o_xa)
        h, xn = post_norm(mm(o_xa, w_xo, l, tn_pref=dm), h, norm_xa_post[l], norm_ffn_pre[l])

        hid, tail_g, tail_v = ffn_up_geglu(xn, bp, lp, w_up, l, ffp, zeros_p(k_ffn - 1, 2 * ffp),
                                           conv_w_ff[l], conv_b_ff[l])
        up_s = mm(xn, w_up, l, row0=tp, tm_pref=512)
        hid = conv_geglu(up_s, ffp, bs, ls, ffn_hist[l], conv_w_ff[l], conv_b_ff[l], hid, tp)
        g_next = norm_mix_pre[l + 1] if l + 1 < depth else None
        res = post_norm(mm(hid, w_down, l, tm_pref=512, tn_pref=512), h, norm_ffn_post[l], g_next)
        h, xn = res if g_next is not None else (res, None)

        conv_p, conv_s = tails(proj, c_xbc, xbc_w, k_ssm - 1)
        pool_p, pool_s = tails(proj, c_u, pool_width, pool_hist)
        kk_p, kk_s = both(proj, c_k, diff_w)
        vv_p, vv_s = both(proj, c_v, diff_w)
        ups = up_s.reshape(bs, ls, 2 * ffp)[:, ls - (k_ffn - 1):]
        ffn_p = jnp.concatenate([tail_g[:, -(k_ffn - 1):, :d_ff], tail_v[:, -(k_ffn - 1):, :d_ff]], axis=-1)
        ffn_s = jnp.concatenate([ups[..., :d_ff], ups[..., ffp:ffp + d_ff]], axis=-1)
        layer_out = (h_last[0], h_last[1], conv_p, conv_s, pool_p, pool_s,
                     kk_p.reshape(bp, lp, dheads, 2, dh), kk_s.reshape(bs, ls, dheads, 2, dh),
                     vv_p.reshape(bp, lp, dheads, 2 * dh), vv_s.reshape(bs, ls, dheads, 2 * dh),
                     mem_kv[:, :xa_w].reshape(bp, n_mem, xheads, xhd),
                     mem_kv[:, xa_w:].reshape(bp, n_mem, xheads, xhd),
                     ffn_p, ffn_s)
        for acc, val in zip(outs, layer_out):
            acc.append(val)

    return (h[:tp].reshape(bp, lp, dm), h[tp:].reshape(bs, ls, dm)) + tuple(jnp.stack(o) for o in outs)
```

```python
import functools
import math

import jax
import jax.numpy as jnp
from jax import lax
from jax.experimental import pallas as pl
from jax.experimental.pallas import tpu as pltpu

F32 = jnp.float32
BF16 = jnp.bfloat16
EPS = 1e-6
CHUNK = 64
POOL_WINDOWS = (2, 4, 8, 16)
SUBLANES = 8
PACKED_ROWS = 16
LANES = 128
VMEM_LIMIT_BYTES = 56 * 1024 * 1024
NT_DIMS = (((1,), (1,)), ((), ()))
TN_DIMS = (((0,), (0,)), ((), ()))
SINGLE = pl.Buffered(1)
FFN_ROW_CHUNK = 64
NORM_UNROLL = 8


def _tile(n, pref, align):
    t = (min(pref, n) // align) * align
    while t >= align:
        if n % t == 0:
            return t
        t -= align
    return n


def _round_up(n, m):
    return (n + m - 1) // m * m


def _params(*sem):
    return pltpu.CompilerParams(dimension_semantics=sem, vmem_limit_bytes=VMEM_LIMIT_BYTES)


def _sigmoid(x):
    return 1.0 / (1.0 + jnp.exp(-x))


def _softplus(x):
    return jnp.maximum(x, 0.0) + jnp.log(1.0 + jnp.exp(-jnp.abs(x)))


def _gelu_tanh(x):
    return 0.5 * x * (1.0 + jnp.tanh(math.sqrt(2.0 / math.pi) * (x + 0.044715 * (x * x * x))))


def _rms(x, g):
    return x * lax.rsqrt(jnp.mean(x * x, axis=-1, keepdims=True) + EPS) * g


def _dot_sel(x, sel, sel_first=False):
    hi = x.astype(BF16)
    rest = x - hi.astype(F32)
    mid = rest.astype(BF16)
    lo = (rest - mid.astype(F32)).astype(BF16)
    sel = sel.astype(BF16)
    dots = [jnp.dot(sel, t, preferred_element_type=F32) if sel_first else
            jnp.dot(t, sel, preferred_element_type=F32) for t in (hi, mid, lo)]
    return dots[0] + dots[1] + dots[2]


def _drop_ref(body, idx):
    def wrapped(*refs):
        return body(*refs[:idx], *refs[idx + 1:])
    return wrapped


def _into(joint, in_specs, operands):
    if joint is None:
        return in_specs, operands, {}
    return (in_specs + [pl.BlockSpec(memory_space=pl.ANY)], operands + [joint], {len(operands): 0})


def _norm_kernel(x_ref, g_ref, o_ref, *, rows):
    def body(r, carry):
        sl = pl.ds(pl.multiple_of(r * rows, rows), rows)
        o_ref[sl, :] = _rms(x_ref[sl, :], g_ref[...]).astype(BF16)
        return carry
    lax.fori_loop(0, x_ref.shape[0] // rows, body, 0, unroll=NORM_UNROLL)


def rmsnorm_bf16(x, g, *, tm_pref=256):
    t, d = x.shape
    tm = _tile(t, tm_pref, PACKED_ROWS)
    return pl.pallas_call(
        functools.partial(_norm_kernel, rows=PACKED_ROWS),
        grid=(t // tm,),
        in_specs=[pl.BlockSpec((tm, d), lambda i: (i, 0)), pl.BlockSpec((1, d), lambda i: (0, 0))],
        out_specs=pl.BlockSpec((tm, d), lambda i: (i, 0)),
        out_shape=jax.ShapeDtypeStruct((t, d), BF16),
        compiler_params=_params("parallel"),
    )(x, g.reshape(1, d))


def _mm_kernel(x_ref, w_ref, o_ref):
    o_ref[...] = jnp.dot(x_ref[...], w_ref[...], preferred_element_type=F32)


def mm(x, w_all, layer, *, row0=0, nrows=None, tm_pref=768, tn_pref=1024):
    k, n = w_all.shape[1:]
    assert k == x.shape[1] or (k < x.shape[1] and k % LANES == 0)
    nrows = x.shape[0] - row0 if nrows is None else nrows
    tm = _tile(math.gcd(nrows, row0) if row0 else nrows, tm_pref, PACKED_ROWS)
    tn = _tile(n, tn_pref, LANES)
    rb = row0 // tm
    return pl.pallas_call(
        _mm_kernel,
        grid=(nrows // tm, n // tn),
        in_specs=[pl.BlockSpec((tm, k), lambda i, j: (rb + i, 0)),
                  pl.BlockSpec((None, k, tn), lambda i, j: (layer, 0, j))],
        out_specs=pl.BlockSpec((tm, tn), lambda i, j: (i, j)),
        out_shape=jax.ShapeDtypeStruct((nrows, n), F32),
        compiler_params=_params("parallel", "parallel"),
    )(x, w_all)


def _cast_pad_kernel(x_ref, o_ref, *, real_blocks, blocks):
    is_real = pl.program_id(1) % blocks < real_blocks
    o_ref[...] = jnp.where(is_real, x_ref[...], 0.0).astype(BF16)


def cast_pad_halves(w, half, half_pad, *, blk_pref=512):
    depth, k, _ = w.shape
    blk = _tile(math.gcd(half, half_pad), blk_pref, LANES)
    if blk % LANES:
        pad = ((0, 0), (0, 0), (0, half_pad - half))
        return jnp.concatenate([jnp.pad(w[..., :half], pad), jnp.pad(w[..., half:], pad)], -1).astype(BF16)
    real_blocks, blocks = half // blk, half_pad // blk

    def src(l, j):
        return (l, 0, (j // blocks) * real_blocks + jnp.minimum(j % blocks, real_blocks - 1))

    return pl.pallas_call(
        functools.partial(_cast_pad_kernel, real_blocks=real_blocks, blocks=blocks),
        grid=(depth, 2 * blocks),
        in_specs=[pl.BlockSpec((None, k, blk), src)],
        out_specs=pl.BlockSpec((None, k, blk), lambda l, j: (l, 0, j)),
        out_shape=jax.ShapeDtypeStruct((depth, k, 2 * half_pad), BF16),
        compiler_params=_params("parallel", "parallel"),
    )(w)


def _post_norm_kernel(y_ref, res_ref, g_ref, gn_ref, h_ref, *maybe_xn_ref, rows):
    def body(r, carry):
        sl = pl.ds(pl.multiple_of(r * rows, rows), rows)
        h_new = res_ref[sl, :] + _rms(y_ref[sl, :], g_ref[...])
        h_ref[sl, :] = h_new
        for xn_ref in maybe_xn_ref:
            xn_ref[sl, :] = _rms(h_new, gn_ref[...]).astype(BF16)
        return carry
    lax.fori_loop(0, y_ref.shape[0] // rows, body, 0, unroll=NORM_UNROLL)


def post_norm(y, res, g, g_next, *, tm_pref=256):
    t, d = y.shape
    tm = _tile(t, tm_pref, PACKED_ROWS)
    row_spec = pl.BlockSpec((tm, d), lambda i: (i, 0))
    vec_spec = pl.BlockSpec((1, d), lambda i: (0, 0))
    emit = g_next is not None
    out = pl.pallas_call(
        functools.partial(_post_norm_kernel, rows=PACKED_ROWS),
        grid=(t // tm,),
        in_specs=[row_spec, row_spec, vec_spec, vec_spec],
        out_specs=[row_spec, row_spec] if emit else [row_spec],
        out_shape=[jax.ShapeDtypeStruct((t, d), F32)] + ([jax.ShapeDtypeStruct((t, d), BF16)] if emit else []),
        compiler_params=_params("parallel"),
    )(y, res, g.reshape(1, d), (g_next if emit else g).reshape(1, d))
    return tuple(out) if emit else out[0]


def _merge_kernel(ys_ref, yp_ref, yd_ref, g0_ref, g1_ref, g2_ref, ws_ref, wp_ref, wd_ref, o_ref):
    acc = _sigmoid(g0_ref[...]) * jnp.dot(ys_ref[...], ws_ref[...], preferred_element_type=F32)
    acc += _sigmoid(g1_ref[...]) * jnp.dot(yp_ref[...], wp_ref[...], preferred_element_type=F32)
    acc += _sigmoid(g2_ref[...]) * jnp.dot(yd_ref[...], wd_ref[...], preferred_element_type=F32)
    o_ref[...] = acc.astype(BF16)


def branch_merge(y_ssm, y_pool, y_diff, proj, gate_col0, w_s, w_p, w_d, layer, *, tm_pref=512, tn_pref=1024):
    t = y_ssm.shape[0]
    d = w_s.shape[2]
    tm = _tile(t, tm_pref, PACKED_ROWS)
    tn = _tile(d, tn_pref, LANES)
    assert gate_col0 % tn == 0
    nj = d // tn
    gj = gate_col0 // tn

    def gate_spec(br):
        return pl.BlockSpec((tm, tn), lambda i, j: (i, gj + br * nj + j))

    def x_spec(kdim):
        return pl.BlockSpec((tm, kdim), lambda i, j: (i, 0))

    def w_spec(kdim):
        return pl.BlockSpec((None, kdim, tn), lambda i, j: (layer, 0, j))

    return pl.pallas_call(
        _merge_kernel,
        grid=(t // tm, nj),
        in_specs=[x_spec(y_ssm.shape[1]), x_spec(y_pool.shape[1]), x_spec(y_diff.shape[1]),
                  gate_spec(0), gate_spec(1), gate_spec(2),
                  w_spec(w_s.shape[1]), w_spec(w_p.shape[1]), w_spec(w_d.shape[1])],
        out_specs=pl.BlockSpec((tm, tn), lambda i, j: (i, j)),
        out_shape=jax.ShapeDtypeStruct((t, d), BF16),
        compiler_params=_params("parallel", "parallel"),
    )(y_ssm, y_pool, y_diff, proj, proj, proj, w_s, w_p, w_d)


def _conv_rows(ext_ref, u, halo, w_ref, b_ref, k):
    n = u.shape[0]
    ext_ref[0:SUBLANES, :] = halo
    ext_ref[SUBLANES:SUBLANES + n, :] = u
    acc = u * w_ref[k - 1:k, :]
    for i in range(k - 1):
        acc = acc + ext_ref[pl.ds(SUBLANES - (k - 1) + i, n), :] * w_ref[i:i + 1, :]
    return acc + b_ref[...]


def _halo(prev_ref, hist_ref, s, nseq):
    if nseq > 1:
        return hist_ref[s]
    return jnp.where(pl.program_id(1) == 0, hist_ref[0], prev_ref[...])


def _conv_silu_kernel(u_ref, prev_ref, hist_ref, w_ref, b_ref, o_ref, ext_ref, *, k, sl, nseq):
    for s in range(nseq):
        rows = slice(s * sl, (s + 1) * sl)
        y = _conv_rows(ext_ref, u_ref[rows, :], _halo(prev_ref, hist_ref, s, nseq), w_ref, b_ref, k)
        o_ref[rows, :] = y * _sigmoid(y)


def _seq_tiling(nb, seq_len, tl_pref, align):
    if seq_len >= tl_pref:
        return _tile(seq_len, tl_pref, align), 1
    nseq = _tile(nb, max(tl_pref // seq_len, 1), 1)
    return nseq * seq_len, nseq


def _seq_specs(row0, rows_per_b, tl, tc, col_blk0):
    blk0 = row0 // tl
    per_b = rows_per_b // tl
    sub0 = row0 // SUBLANES
    sub_per_b = rows_per_b // SUBLANES
    sub_per_tile = tl // SUBLANES
    cur = pl.BlockSpec((tl, tc), lambda b, t, c: (blk0 + b * per_b + t, col_blk0 + c))
    prev = pl.BlockSpec(
        (SUBLANES, tc),
        lambda b, t, c: (jnp.maximum(sub0 + b * sub_per_b + t * sub_per_tile - 1, 0), col_blk0 + c))
    return cur, prev


def _pad_hist(hist, rows):
    return jnp.pad(hist, ((0, 0), (rows - hist.shape[1], 0), (0, 0)))


def conv_silu(proj, col0, width, row0, nb, seq_len, hist, w, bias, *, tl_pref=512, tc_pref=512):
    k = w.shape[0]
    tl, nseq = _seq_tiling(nb, seq_len, tl_pref, SUBLANES)
    tc = _tile(width, tc_pref, LANES)
    rows_per_b = max(tl, seq_len)
    assert row0 % tl == 0 and col0 % tc == 0
    cur, prev = _seq_specs(row0, rows_per_b, tl, tc, col0 // tc)
    per_b = rows_per_b // tl
    return pl.pallas_call(
        functools.partial(_conv_silu_kernel, k=k, sl=tl // nseq, nseq=nseq),
        grid=(nb // nseq, per_b, width // tc),
        in_specs=[cur, prev,
                  pl.BlockSpec((nseq, SUBLANES, tc), lambda b, t, c: (b, 0, c)),
                  pl.BlockSpec((k, tc), lambda b, t, c: (0, c)),
                  pl.BlockSpec((1, tc), lambda b, t, c: (0, c))],
        out_specs=pl.BlockSpec((tl, tc), lambda b, t, c: (b * per_b + t, c)),
        out_shape=jax.ShapeDtypeStruct((nb * seq_len, width), F32),
        scratch_shapes=[pltpu.VMEM((SUBLANES + tl // nseq, tc), F32)],
        compiler_params=_params("parallel", "parallel", "parallel"),
    )(proj, proj, _pad_hist(hist, SUBLANES), w, bias.reshape(1, width))


def _conv_geglu_kernel(ug_ref, pg_ref, hg_ref, wg_ref, bg_ref, uv_ref, pv_ref, hv_ref, wv_ref, bv_ref,
                       o_ref, extg_ref, extv_ref, *, k, sl, nseq):
    for s in range(nseq):
        rows = slice(s * sl, (s + 1) * sl)
        gate = _conv_rows(extg_ref, ug_ref[rows, :], _halo(pg_ref, hg_ref, s, nseq), wg_ref, bg_ref, k)
        val = _conv_rows(extv_ref, uv_ref[rows, :], _halo(pv_ref, hv_ref, s, nseq), wv_ref, bv_ref, k)
        o_ref[rows, :] = (_gelu_tanh(gate) * val).astype(BF16)


def conv_geglu(up, half, nb, seq_len, hist, w, bias, joint, out_row0, *, tl_pref=512, tc_pref=1024):
    k = w.shape[0]
    tl, nseq = _seq_tiling(nb, seq_len, tl_pref, PACKED_ROWS)
    tc = _tile(half, tc_pref, LANES)
    rows_per_b = max(tl, seq_len)
    per_b = rows_per_b // tl
    ncb = half // tc
    assert out_row0 % tl == 0
    ob = out_row0 // tl
    cur_g, prev_g = _seq_specs(0, rows_per_b, tl, tc, 0)
    cur_v, prev_v = _seq_specs(0, rows_per_b, tl, tc, ncb)
    hist8 = _pad_hist(hist, SUBLANES)
    bias2 = bias.reshape(1, 2 * half)

    def side(off):
        return [pl.BlockSpec((nseq, SUBLANES, tc), lambda b, t, c: (b, 0, off + c)),
                pl.BlockSpec((k, tc), lambda b, t, c: (0, off + c)),
                pl.BlockSpec((1, tc), lambda b, t, c: (0, off + c))]

    in_specs, operands, aliases = _into(
        joint, [cur_g, prev_g] + side(0) + [cur_v, prev_v] + side(ncb),
        [up, up, hist8, w, bias2, up, up, hist8, w, bias2])
    body = functools.partial(_conv_geglu_kernel, k=k, sl=tl // nseq, nseq=nseq)
    return pl.pallas_call(
        _drop_ref(body, 10) if aliases else body,
        grid=(nb // nseq, per_b, ncb),
        in_specs=in_specs,
        out_specs=pl.BlockSpec((tl, tc), lambda b, t, c: (ob + b * per_b + t, c)),
        out_shape=jax.ShapeDtypeStruct(joint.shape, BF16),
        input_output_aliases=aliases,
        scratch_shapes=[pltpu.VMEM((SUBLANES + tl // nseq, tc), F32)] * 2,
        compiler_params=_params("parallel", "parallel", "parallel"),
    )(*operands)


def _ffn_up_kernel(xc_ref, xp_ref, wg_ref, wv_ref, hg_ref, hv_ref, cwg_ref, cbg_ref, cwv_ref, cbv_ref,
                   hid_ref, tg_ref, tv_ref, xe_ref, eg_ref, ev_ref, gg_ref, *, k, tm, tiles_per_seq):
    halo = PACKED_ROWS

    @pl.when(pl.program_id(1) == 0)
    def _():
        xe_ref[0:halo, :] = xp_ref[...]
        xe_ref[halo:halo + tm, :] = xc_ref[...]

    first = pl.program_id(0) % tiles_per_seq == 0

    def project(w_ref, h_ref, e_ref, t_ref):
        e_ref[...] = jnp.dot(xe_ref[...], w_ref[...], preferred_element_type=F32)
        e_ref[0:halo, :] = jnp.where(first, h_ref[...], e_ref[0:halo, :])
        t_ref[...] = e_ref[tm:tm + halo, :]

    def conv(e_ref, cw_ref, cb_ref, r0, n):
        acc = e_ref[halo + r0:halo + r0 + n, :] * cw_ref[k - 1:k, :]
        for i in range(k - 1):
            acc = acc + e_ref[pl.ds(halo + r0 - (k - 1) + i, n), :] * cw_ref[i:i + 1, :]
        return acc + cb_ref[...]

    chunk = _tile(tm, FFN_ROW_CHUNK, PACKED_ROWS)
    project(wg_ref, hg_ref, eg_ref, tg_ref)
    for r0 in range(0, tm, chunk):
        gg_ref[r0:r0 + chunk, :] = _gelu_tanh(conv(eg_ref, cwg_ref, cbg_ref, r0, chunk))
    project(wv_ref, hv_ref, ev_ref, tv_ref)
    for r0 in range(0, tm, chunk):
        val = conv(ev_ref, cwv_ref, cbv_ref, r0, chunk)
        hid_ref[r0:r0 + chunk, :] = (gg_ref[r0:r0 + chunk, :] * val).astype(BF16)


def ffn_up_geglu(xn, nb, seq_len, w_up_all, layer, half, hist, conv_w, conv_b, *, tm_pref=1024, tf_pref=512):
    t, d = xn.shape
    k = conv_w.shape[0]
    halo = PACKED_ROWS
    tm = _tile(seq_len, tm_pref, halo)
    tf = _tile(half, tf_pref, LANES)
    tiles_per_seq = seq_len // tm
    ncb = half // tf
    hist16 = _pad_hist(hist, halo)
    bias2 = conv_b.reshape(1, 2 * half)

    def w_spec(off):
        return pl.BlockSpec((None, d, tf), lambda i, j: (layer, 0, off + j))

    def hist_spec(off):
        return pl.BlockSpec((None, halo, tf), lambda i, j: (i // tiles_per_seq, 0, off + j))

    def cw_spec(off):
        return pl.BlockSpec((k, tf), lambda i, j: (0, off + j))

    def cb_spec(off):
        return pl.BlockSpec((1, tf), lambda i, j: (0, off + j))

    tail_spec = pl.BlockSpec((None, halo, tf), lambda i, j: (i, 0, j))
    tail_shape = jax.ShapeDtypeStruct((nb * tiles_per_seq, halo, half), F32)
    hid, tail_g, tail_v = pl.pallas_call(
        functools.partial(_ffn_up_kernel, k=k, tm=tm, tiles_per_seq=tiles_per_seq),
        grid=(nb * tiles_per_seq, ncb),
        in_specs=[pl.BlockSpec((tm, d), lambda i, j: (i, 0), pipeline_mode=SINGLE),
                  pl.BlockSpec((halo, d), lambda i, j: (jnp.maximum(i * (tm // halo) - 1, 0), 0)),
                  w_spec(0), w_spec(ncb), hist_spec(0), hist_spec(ncb),
                  cw_spec(0), cb_spec(0), cw_spec(ncb), cb_spec(ncb)],
        out_specs=[pl.BlockSpec((tm, tf), lambda i, j: (i, j)), tail_spec, tail_spec],
        out_shape=[jax.ShapeDtypeStruct((t, half), BF16), tail_shape, tail_shape],
        scratch_shapes=[pltpu.VMEM((halo + tm, d), BF16), pltpu.VMEM((halo + tm, tf), F32),
                        pltpu.VMEM((halo + tm, tf), F32), pltpu.VMEM((tm, tf), F32)],
        compiler_params=_params("parallel", "arbitrary"),
    )(xn, xn, w_up_all, w_up_all, hist16, hist16, conv_w, bias2, conv_w, bias2)
    last = slice(tiles_per_seq - 1, None, tiles_per_seq)
    return hid, tail_g[last], tail_v[last]


def _pool_kernel(u_ref, prev_ref, hist_ref, pw_ref, ps_ref, o_ref, ext_ref, *, tl, halo, pos_base, gd):
    t = pl.program_id(1)
    ext_ref[0:halo, :] = jnp.where(t == 0, hist_ref[...], prev_ref[...])
    ext_ref[halo:halo + tl, :] = u_ref[...]
    pos = pos_base + t * tl + lax.broadcasted_iota(jnp.int32, (tl, 1), 0)
    for gi, win in enumerate(POOL_WINDOWS):
        cols = slice(gi * gd, (gi + 1) * gd)
        cur = u_ref[:, cols]
        wsum = cur
        for i in range(1, win):
            wsum = wsum + ext_ref[pl.ds(halo - i, tl), cols]
        cnt = jnp.minimum(pos + 1, win).astype(F32)
        pooled = wsum / cnt - cur
        y = jnp.dot(pooled.astype(BF16), pw_ref[gi], preferred_element_type=F32) * ps_ref[:, cols]
        o_ref[:, cols] = y.astype(BF16)


def pool_mix(proj, col0, row0, nb, seq_len, hist, pos_base, pool_w, pool_scale, joint, *, tl_pref=512):
    ng, gd, _ = pool_w.shape
    width = ng * gd
    halo = 2 * SUBLANES
    assert ng == len(POOL_WINDOWS) and max(POOL_WINDOWS) <= halo
    tl = _tile(seq_len, tl_pref, halo)
    assert row0 % tl == 0 and col0 % width == 0
    blk0, per_seq = row0 // tl, seq_len // tl
    h0, h_per_seq, h_per_tile = row0 // halo, seq_len // halo, tl // halo
    cb = col0 // width
    in_specs, operands, aliases = _into(
        joint,
        [pl.BlockSpec((tl, width), lambda b, t: (blk0 + b * per_seq + t, cb)),
         pl.BlockSpec((halo, width),
                      lambda b, t: (jnp.maximum(h0 + b * h_per_seq + t * h_per_tile - 1, 0), cb)),
         pl.BlockSpec((None, halo, width), lambda b, t: (b, 0, 0)),
         pl.BlockSpec((ng, gd, gd), lambda b, t: (0, 0, 0)),
         pl.BlockSpec((1, width), lambda b, t: (0, 0))],
        [proj, proj, _pad_hist(hist, halo), pool_w, pool_scale.reshape(1, width)])
    body = functools.partial(_pool_kernel, tl=tl, halo=halo, pos_base=pos_base, gd=gd)
    return pl.pallas_call(
        _drop_ref(body, 5) if aliases else body,
        grid=(nb, per_seq),
        in_specs=in_specs,
        out_specs=pl.BlockSpec((tl, width), lambda b, t: (blk0 + b * per_seq + t, 0)),
        out_shape=jax.ShapeDtypeStruct((proj.shape[0], width), BF16),
        input_output_aliases=aliases,
        scratch_shapes=[pltpu.VMEM((halo + tl, width), F32)],
        compiler_params=_params("parallel", "parallel"),
    )(*operands)


def _ssd_kernel(xs_ref, b_ref, c_ref, z_ref, dtc_ref, dtr_ref, biasc_ref, biasr_ref, alogc_ref, alogr_ref,
                dskip_ref, nw_ref, h0_ref, y_ref, hl_ref, ht_ref, *, q, nsub, hpg, p):
    ti = pl.program_id(2)
    gw = hpg * p
    n = b_ref.shape[1]

    @pl.when(ti == 0)
    def _():
        ht_ref[...] = h0_ref[...].reshape(gw, n).T

    dt_c_all = _softplus(dtc_ref[...] + biasc_ref[...])
    dt_r_all = _softplus(dtr_ref[...] + biasr_ref[...])
    a_c = -jnp.exp(alogc_ref[...])
    a_r = -jnp.exp(alogr_ref[...])
    row = lax.broadcasted_iota(jnp.int32, (q, q), 0)
    col = lax.broadcasted_iota(jnp.int32, (q, q), 1)
    lower = row >= col
    upper = row <= col

    def spread(width, rep):
        head = lax.broadcasted_iota(jnp.int32, (hpg, width), 0)
        lane = lax.broadcasted_iota(jnp.int32, (hpg, width), 1)
        return lane // rep == head

    to_lanes = spread(gw, p)
    to_keys = spread(hpg * q, q)
    lane_head = lax.broadcasted_iota(jnp.int32, (1, gw), 1) // p
    d_skip = _dot_sel(dskip_ref[...], to_lanes)
    ht = ht_ref[...]
    for s in range(nsub):
        rows = slice(s * q, (s + 1) * q)
        dt_c = dt_c_all[rows, :]
        dt_r = dt_r_all[:, rows]
        acum_c = _dot_sel(dt_c * a_c, lower, sel_first=True)
        acum_r = _dot_sel(dt_r * a_r, upper)
        a_last = acum_c[q - 1:q, :]
        per_head = jnp.concatenate([jnp.exp(acum_c), jnp.exp(a_last - acum_c) * dt_c], axis=0)
        per_lane = _dot_sel(per_head, to_lanes)
        exp_a, w_end = per_lane[:q], per_lane[q:]
        chunk_decay = _dot_sel(jnp.exp(a_last), to_lanes)
        ac_keys = _dot_sel(acum_c, to_keys)
        x = xs_ref[rows, :]
        bm = b_ref[rows, :].astype(BF16)
        cm = c_ref[rows, :].astype(BF16)
        cb = lax.dot_general(cm, bm, NT_DIMS, preferred_element_type=F32)
        wmats = []
        for r in range(hpg):
            seg = ac_keys[:, r * q:(r + 1) * q] - acum_r[r:r + 1, :]
            decay = jnp.exp(jnp.where(lower, seg, -jnp.inf))
            wmats.append((cb * decay * dt_r[r:r + 1, :]).astype(BF16))
        y_all = jnp.dot(jnp.concatenate(wmats, axis=0), x.astype(BF16), preferred_element_type=F32)
        y = y_all[:q]
        for r in range(1, hpg):
            y = jnp.where(lane_head == r, y_all[r * q:(r + 1) * q], y)
        y = y + jnp.dot(cm, ht.astype(BF16), preferred_element_type=F32) * exp_a
        s_new = lax.dot_general(bm, (x * w_end).astype(BF16), TN_DIMS, preferred_element_type=F32)
        ht = ht * chunk_decay + s_new
        zt = z_ref[rows, :]
        yg = (y + x * d_skip) * (zt * _sigmoid(zt))
        y_ref[rows, :] = _rms(yg, nw_ref[...]).astype(BF16)
    ht_ref[...] = ht

    @pl.when(ti == pl.num_programs(2) - 1)
    def _():
        hl_ref[...] = ht.T.reshape(hpg, p, n)


def ssd_mix(xbc, proj, z_col0, row0, nb, seq_len, dt_raw, dt_bias, a_log, d_skip, norm_w, h0, joint, *,
            groups, n_state, q, nsub):
    heads, p = h0.shape[1], h0.shape[2]
    hpg = heads // groups
    gw = hpg * p
    d_inner = heads * p
    tl = q * nsub
    nt = seq_len // tl
    assert seq_len % tl == 0 and row0 % tl == 0 and z_col0 % gw == 0 and d_inner % n_state == 0
    dt_g = dt_raw.reshape(nb, seq_len, groups, hpg)
    dt_col = dt_g.transpose(0, 2, 1, 3)
    dt_row = dt_g.transpose(0, 2, 3, 1)
    blk0 = row0 // tl
    zc = z_col0 // gw
    bcol = d_inner // n_state

    def per_group(arr, shape):
        return arr.reshape((groups,) + shape), pl.BlockSpec((None,) + shape, lambda b, g, t: (g, 0, 0))

    biasc, biasc_spec = per_group(dt_bias, (1, hpg))
    biasr, biasr_spec = per_group(dt_bias, (hpg, 1))
    alogc, alogc_spec = per_group(a_log, (1, hpg))
    alogr, alogr_spec = per_group(a_log, (hpg, 1))
    dsk, dsk_spec = per_group(d_skip, (1, hpg))
    h_spec = pl.BlockSpec((None, hpg, p, n_state), lambda b, g, t: (b, g, 0, 0))
    in_specs, operands, aliases = _into(
        joint,
        [pl.BlockSpec((tl, gw), lambda b, g, t: (b * nt + t, g)),
         pl.BlockSpec((tl, n_state), lambda b, g, t: (b * nt + t, bcol + g)),
         pl.BlockSpec((tl, n_state), lambda b, g, t: (b * nt + t, bcol + groups + g)),
         pl.BlockSpec((tl, gw), lambda b, g, t: (blk0 + b * nt + t, zc + g)),
         pl.BlockSpec((None, None, tl, hpg), lambda b, g, t: (b, g, t, 0)),
         pl.BlockSpec((None, None, hpg, tl), lambda b, g, t: (b, g, 0, t)),
         biasc_spec, biasr_spec, alogc_spec, alogr_spec, dsk_spec,
         pl.BlockSpec((1, gw), lambda b, g, t: (0, g)),
         h_spec],
        [xbc, xbc, xbc, proj, dt_col, dt_row, biasc, biasr, alogc, alogr, dsk,
         norm_w.reshape(1, d_inner), h0])
    body = functools.partial(_ssd_kernel, q=q, nsub=nsub, hpg=hpg, p=p)
    return pl.pallas_call(
        _drop_ref(body, 13) if aliases else body,
        grid=(nb, groups, nt),
        in_specs=in_specs,
        out_specs=[pl.BlockSpec((tl, gw), lambda b, g, t: (blk0 + b * nt + t, g)), h_spec],
        out_shape=[jax.ShapeDtypeStruct((proj.shape[0], d_inner), BF16),
                   jax.ShapeDtypeStruct(h0.shape, F32)],
        input_output_aliases=aliases,
        scratch_shapes=[pltpu.VMEM((n_state, gw), F32)],
        compiler_params=_params("parallel", "parallel", "arbitrary"),
    )(*operands)


def _lambda(lq1_ref, lk1_ref, lq2_ref, lk2_ref, lam_init):
    s1 = jnp.sum(lq1_ref[...] * lk1_ref[...], axis=-1, keepdims=True)
    s2 = jnp.sum(lq2_ref[...] * lk2_ref[...], axis=-1, keepdims=True)
    return jnp.exp(s1) - jnp.exp(s2) + lam_init


def _loop_by_two(n, body, init):
    pairs = n // 2
    carry = lax.fori_loop(0, pairs, lambda i, c: body(2 * i + 1, body(2 * i, c)), init)
    return lax.fori_loop(2 * pairs, n, body, carry)


def _fold_lanes(a, op):
    if a.shape[1] % LANES:
        return a
    parts = [a[:, i:i + LANES] for i in range(0, a.shape[1], LANES)]
    while len(parts) > 1:
        parts = [op(parts[i], parts[i + 1]) for i in range(0, len(parts) - 1, 2)] + parts[len(parts) & ~1:]
    return parts[0]


def _diff_prompt_kernel(q_ref, k_ref, v_ref, lq1_ref, lk1_ref, lq2_ref, lk2_ref, sub_ref, o_ref,
                        s_ref, acc_ref, *, t, tk, d, scale, lam_init):
    iq = pl.program_id(2)
    lam = _lambda(lq1_ref, lk1_ref, lq2_ref, lk2_ref, lam_init)
    n_full = (iq * t) // tk
    q_chunk = (iq * t + lax.broadcasted_iota(jnp.int32, (t, 1), 0)) // CHUNK
    k_chunk = (n_full * tk + lax.broadcasted_iota(jnp.int32, (1, tk), 1)) // CHUNK
    visible = k_chunk <= q_chunk
    width = LANES if tk % LANES == 0 else tk

    def tile(j):
        return pl.ds(pl.multiple_of(j * tk, tk), tk)

    outs = []
    for m in range(2):
        cols = slice(m * d, (m + 1) * d)
        qm = (q_ref[:, cols] * scale).astype(BF16)

        def scores(j, qm=qm, cols=cols):
            kt = k_ref[tile(j), cols].astype(BF16)
            return lax.dot_general(qm, kt, NT_DIMS, preferred_element_type=F32)

        def pass1(j, mx, scores=scores):
            s = scores(j)
            s_ref[:, tile(j)] = s
            return jnp.maximum(mx, _fold_lanes(s, jnp.maximum))

        mx = _loop_by_two(n_full, pass1, jnp.full((t, width), -jnp.inf, F32))
        s_last = jnp.where(visible, scores(n_full), -jnp.inf)
        s_ref[:, tile(n_full)] = s_last
        mx = jnp.max(jnp.maximum(mx, _fold_lanes(s_last, jnp.maximum)), axis=-1, keepdims=True)
        acc_ref[...] = jnp.zeros(acc_ref.shape, F32)

        def pass2(j, den, mx=mx):
            pr = jnp.exp(s_ref[:, tile(j)] - mx)
            vt = v_ref[tile(j), :].astype(BF16)
            acc_ref[...] += jnp.dot(pr.astype(BF16), vt, preferred_element_type=F32)
            return den + _fold_lanes(pr, jnp.add)

        den = _loop_by_two(n_full + 1, pass2, jnp.zeros((t, width), F32))
        outs.append(acc_ref[...] / jnp.sum(den, axis=-1, keepdims=True))
    o = outs[0] - lam * outs[1]
    o_ref[...] = (_rms(o, sub_ref[...]) * (1.0 - lam_init)).astype(BF16)


def _lam_specs(d):
    fixed = lambda *_: (0, 0)
    return [pl.BlockSpec((1, d), fixed)] * 4 + [pl.BlockSpec((1, 2 * d), fixed)]


def diff_attn_prompt(proj, q_col0, k_col0, v_col0, nb, seq_len, heads, d, lam_vecs, subln, lam_init, *,
                     tq_pref=256):
    hw = 2 * d
    tq = _tile(seq_len, tq_pref, CHUNK)
    nq = seq_len // tq
    tk = 2 * tq if nq % 2 == 0 else tq
    assert q_col0 % hw == 0 and k_col0 % hw == 0 and v_col0 % hw == 0
    qc, kc, vc = q_col0 // hw, k_col0 // hw, v_col0 // hw
    return pl.pallas_call(
        functools.partial(_diff_prompt_kernel, t=tq, tk=tk, d=d, scale=1.0 / math.sqrt(d), lam_init=lam_init),
        grid=(nb, heads, nq),
        in_specs=[pl.BlockSpec((tq, hw), lambda b, h, i: (b * nq + i, qc + h)),
                  pl.BlockSpec((seq_len, hw), lambda b, h, i: (b, kc + h)),
                  pl.BlockSpec((seq_len, hw), lambda b, h, i: (b, vc + h))] + _lam_specs(d),
        out_specs=pl.BlockSpec((tq, hw), lambda b, h, i: (b * nq + i, h)),
        out_shape=jax.ShapeDtypeStruct((proj.shape[0], heads * hw), BF16),
        scratch_shapes=[pltpu.VMEM((tq, seq_len), F32), pltpu.VMEM((tq, hw), F32)],
        compiler_params=_params("parallel", "parallel", "parallel"),
    )(proj, proj, proj, *[v.reshape(1, d) for v in lam_vecs], subln.reshape(1, hw))


def _diff_sample_kernel(q_ref, kn_ref, vn_ref, kp_ref, vp_ref, lq1_ref, lk1_ref, lq2_ref, lk2_ref, sub_ref,
                        o_ref, *, heads, d, past, scale, lam_init):
    lam = _lambda(lq1_ref, lk1_ref, lq2_ref, lk2_ref, lam_init)
    hw = 2 * d
    for h in range(heads):
        vp = vp_ref[:, h * hw:(h + 1) * hw].astype(BF16)
        vn = vn_ref[:, h * hw:(h + 1) * hw].astype(BF16)
        outs = []
        for m in range(2):
            cols = slice(h * hw + m * d, h * hw + (m + 1) * d)
            qm = (q_ref[:, cols] * scale).astype(BF16)
            kp = kp_ref[pl.ds(2 * h + m, past, stride=2 * heads), :].astype(BF16)
            s_p = lax.dot_general(qm, kp, NT_DIMS, preferred_element_type=F32)
            s_n = lax.dot_general(qm, kn_ref[:, cols].astype(BF16), NT_DIMS, preferred_element_type=F32)
            mx = jnp.maximum(jnp.max(s_p, axis=-1, keepdims=True), jnp.max(s_n, axis=-1, keepdims=True))
            p_p = jnp.exp(s_p - mx)
            p_n = jnp.exp(s_n - mx)
            den = jnp.sum(p_p, axis=-1, keepdims=True) + jnp.sum(p_n, axis=-1, keepdims=True)
            acc = jnp.dot(p_p.astype(BF16), vp, preferred_element_type=F32)
            acc += jnp.dot(p_n.astype(BF16), vn, preferred_element_type=F32)
            outs.append(acc / den)
        o = outs[0] - lam * outs[1]
        o_ref[:, h * hw:(h + 1) * hw] = (_rms(o, sub_ref[...]) * (1.0 - lam_init)).astype(BF16)


def diff_attn_sample(proj, q_col0, k_col0, v_col0, row0, nb, seq_len, heads, d, k_cache, v_cache, layer,
                     lam_vecs, subln, lam_init, joint):
    width = heads * 2 * d
    past = v_cache.shape[2]
    assert row0 % seq_len == 0 and q_col0 % width == 0 and k_col0 % width == 0 and v_col0 % width == 0
    rb = row0 // seq_len

    def new_spec(col0):
        return pl.BlockSpec((seq_len, width), lambda b: (rb + b, col0 // width))

    in_specs, operands, aliases = _into(
        joint,
        [new_spec(q_col0), new_spec(k_col0), new_spec(v_col0),
         pl.BlockSpec((None, None, past * heads * 2, d), lambda b: (layer, b, 0, 0)),
         pl.BlockSpec((None, None, past, width), lambda b: (layer, b, 0, 0))] + _lam_specs(d),
        [proj, proj, proj, k_cache, v_cache] + [v.reshape(1, d) for v in lam_vecs] + [subln.reshape(1, 2 * d)])
    body = functools.partial(_diff_sample_kernel, heads=heads, d=d, past=past, scale=1.0 / math.sqrt(d),
                             lam_init=lam_init)
    return pl.pallas_call(
        _drop_ref(body, 10),
        grid=(nb,),
        in_specs=in_specs,
        out_specs=pl.BlockSpec((seq_len, width), lambda b: (rb + b, 0)),
        out_shape=jax.ShapeDtypeStruct(joint.shape, BF16),
        input_output_aliases=aliases,
        compiler_params=_params("parallel"),
    )(*operands)


def _xattn_kernel(q_ref, k_ref, v_ref, o_ref, *, heads, hd, scale):
    for h in range(heads):
        cols = slice(h * hd, (h + 1) * hd)
        qh = (q_ref[:, cols] * scale).astype(BF16)
        s = lax.dot_general(qh, k_ref[:, cols].astype(BF16), NT_DIMS, preferred_element_type=F32)
        pr = jnp.exp(s - jnp.max(s, axis=-1, keepdims=True))
        den = jnp.sum(pr, axis=-1, keepdims=True)
        o = jnp.dot(pr.astype(BF16), v_ref[:, cols].astype(BF16), preferred_element_type=F32)
        o_ref[:, cols] = (o / den).astype(BF16)


def cross_attn(q_all, row0, nb, seq_len, k_arr, v_arr, kv_specs, heads, hd, joint, *, tq_pref=512):
    width = heads * hd
    tq = _tile(seq_len, tq_pref, PACKED_ROWS)
    assert row0 % tq == 0
    blk0, per_seq = row0 // tq, seq_len // tq
    in_specs, operands, aliases = _into(
        joint, [pl.BlockSpec((tq, width), lambda b, t: (blk0 + b * per_seq + t, 0))] + kv_specs,
        [q_all, k_arr, v_arr])
    body = functools.partial(_xattn_kernel, heads=heads, hd=hd, scale=1.0 / math.sqrt(hd))
    return pl.pallas_call(
        _drop_ref(body, 3) if aliases else body,
        grid=(nb, per_seq),
        in_specs=in_specs,
        out_specs=pl.BlockSpec((tq, width), lambda b, t: (blk0 + b * per_seq + t, 0)),
        out_shape=jax.ShapeDtypeStruct((q_all.shape[0], width), BF16),
        input_output_aliases=aliases,
        compiler_params=_params("parallel", "parallel"),
    )(*operands)


def kernel(x_prompt, x_sample, state_ssm, state_ssm_conv, state_pool, cache_diff_k, cache_diff_v, cache_mem_k, cache_mem_v, state_ffn_conv, mem_prompt, norm_mix_pre, norm_mix_post, w_in, ssm_conv_w, ssm_conv_b, ssm_dt_bias, ssm_a_log, ssm_d, ssm_norm, w_br_ssm, pool_w, pool_scale, w_br_pool, diff_lq1, diff_lk1, diff_lq2, diff_lk2, diff_subln, w_br_diff, w_o, norm_xa_pre, norm_xa_post, norm_mem, xa_wq, xa_wk, xa_wv, xa_wo, norm_ffn_pre, norm_ffn_post, ffn_w_up, ffn_conv_w, ffn_conv_b, ffn_w_down):
    bp, lp, dm = x_prompt.shape
    bs, ls, _ = x_sample.shape
    depth = w_in.shape[0]
    tp, ts = bp * lp, bs * ls
    heads, p_dim, n_state = state_ssm.shape[2:]
    d_inner = heads * p_dim
    xbc_w = state_ssm_conv.shape[3]
    groups = (xbc_w - d_inner) // (2 * n_state)
    k_ssm = ssm_conv_w.shape[1]
    pool_hist, pool_width = state_pool.shape[2:]
    past, dheads, _, dh = cache_diff_k.shape[2:]
    diff_w = dheads * 2 * dh
    n_mem, xheads, xhd = cache_mem_k.shape[2:]
    xa_w = xheads * xhd
    d_ff = ffn_w_down.shape[1]
    k_ffn = ffn_conv_w.shape[1]
    ffp = _round_up(d_ff, 1024)
    assert lp >= pool_hist and ls >= pool_hist and min(lp, ls) >= max(k_ssm, k_ffn) - 1

    o_z, o_xbc, o_dt = 0, d_inner, d_inner + xbc_w
    o_u = o_dt + heads
    o_q = o_u + pool_width
    o_g = o_q + 3 * diff_w

    def seg(off, width):
        return w_in[:, :, off:off + width]

    w_main = jnp.concatenate(
        [seg(o_g, 3 * dm), seg(o_xbc, xbc_w), seg(o_q, 3 * diff_w), seg(o_z, d_inner), seg(o_u, pool_width)],
        axis=-1).astype(BF16)
    c_g, c_xbc = 0, 3 * dm
    c_q = c_xbc + xbc_w
    c_k, c_v = c_q + diff_w, c_q + 2 * diff_w
    c_z = c_q + 3 * diff_w
    c_u = c_z + d_inner
    dt_pad = _round_up(heads, LANES)
    w_dt = jnp.pad(seg(o_dt, heads), ((0, 0), (0, 0), (0, dt_pad - heads))).astype(BF16)
    w_bs, w_bp, w_bd = w_br_ssm.astype(BF16), w_br_pool.astype(BF16), w_br_diff.astype(BF16)
    w_out = w_o.astype(BF16)
    pool_wb = pool_w.astype(BF16)
    w_q = xa_wq.astype(BF16)
    w_kv = jnp.concatenate([xa_wk, xa_wv], axis=-1).astype(BF16)
    w_xo = xa_wo.astype(BF16)

    def pad_ff(a):
        pad = [(0, 0)] * (a.ndim - 1) + [(0, ffp - d_ff)]
        return jnp.concatenate([jnp.pad(a[..., :d_ff], pad), jnp.pad(a[..., d_ff:], pad)], axis=-1)

    w_up = cast_pad_halves(ffn_w_up, d_ff, ffp)
    conv_w_ff = pad_ff(ffn_conv_w)
    conv_b_ff = pad_ff(ffn_conv_b)
    ffn_hist = pad_ff(state_ffn_conv)
    w_down = ffn_w_down.astype(BF16)
    if d_ff % LANES:
        w_down = jnp.pad(w_down, ((0, 0), (0, ffp - d_ff), (0, 0)))

    k_cache = cache_diff_k.reshape(depth, bs, past * dheads * 2, dh)
    v_cache = cache_diff_v.reshape(depth, bs, past, diff_w)
    mem_k = cache_mem_k.reshape(depth, bs, n_mem, xa_w)
    mem_v = cache_mem_v.reshape(depth, bs, n_mem, xa_w)
    mem_rows = mem_prompt.reshape(bp * n_mem, dm)

    h = jnp.concatenate([x_prompt.reshape(tp, dm), x_sample.reshape(ts, dm)], axis=0)
    xn = rmsnorm_bf16(h, norm_mix_pre[0])
    zeros_p = lambda *shape: jnp.zeros((bp,) + shape, F32)
    q_ssd = _tile(lp, 128, LANES)
    nsub_ssd = _tile(lp // q_ssd, 4, 1)

    def tails(arr, col0, width, n):
        a_p = jnp.stack([arr[(b + 1) * lp - n:(b + 1) * lp, col0:col0 + width] for b in range(bp)])
        a_s = arr[tp:, col0:col0 + width].reshape(bs, ls, width)[:, ls - n:]
        return a_p, a_s

    def both(arr, col0, width):
        return arr[:tp, col0:col0 + width], arr[tp:, col0:col0 + width]

    outs = [[] for _ in range(14)]
    for l in range(depth):
        lam_init = 0.8 - 0.6 * math.exp(-0.3 * l)
        lam_vecs = (diff_lq1[l], diff_lk1[l], diff_lq2[l], diff_lk2[l])

        proj = mm(xn, w_main, l)
        dt_raw = mm(xn, w_dt, l)[:, :heads]
        y_ssm = y_pool = None
        h_last = []
        for (row0, nb, sl, conv_hist, h0, p_hist, pos_base, q_chunk, nsub) in (
                (0, bp, lp, zeros_p(k_ssm - 1, xbc_w), zeros_p(heads, p_dim, n_state),
                 zeros_p(pool_hist, pool_width), 0, q_ssd, nsub_ssd),
                (tp, bs, ls, state_ssm_conv[l], state_ssm[l], state_pool[l], past, ls, 1)):
            xbc = conv_silu(proj, c_xbc, xbc_w, row0, nb, sl, conv_hist, ssm_conv_w[l], ssm_conv_b[l])
            dt_g = dt_raw[row0:row0 + nb * sl].reshape(nb, sl, heads)
            y_ssm, hl = ssd_mix(xbc, proj, c_z, row0, nb, sl, dt_g, ssm_dt_bias[l], ssm_a_log[l], ssm_d[l],
                                ssm_norm[l], h0, y_ssm, groups=groups, n_state=n_state, q=q_chunk, nsub=nsub)
            h_last.append(hl)
            y_pool = pool_mix(proj, c_u, row0, nb, sl, p_hist, pos_base, pool_wb[l], pool_scale[l], y_pool)
        y_diff = diff_attn_prompt(proj, c_q, c_k, c_v, bp, lp, dheads, dh, lam_vecs, diff_subln[l], lam_init)
        y_diff = diff_attn_sample(proj, c_q, c_k, c_v, tp, bs, ls, dheads, dh, k_cache, v_cache, l,
                                  lam_vecs, diff_subln[l], lam_init, y_diff)
        merged = branch_merge(y_ssm, y_pool, y_diff, proj, c_g, w_bs, w_bp, w_bd, l)
        h, xn = post_norm(mm(merged, w_out, l), h, norm_mix_post[l], norm_xa_pre[l])

        mem_kv = mm(rmsnorm_bf16(mem_rows, norm_mem[l]), w_kv, l)
        q_xa = mm(xn, w_q, l)
        kv_p = [pl.BlockSpec((n_mem, xa_w), lambda b, t: (b, 0)), pl.BlockSpec((n_mem, xa_w), lambda b, t: (b, 1))]
        kv_s = [pl.BlockSpec((None, None, n_mem, xa_w), lambda b, t, l=l: (l, b, 0, 0))] * 2
        o_xa = cross_attn(q_xa, 0, bp, lp, mem_kv, mem_kv, kv_p, xheads, xhd, None)
        o_xa = cross_attn(q_xa, tp, bs, ls, mem_k, mem_v, kv_s, xheads, xhd, o_xa)
        h, xn = post_norm(mm(o_xa, w_xo, l, tn_pref=dm), h, norm_xa_post[l], norm_ffn_pre[l])

        hid, tail_g, tail_v = ffn_up_geglu(xn, bp, lp, w_up, l, ffp, zeros_p(k_ffn - 1, 2 * ffp),
                                           conv_w_ff[l], conv_b_ff[l])
        up_s = mm(xn, w_up, l, row0=tp, tm_pref=512)
        hid = conv_geglu(up_s, ffp, bs, ls, ffn_hist[l], conv_w_ff[l], conv_b_ff[l], hid, tp)
        g_next = norm_mix_pre[l + 1] if l + 1 < depth else None
        res = post_norm(mm(hid, w_down, l, tm_pref=512, tn_pref=512), h, norm_ffn_post[l], g_next)
        h, xn = res if g_next is not None else (res, None)

        conv_p, conv_s = tails(proj, c_xbc, xbc_w, k_ssm - 1)
        pool_p, pool_s = tails(proj, c_u, pool_width, pool_hist)
        kk_p, kk_s = both(proj, c_k, diff_w)
        vv_p, vv_s = both(proj, c_v, diff_w)
        ups = up_s.reshape(bs, ls, 2 * ffp)[:, ls - (k_ffn - 1):]
        ffn_p = jnp.concatenate([tail_g[:, -(k_ffn - 1):, :d_ff], tail_v[:, -(k_ffn - 1):, :d_ff]], axis=-1)
        ffn_s = jnp.concatenate([ups[..., :d_ff], ups[..., ffp:ffp + d_ff]], axis=-1)
        layer_out = (h_last[0], h_last[1], conv_p, conv_s, pool_p, pool_s,
                     kk_p.reshape(bp, lp, dheads, 2, dh), kk_s.reshape(bs, ls, dheads, 2, dh),
                     vv_p.reshape(bp, lp, dheads, 2 * dh), vv_s.reshape(bs, ls, dheads, 2 * dh),
                     mem_kv[:, :xa_w].reshape(bp, n_mem, xheads, xhd),
                     mem_kv[:, xa_w:].reshape(bp, n_mem, xheads, xhd),
                     ffn_p, ffn_s)
        for acc, val in zip(outs, layer_out):
            acc.append(val)

    return (h[:tp].reshape(bp, lp, dm), h[tp:].reshape(bs, ls, dm)) + tuple(jnp.stack(o) for o in outs)
```

```python
import functools
import math

import jax
import jax.numpy as jnp
from jax import lax
from jax.experimental import pallas as pl
from jax.experimental.pallas import tpu as pltpu

F32 = jnp.float32
BF16 = jnp.bfloat16
EPS = 1e-6
CHUNK = 64
POOL_WINDOWS = (2, 4, 8, 16)
SUBLANES = 8
PACKED_ROWS = 16
LANES = 128
VMEM_LIMIT_BYTES = 56 * 1024 * 1024
NT_DIMS = (((1,), (1,)), ((), ()))
TN_DIMS = (((0,), (0,)), ((), ()))
SINGLE = pl.Buffered(1)
FFN_ROW_CHUNK = 64
NORM_UNROLL = 8


def _tile(n, pref, align):
    t = (min(pref, n) // align) * align
    while t >= align:
        if n % t == 0:
            return t
        t -= align
    return n


def _round_up(n, m):
    return (n + m - 1) // m * m


def _params(*sem):
    return pltpu.CompilerParams(dimension_semantics=sem, vmem_limit_bytes=VMEM_LIMIT_BYTES)


def _sigmoid(x):
    return 1.0 / (1.0 + jnp.exp(-x))


def _softplus(x):
    return jnp.maximum(x, 0.0) + jnp.log(1.0 + jnp.exp(-jnp.abs(x)))


def _gelu_tanh(x):
    return 0.5 * x * (1.0 + jnp.tanh(math.sqrt(2.0 / math.pi) * (x + 0.044715 * (x * x * x))))


def _rms(x, g):
    return x * lax.rsqrt(jnp.mean(x * x, axis=-1, keepdims=True) + EPS) * g


def _dot_sel(x, sel, sel_first=False):
    hi = x.astype(BF16)
    rest = x - hi.astype(F32)
    mid = rest.astype(BF16)
    lo = (rest - mid.astype(F32)).astype(BF16)
    sel = sel.astype(BF16)
    dots = [jnp.dot(sel, t, preferred_element_type=F32) if sel_first else
            jnp.dot(t, sel, preferred_element_type=F32) for t in (hi, mid, lo)]
    return dots[0] + dots[1] + dots[2]


def _drop_ref(body, idx):
    def wrapped(*refs):
        return body(*refs[:idx], *refs[idx + 1:])
    return wrapped


def _into(joint, in_specs, operands):
    if joint is None:
        return in_specs, operands, {}
    return (in_specs + [pl.BlockSpec(memory_space=pl.ANY)], operands + [joint], {len(operands): 0})


def _norm_kernel(x_ref, g_ref, o_ref, *, rows):
    def body(r, carry):
        sl = pl.ds(pl.multiple_of(r * rows, rows), rows)
        o_ref[sl, :] = _rms(x_ref[sl, :], g_ref[...]).astype(BF16)
        return carry
    lax.fori_loop(0, x_ref.shape[0] // rows, body, 0, unroll=NORM_UNROLL)


def rmsnorm_bf16(x, g, *, tm_pref=256):
    t, d = x.shape
    tm = _tile(t, tm_pref, PACKED_ROWS)
    return pl.pallas_call(
        functools.partial(_norm_kernel, rows=PACKED_ROWS),
        grid=(t // tm,),
        in_specs=[pl.BlockSpec((tm, d), lambda i: (i, 0)), pl.BlockSpec((1, d), lambda i: (0, 0))],
        out_specs=pl.BlockSpec((tm, d), lambda i: (i, 0)),
        out_shape=jax.ShapeDtypeStruct((t, d), BF16),
        compiler_params=_params("parallel"),
    )(x, g.reshape(1, d))


def _mm_kernel(x_ref, w_ref, o_ref):
    o_ref[...] = jnp.dot(x_ref[...], w_ref[...], preferred_element_type=F32)


def mm(x, w_all, layer, *, row0=0, nrows=None, tm_pref=768, tn_pref=1024):
    k, n = w_all.shape[1:]
    assert k == x.shape[1] or (k < x.shape[1] and k % LANES == 0)
    nrows = x.shape[0] - row0 if nrows is None else nrows
    tm = _tile(math.gcd(nrows, row0) if row0 else nrows, tm_pref, PACKED_ROWS)
    tn = _tile(n, tn_pref, LANES)
    rb = row0 // tm
    return pl.pallas_call(
        _mm_kernel,
        grid=(nrows // tm, n // tn),
        in_specs=[pl.BlockSpec((tm, k), lambda i, j: (rb + i, 0)),
                  pl.BlockSpec((None, k, tn), lambda i, j: (layer, 0, j))],
        out_specs=pl.BlockSpec((tm, tn), lambda i, j: (i, j)),
        out_shape=jax.ShapeDtypeStruct((nrows, n), F32),
        compiler_params=_params("parallel", "parallel"),
    )(x, w_all)


def _regroup_kernel(a_ref, b_ref, o_ref, *, shift, shifted_blocks):
    j = pl.program_id(1)
    x = a_ref[...]
    if shift:
        moved = jnp.concatenate([x[:, shift:], b_ref[:, :shift]], axis=1)
        is_moved = functools.reduce(jnp.logical_or, [(j >= lo) & (j < hi) for lo, hi in shifted_blocks])
        x = jnp.where(is_moved, moved, x)
    o_ref[...] = x.astype(BF16)


def regroup_cast(w, segments, *, blk_pref=512):
    depth, k, n = w.shape
    blk = _tile(functools.reduce(math.gcd, [width for _, width in segments]), blk_pref, LANES)
    shifts = {off % blk for off, _ in segments} - {0}
    if blk % LANES or len(shifts) > 1 or any(sh >= LANES for sh in shifts):
        return jnp.concatenate([w[:, :, off:off + width] for off, width in segments], axis=-1).astype(BF16)
    shift = shifts.pop() if shifts else 0
    table, j0 = [], 0
    for off, width in segments:
        table.append((j0, j0 + width // blk, off // blk, off % blk != 0))
        j0 += width // blk
    last_lane_block = (n - 1) // LANES

    def src_block(j):
        return sum(jnp.where((j >= lo) & (j < hi), a0 + j - lo, 0) for lo, hi, a0, _ in table)

    return pl.pallas_call(
        functools.partial(_regroup_kernel, shift=shift,
                          shifted_blocks=[(lo, hi) for lo, hi, _, moved in table if moved]),
        grid=(depth, j0),
        in_specs=[pl.BlockSpec((None, k, blk), lambda l, j: (l, 0, src_block(j))),
                  pl.BlockSpec((None, k, LANES),
                               lambda l, j: (l, 0, jnp.minimum((src_block(j) + 1) * (blk // LANES), last_lane_block)))],
        out_specs=pl.BlockSpec((None, k, blk), lambda l, j: (l, 0, j)),
        out_shape=jax.ShapeDtypeStruct((depth, k, j0 * blk), BF16),
        compiler_params=_params("parallel", "parallel"),
    )(w, w)


def _cast_pad_kernel(x_ref, o_ref, *, real_blocks, blocks):
    is_real = pl.program_id(1) % blocks < real_blocks
    o_ref[...] = jnp.where(is_real, x_ref[...], 0.0).astype(BF16)


def cast_pad_halves(w, half, half_pad, *, blk_pref=512):
    depth, k, _ = w.shape
    blk = _tile(math.gcd(half, half_pad), blk_pref, LANES)
    if blk % LANES:
        pad = ((0, 0), (0, 0), (0, half_pad - half))
        return jnp.concatenate([jnp.pad(w[..., :half], pad), jnp.pad(w[..., half:], pad)], -1).astype(BF16)
    real_blocks, blocks = half // blk, half_pad // blk

    def src(l, j):
        return (l, 0, (j // blocks) * real_blocks + jnp.minimum(j % blocks, real_blocks - 1))

    return pl.pallas_call(
        functools.partial(_cast_pad_kernel, real_blocks=real_blocks, blocks=blocks),
        grid=(depth, 2 * blocks),
        in_specs=[pl.BlockSpec((None, k, blk), src)],
        out_specs=pl.BlockSpec((None, k, blk), lambda l, j: (l, 0, j)),
        out_shape=jax.ShapeDtypeStruct((depth, k, 2 * half_pad), BF16),
        compiler_params=_params("parallel", "parallel"),
    )(w)


def _post_norm_kernel(y_ref, res_ref, g_ref, gn_ref, h_ref, *maybe_xn_ref, rows):
    def body(r, carry):
        sl = pl.ds(pl.multiple_of(r * rows, rows), rows)
        h_new = res_ref[sl, :] + _rms(y_ref[sl, :], g_ref[...])
        h_ref[sl, :] = h_new
        for xn_ref in maybe_xn_ref:
            xn_ref[sl, :] = _rms(h_new, gn_ref[...]).astype(BF16)
        return carry
    lax.fori_loop(0, y_ref.shape[0] // rows, body, 0, unroll=NORM_UNROLL)


def _mm_post_norm_kernel(x_ref, w_ref, res_ref, g_ref, gn_ref, h_ref, *rest, rows):
    *maybe_xn_ref, y_ref = rest
    y_ref[...] = jnp.dot(x_ref[...], w_ref[...], preferred_element_type=F32)
    _post_norm_kernel(y_ref, res_ref, g_ref, gn_ref, h_ref, *maybe_xn_ref, rows=rows)


def mm_post_norm(x, w_all, layer, res, g, g_next, *, tm_pref=512):
    t, k = x.shape
    d = w_all.shape[2]
    tm = _tile(t, tm_pref, PACKED_ROWS)
    row_spec = pl.BlockSpec((tm, d), lambda i: (i, 0))
    vec_spec = pl.BlockSpec((1, d), lambda i: (0, 0))
    emit = g_next is not None
    out = pl.pallas_call(
        functools.partial(_mm_post_norm_kernel, rows=PACKED_ROWS),
        grid=(t // tm,),
        in_specs=[pl.BlockSpec((tm, k), lambda i: (i, 0)),
                  pl.BlockSpec((None, k, d), lambda i: (layer, 0, 0), pipeline_mode=SINGLE),
                  pl.BlockSpec((tm, d), lambda i: (i, 0), pipeline_mode=SINGLE), vec_spec, vec_spec],
        out_specs=[row_spec, row_spec] if emit else [row_spec],
        out_shape=[jax.ShapeDtypeStruct((t, d), F32)] + ([jax.ShapeDtypeStruct((t, d), BF16)] if emit else []),
        scratch_shapes=[pltpu.VMEM((tm, d), F32)],
        compiler_params=_params("parallel"),
    )(x, w_all, res, g.reshape(1, d), (g_next if emit else g).reshape(1, d))
    return tuple(out) if emit else out[0]


def post_norm(y, res, g, g_next, *, tm_pref=256):
    t, d = y.shape
    tm = _tile(t, tm_pref, PACKED_ROWS)
    row_spec = pl.BlockSpec((tm, d), lambda i: (i, 0))
    vec_spec = pl.BlockSpec((1, d), lambda i: (0, 0))
    emit = g_next is not None
    out = pl.pallas_call(
        functools.partial(_post_norm_kernel, rows=PACKED_ROWS),
        grid=(t // tm,),
        in_specs=[row_spec, row_spec, vec_spec, vec_spec],
        out_specs=[row_spec, row_spec] if emit else [row_spec],
        out_shape=[jax.ShapeDtypeStruct((t, d), F32)] + ([jax.ShapeDtypeStruct((t, d), BF16)] if emit else []),
        compiler_params=_params("parallel"),
    )(y, res, g.reshape(1, d), (g_next if emit else g).reshape(1, d))
    return tuple(out) if emit else out[0]


def _merge_kernel(ys_ref, yp_ref, yd_ref, g0_ref, g1_ref, g2_ref, ws_ref, wp_ref, wd_ref, o_ref):
    acc = _sigmoid(g0_ref[...]) * jnp.dot(ys_ref[...], ws_ref[...], preferred_element_type=F32)
    acc += _sigmoid(g1_ref[...]) * jnp.dot(yp_ref[...], wp_ref[...], preferred_element_type=F32)
    acc += _sigmoid(g2_ref[...]) * jnp.dot(yd_ref[...], wd_ref[...], preferred_element_type=F32)
    o_ref[...] = acc.astype(BF16)


def branch_merge(y_ssm, y_pool, y_diff, proj, gate_col0, w_s, w_p, w_d, layer, *, tm_pref=512, tn_pref=1024):
    t = y_ssm.shape[0]
    d = w_s.shape[2]
    tm = _tile(t, tm_pref, PACKED_ROWS)
    tn = _tile(d, tn_pref, LANES)
    assert gate_col0 % tn == 0
    nj = d // tn
    gj = gate_col0 // tn

    def gate_spec(br):
        return pl.BlockSpec((tm, tn), lambda i, j: (i, gj + br * nj + j))

    def x_spec(kdim):
        return pl.BlockSpec((tm, kdim), lambda i, j: (i, 0))

    def w_spec(kdim):
        return pl.BlockSpec((None, kdim, tn), lambda i, j: (layer, 0, j))

    return pl.pallas_call(
        _merge_kernel,
        grid=(t // tm, nj),
        in_specs=[x_spec(y_ssm.shape[1]), x_spec(y_pool.shape[1]), x_spec(y_diff.shape[1]),
                  gate_spec(0), gate_spec(1), gate_spec(2),
                  w_spec(w_s.shape[1]), w_spec(w_p.shape[1]), w_spec(w_d.shape[1])],
        out_specs=pl.BlockSpec((tm, tn), lambda i, j: (i, j)),
        out_shape=jax.ShapeDtypeStruct((t, d), BF16),
        compiler_params=_params("parallel", "parallel"),
    )(y_ssm, y_pool, y_diff, proj, proj, proj, w_s, w_p, w_d)


def _conv_rows(ext_ref, u, halo, w_ref, b_ref, k):
    n = u.shape[0]
    ext_ref[0:SUBLANES, :] = halo
    ext_ref[SUBLANES:SUBLANES + n, :] = u
    acc = u * w_ref[k - 1:k, :]
    for i in range(k - 1):
        acc = acc + ext_ref[pl.ds(SUBLANES - (k - 1) + i, n), :] * w_ref[i:i + 1, :]
    return acc + b_ref[...]


def _halo(prev_ref, hist_ref, s, nseq):
    if nseq > 1:
        return hist_ref[s]
    return jnp.where(pl.program_id(1) == 0, hist_ref[0], prev_ref[...])


def _conv_silu_kernel(u_ref, prev_ref, hist_ref, w_ref, b_ref, o_ref, ext_ref, *, k, sl, nseq):
    for s in range(nseq):
        rows = slice(s * sl, (s + 1) * sl)
        y = _conv_rows(ext_ref, u_ref[rows, :], _halo(prev_ref, hist_ref, s, nseq), w_ref, b_ref, k)
        o_ref[rows, :] = y * _sigmoid(y)


def _seq_tiling(nb, seq_len, tl_pref, align):
    if seq_len >= tl_pref:
        return _tile(seq_len, tl_pref, align), 1
    nseq = _tile(nb, max(tl_pref // seq_len, 1), 1)
    return nseq * seq_len, nseq


def _seq_specs(row0, rows_per_b, tl, tc, col_blk0):
    blk0 = row0 // tl
    per_b = rows_per_b // tl
    sub0 = row0 // SUBLANES
    sub_per_b = rows_per_b // SUBLANES
    sub_per_tile = tl // SUBLANES
    cur = pl.BlockSpec((tl, tc), lambda b, t, c: (blk0 + b * per_b + t, col_blk0 + c))
    prev = pl.BlockSpec(
        (SUBLANES, tc),
        lambda b, t, c: (jnp.maximum(sub0 + b * sub_per_b + t * sub_per_tile - 1, 0), col_blk0 + c))
    return cur, prev


def _pad_hist(hist, rows):
    return jnp.pad(hist, ((0, 0), (rows - hist.shape[1], 0), (0, 0)))


def conv_silu(proj, col0, width, row0, nb, seq_len, hist, w, bias, *, tl_pref=512, tc_pref=512):
    k = w.shape[0]
    tl, nseq = _seq_tiling(nb, seq_len, tl_pref, SUBLANES)
    tc = _tile(width, tc_pref, LANES)
    rows_per_b = max(tl, seq_len)
    assert row0 % tl == 0 and col0 % tc == 0
    cur, prev = _seq_specs(row0, rows_per_b, tl, tc, col0 // tc)
    per_b = rows_per_b // tl
    return pl.pallas_call(
        functools.partial(_conv_silu_kernel, k=k, sl=tl // nseq, nseq=nseq),
        grid=(nb // nseq, per_b, width // tc),
        in_specs=[cur, prev,
                  pl.BlockSpec((nseq, SUBLANES, tc), lambda b, t, c: (b, 0, c)),
                  pl.BlockSpec((k, tc), lambda b, t, c: (0, c)),
                  pl.BlockSpec((1, tc), lambda b, t, c: (0, c))],
        out_specs=pl.BlockSpec((tl, tc), lambda b, t, c: (b * per_b + t, c)),
        out_shape=jax.ShapeDtypeStruct((nb * seq_len, width), F32),
        scratch_shapes=[pltpu.VMEM((SUBLANES + tl // nseq, tc), F32)],
        compiler_params=_params("parallel", "parallel", "parallel"),
    )(proj, proj, _pad_hist(hist, SUBLANES), w, bias.reshape(1, width))


def _conv_geglu_kernel(ug_ref, pg_ref, hg_ref, wg_ref, bg_ref, uv_ref, pv_ref, hv_ref, wv_ref, bv_ref,
                       o_ref, extg_ref, extv_ref, *, k, sl, nseq):
    for s in range(nseq):
        rows = slice(s * sl, (s + 1) * sl)
        gate = _conv_rows(extg_ref, ug_ref[rows, :], _halo(pg_ref, hg_ref, s, nseq), wg_ref, bg_ref, k)
        val = _conv_rows(extv_ref, uv_ref[rows, :], _halo(pv_ref, hv_ref, s, nseq), wv_ref, bv_ref, k)
        o_ref[rows, :] = (_gelu_tanh(gate) * val).astype(BF16)


def conv_geglu(up, half, nb, seq_len, hist, w, bias, joint, out_row0, *, tl_pref=512, tc_pref=1024):
    k = w.shape[0]
    tl, nseq = _seq_tiling(nb, seq_len, tl_pref, PACKED_ROWS)
    tc = _tile(half, tc_pref, LANES)
    rows_per_b = max(tl, seq_len)
    per_b = rows_per_b // tl
    ncb = half // tc
    assert out_row0 % tl == 0
    ob = out_row0 // tl
    cur_g, prev_g = _seq_specs(0, rows_per_b, tl, tc, 0)
    cur_v, prev_v = _seq_specs(0, rows_per_b, tl, tc, ncb)
    hist8 = _pad_hist(hist, SUBLANES)
    bias2 = bias.reshape(1, 2 * half)

    def side(off):
        return [pl.BlockSpec((nseq, SUBLANES, tc), lambda b, t, c: (b, 0, off + c)),
                pl.BlockSpec((k, tc), lambda b, t, c: (0, off + c)),
                pl.BlockSpec((1, tc), lambda b, t, c: (0, off + c))]

    in_specs, operands, aliases = _into(
        joint, [cur_g, prev_g] + side(0) + [cur_v, prev_v] + side(ncb),
        [up, up, hist8, w, bias2, up, up, hist8, w, bias2])
    body = functools.partial(_conv_geglu_kernel, k=k, sl=tl // nseq, nseq=nseq)
    return pl.pallas_call(
        _drop_ref(body, 10) if aliases else body,
        grid=(nb // nseq, per_b, ncb),
        in_specs=in_specs,
        out_specs=pl.BlockSpec((tl, tc), lambda b, t, c: (ob + b * per_b + t, c)),
        out_shape=jax.ShapeDtypeStruct(joint.shape, BF16),
        input_output_aliases=aliases,
        scratch_shapes=[pltpu.VMEM((SUBLANES + tl // nseq, tc), F32)] * 2,
        compiler_params=_params("parallel", "parallel", "parallel"),
    )(*operands)


def _ffn_up_kernel(xc_ref, xp_ref, wg_ref, wv_ref, hg_ref, hv_ref, cwg_ref, cbg_ref, cwv_ref, cbv_ref,
                   hid_ref, tg_ref, tv_ref, xe_ref, eg_ref, ev_ref, gg_ref, *, k, tm, tiles_per_seq):
    halo = PACKED_ROWS

    @pl.when(pl.program_id(1) == 0)
    def _():
        xe_ref[0:halo, :] = xp_ref[...]
        xe_ref[halo:halo + tm, :] = xc_ref[...]

    first = pl.program_id(0) % tiles_per_seq == 0

    def project(w_ref, h_ref, e_ref, t_ref):
        e_ref[...] = jnp.dot(xe_ref[...], w_ref[...], preferred_element_type=F32)
        e_ref[0:halo, :] = jnp.where(first, h_ref[...], e_ref[0:halo, :])
        t_ref[...] = e_ref[tm:tm + halo, :]

    def conv(e_ref, cw_ref, cb_ref, r0, n):
        acc = e_ref[halo + r0:halo + r0 + n, :] * cw_ref[k - 1:k, :]
        for i in range(k - 1):
            acc = acc + e_ref[pl.ds(halo + r0 - (k - 1) + i, n), :] * cw_ref[i:i + 1, :]
        return acc + cb_ref[...]

    chunk = _tile(tm, FFN_ROW_CHUNK, PACKED_ROWS)
    project(wg_ref, hg_ref, eg_ref, tg_ref)
    for r0 in range(0, tm, chunk):
        gg_ref[r0:r0 + chunk, :] = _gelu_tanh(conv(eg_ref, cwg_ref, cbg_ref, r0, chunk))
    project(wv_ref, hv_ref, ev_ref, tv_ref)
    for r0 in range(0, tm, chunk):
        val = conv(ev_ref, cwv_ref, cbv_ref, r0, chunk)
        hid_ref[r0:r0 + chunk, :] = (gg_ref[r0:r0 + chunk, :] * val).astype(BF16)


def ffn_up_geglu(xn, nb, seq_len, w_up_all, layer, half, hist, conv_w, conv_b, *, tm_pref=1024, tf_pref=512):
    t, d = xn.shape
    k = conv_w.shape[0]
    halo = PACKED_ROWS
    tm = _tile(seq_len, tm_pref, halo)
    tf = _tile(half, tf_pref, LANES)
    tiles_per_seq = seq_len // tm
    ncb = half // tf
    hist16 = _pad_hist(hist, halo)
    bias2 = conv_b.reshape(1, 2 * half)

    def w_spec(off):
        return pl.BlockSpec((None, d, tf), lambda i, j: (layer, 0, off + j))

    def hist_spec(off):
        return pl.BlockSpec((None, halo, tf), lambda i, j: (i // tiles_per_seq, 0, off + j))

    def cw_spec(off):
        return pl.BlockSpec((k, tf), lambda i, j: (0, off + j))

    def cb_spec(off):
        return pl.BlockSpec((1, tf), lambda i, j: (0, off + j))

    tail_spec = pl.BlockSpec((None, halo, tf), lambda i, j: (i, 0, j))
    tail_shape = jax.ShapeDtypeStruct((nb * tiles_per_seq, halo, half), F32)
    hid, tail_g, tail_v = pl.pallas_call(
        functools.partial(_ffn_up_kernel, k=k, tm=tm, tiles_per_seq=tiles_per_seq),
        grid=(nb * tiles_per_seq, ncb),
        in_specs=[pl.BlockSpec((tm, d), lambda i, j: (i, 0), pipeline_mode=SINGLE),
                  pl.BlockSpec((halo, d), lambda i, j: (jnp.maximum(i * (tm // halo) - 1, 0), 0)),
                  w_spec(0), w_spec(ncb), hist_spec(0), hist_spec(ncb),
                  cw_spec(0), cb_spec(0), cw_spec(ncb), cb_spec(ncb)],
        out_specs=[pl.BlockSpec((tm, tf), lambda i, j: (i, j)), tail_spec, tail_spec],
        out_shape=[jax.ShapeDtypeStruct((t, half), BF16), tail_shape, tail_shape],
        scratch_shapes=[pltpu.VMEM((halo + tm, d), BF16), pltpu.VMEM((halo + tm, tf), F32),
                        pltpu.VMEM((halo + tm, tf), F32), pltpu.VMEM((tm, tf), F32)],
        compiler_params=_params("parallel", "arbitrary"),
    )(xn, xn, w_up_all, w_up_all, hist16, hist16, conv_w, bias2, conv_w, bias2)
    last = slice(tiles_per_seq - 1, None, tiles_per_seq)
    return hid, tail_g[last], tail_v[last]


def _pool_kernel(u_ref, prev_ref, hist_ref, pw_ref, ps_ref, o_ref, ext_ref, *, tl, halo, pos_base, gd):
    t = pl.program_id(1)
    ext_ref[0:halo, :] = jnp.where(t == 0, hist_ref[...], prev_ref[...])
    ext_ref[halo:halo + tl, :] = u_ref[...]
    pos = pos_base + t * tl + lax.broadcasted_iota(jnp.int32, (tl, 1), 0)
    for gi, win in enumerate(POOL_WINDOWS):
        cols = slice(gi * gd, (gi + 1) * gd)
        cur = u_ref[:, cols]
        wsum = cur
        for i in range(1, win):
            wsum = wsum + ext_ref[pl.ds(halo - i, tl), cols]
        cnt = jnp.minimum(pos + 1, win).astype(F32)
        pooled = wsum / cnt - cur
        y = jnp.dot(pooled.astype(BF16), pw_ref[gi], preferred_element_type=F32) * ps_ref[:, cols]
        o_ref[:, cols] = y.astype(BF16)


def pool_mix(proj, col0, row0, nb, seq_len, hist, pos_base, pool_w, pool_scale, joint, *, tl_pref=512):
    ng, gd, _ = pool_w.shape
    width = ng * gd
    halo = 2 * SUBLANES
    assert ng == len(POOL_WINDOWS) and max(POOL_WINDOWS) <= halo
    tl = _tile(seq_len, tl_pref, halo)
    assert row0 % tl == 0 and col0 % width == 0
    blk0, per_seq = row0 // tl, seq_len // tl
    h0, h_per_seq, h_per_tile = row0 // halo, seq_len // halo, tl // halo
    cb = col0 // width
    in_specs, operands, aliases = _into(
        joint,
        [pl.BlockSpec((tl, width), lambda b, t: (blk0 + b * per_seq + t, cb)),
         pl.BlockSpec((halo, width),
                      lambda b, t: (jnp.maximum(h0 + b * h_per_seq + t * h_per_tile - 1, 0), cb)),
         pl.BlockSpec((None, halo, width), lambda b, t: (b, 0, 0)),
         pl.BlockSpec((ng, gd, gd), lambda b, t: (0, 0, 0)),
         pl.BlockSpec((1, width), lambda b, t: (0, 0))],
        [proj, proj, _pad_hist(hist, halo), pool_w, pool_scale.reshape(1, width)])
    body = functools.partial(_pool_kernel, tl=tl, halo=halo, pos_base=pos_base, gd=gd)
    return pl.pallas_call(
        _drop_ref(body, 5) if aliases else body,
        grid=(nb, per_seq),
        in_specs=in_specs,
        out_specs=pl.BlockSpec((tl, width), lambda b, t: (blk0 + b * per_seq + t, 0)),
        out_shape=jax.ShapeDtypeStruct((proj.shape[0], width), BF16),
        input_output_aliases=aliases,
        scratch_shapes=[pltpu.VMEM((halo + tl, width), F32)],
        compiler_params=_params("parallel", "parallel"),
    )(*operands)


def _ssd_kernel(xs_ref, b_ref, c_ref, z_ref, dtc_ref, dtr_ref, biasc_ref, biasr_ref, alogc_ref, alogr_ref,
                dskip_ref, nw_ref, h0_ref, y_ref, hl_ref, ht_ref, *, q, nsub, hpg, p):
    ti = pl.program_id(2)
    gw = hpg * p
    n = b_ref.shape[1]

    @pl.when(ti == 0)
    def _():
        ht_ref[...] = h0_ref[...].reshape(gw, n).T

    dt_c_all = _softplus(dtc_ref[...] + biasc_ref[...])
    dt_r_all = _softplus(dtr_ref[...] + biasr_ref[...])
    a_c = -jnp.exp(alogc_ref[...])
    a_r = -jnp.exp(alogr_ref[...])
    row = lax.broadcasted_iota(jnp.int32, (q, q), 0)
    col = lax.broadcasted_iota(jnp.int32, (q, q), 1)
    lower = row >= col
    upper = row <= col

    def spread(width, rep):
        head = lax.broadcasted_iota(jnp.int32, (hpg, width), 0)
        lane = lax.broadcasted_iota(jnp.int32, (hpg, width), 1)
        return lane // rep == head

    to_lanes = spread(gw, p)
    to_keys = spread(hpg * q, q)
    lane_head = lax.broadcasted_iota(jnp.int32, (1, gw), 1) // p
    d_skip = _dot_sel(dskip_ref[...], to_lanes)
    ht = ht_ref[...]
    for s in range(nsub):
        rows = slice(s * q, (s + 1) * q)
        dt_c = dt_c_all[rows, :]
        dt_r = dt_r_all[:, rows]
        acum_c = _dot_sel(dt_c * a_c, lower, sel_first=True)
        acum_r = _dot_sel(dt_r * a_r, upper)
        a_last = acum_c[q - 1:q, :]
        per_head = jnp.concatenate([jnp.exp(acum_c), jnp.exp(a_last - acum_c) * dt_c], axis=0)
        per_lane = _dot_sel(per_head, to_lanes)
        exp_a, w_end = per_lane[:q], per_lane[q:]
        chunk_decay = _dot_sel(jnp.exp(a_last), to_lanes)
        ac_keys = _dot_sel(acum_c, to_keys)
        x = xs_ref[rows, :]
        bm = b_ref[rows, :].astype(BF16)
        cm = c_ref[rows, :].astype(BF16)
        cb = lax.dot_general(cm, bm, NT_DIMS, preferred_element_type=F32)
        wmats = []
        for r in range(hpg):
            seg = ac_keys[:, r * q:(r + 1) * q] - acum_r[r:r + 1, :]
            decay = jnp.exp(jnp.where(lower, seg, -jnp.inf))
            wmats.append((cb * decay * dt_r[r:r + 1, :]).astype(BF16))
        y_all = jnp.dot(jnp.concatenate(wmats, axis=0), x.astype(BF16), preferred_element_type=F32)
        y = y_all[:q]
        for r in range(1, hpg):
            y = jnp.where(lane_head == r, y_all[r * q:(r + 1) * q], y)
        y = y + jnp.dot(cm, ht.astype(BF16), preferred_element_type=F32) * exp_a
        s_new = lax.dot_general(bm, (x * w_end).astype(BF16), TN_DIMS, preferred_element_type=F32)
        ht = ht * chunk_decay + s_new
        zt = z_ref[rows, :]
        yg = (y + x * d_skip) * (zt * _sigmoid(zt))
        y_ref[rows, :] = _rms(yg, nw_ref[...]).astype(BF16)
    ht_ref[...] = ht

    @pl.when(ti == pl.num_programs(2) - 1)
    def _():
        hl_ref[...] = ht.T.reshape(hpg, p, n)


def ssd_mix(xbc, proj, z_col0, row0, nb, seq_len, dt_raw, dt_bias, a_log, d_skip, norm_w, h0, joint, *,
            groups, n_state, q, nsub):
    heads, p = h0.shape[1], h0.shape[2]
    hpg = heads // groups
    gw = hpg * p
    d_inner = heads * p
    tl = q * nsub
    nt = seq_len // tl
    assert seq_len % tl == 0 and row0 % tl == 0 and z_col0 % gw == 0 and d_inner % n_state == 0
    dt_g = dt_raw.reshape(nb, seq_len, groups, hpg)
    dt_col = dt_g.transpose(0, 2, 1, 3)
    dt_row = dt_g.transpose(0, 2, 3, 1)
    blk0 = row0 // tl
    zc = z_col0 // gw
    bcol = d_inner // n_state

    def per_group(arr, shape):
        return arr.reshape((groups,) + shape), pl.BlockSpec((None,) + shape, lambda b, g, t: (g, 0, 0))

    biasc, biasc_spec = per_group(dt_bias, (1, hpg))
    biasr, biasr_spec = per_group(dt_bias, (hpg, 1))
    alogc, alogc_spec = per_group(a_log, (1, hpg))
    alogr, alogr_spec = per_group(a_log, (hpg, 1))
    dsk, dsk_spec = per_group(d_skip, (1, hpg))
    h_spec = pl.BlockSpec((None, hpg, p, n_state), lambda b, g, t: (b, g, 0, 0))
    in_specs, operands, aliases = _into(
        joint,
        [pl.BlockSpec((tl, gw), lambda b, g, t: (b * nt + t, g)),
         pl.BlockSpec((tl, n_state), lambda b, g, t: (b * nt + t, bcol + g)),
         pl.BlockSpec((tl, n_state), lambda b, g, t: (b * nt + t, bcol + groups + g)),
         pl.BlockSpec((tl, gw), lambda b, g, t: (blk0 + b * nt + t, zc + g)),
         pl.BlockSpec((None, None, tl, hpg), lambda b, g, t: (b, g, t, 0)),
         pl.BlockSpec((None, None, hpg, tl), lambda b, g, t: (b, g, 0, t)),
         biasc_spec, biasr_spec, alogc_spec, alogr_spec, dsk_spec,
         pl.BlockSpec((1, gw), lambda b, g, t: (0, g)),
         h_spec],
        [xbc, xbc, xbc, proj, dt_col, dt_row, biasc, biasr, alogc, alogr, dsk,
         norm_w.reshape(1, d_inner), h0])
    body = functools.partial(_ssd_kernel, q=q, nsub=nsub, hpg=hpg, p=p)
    return pl.pallas_call(
        _drop_ref(body, 13) if aliases else body,
        grid=(nb, groups, nt),
        in_specs=in_specs,
        out_specs=[pl.BlockSpec((tl, gw), lambda b, g, t: (blk0 + b * nt + t, g)), h_spec],
        out_shape=[jax.ShapeDtypeStruct((proj.shape[0], d_inner), BF16),
                   jax.ShapeDtypeStruct(h0.shape, F32)],
        input_output_aliases=aliases,
        scratch_shapes=[pltpu.VMEM((n_state, gw), F32)],
        compiler_params=_params("parallel", "parallel", "arbitrary"),
    )(*operands)


def _lambda(lq1_ref, lk1_ref, lq2_ref, lk2_ref, lam_init):
    s1 = jnp.sum(lq1_ref[...] * lk1_ref[...], axis=-1, keepdims=True)
    s2 = jnp.sum(lq2_ref[...] * lk2_ref[...], axis=-1, keepdims=True)
    return jnp.exp(s1) - jnp.exp(s2) + lam_init


def _loop_by_two(n, body, init):
    pairs = n // 2
    carry = lax.fori_loop(0, pairs, lambda i, c: body(2 * i + 1, body(2 * i, c)), init)
    return lax.fori_loop(2 * pairs, n, body, carry)


def _fold_lanes(a, op):
    if a.shape[1] % LANES:
        return a
    parts = [a[:, i:i + LANES] for i in range(0, a.shape[1], LANES)]
    while len(parts) > 1:
        parts = [op(parts[i], parts[i + 1]) for i in range(0, len(parts) - 1, 2)] + parts[len(parts) & ~1:]
    return parts[0]


def _diff_prompt_kernel(q_ref, k_ref, v_ref, lq1_ref, lk1_ref, lq2_ref, lk2_ref, sub_ref, o_ref,
                        s_ref, acc_ref, *, t, tk, d, scale, lam_init):
    iq = pl.program_id(2)
    lam = _lambda(lq1_ref, lk1_ref, lq2_ref, lk2_ref, lam_init)
    n_full = (iq * t) // tk
    q_chunk = (iq * t + lax.broadcasted_iota(jnp.int32, (t, 1), 0)) // CHUNK
    k_chunk = (n_full * tk + lax.broadcasted_iota(jnp.int32, (1, tk), 1)) // CHUNK
    visible = k_chunk <= q_chunk
    width = LANES if tk % LANES == 0 else tk

    def tile(j):
        return pl.ds(pl.multiple_of(j * tk, tk), tk)

    outs = []
    for m in range(2):
        cols = slice(m * d, (m + 1) * d)
        qm = (q_ref[:, cols] * scale).astype(BF16)

        def scores(j, qm=qm, cols=cols):
            kt = k_ref[tile(j), cols].astype(BF16)
            return lax.dot_general(qm, kt, NT_DIMS, preferred_element_type=F32)

        def pass1(j, mx, scores=scores):
            s = scores(j)
            s_ref[:, tile(j)] = s
            return jnp.maximum(mx, _fold_lanes(s, jnp.maximum))

        mx = _loop_by_two(n_full, pass1, jnp.full((t, width), -jnp.inf, F32))
        s_last = jnp.where(visible, scores(n_full), -jnp.inf)
        s_ref[:, tile(n_full)] = s_last
        mx = jnp.max(jnp.maximum(mx, _fold_lanes(s_last, jnp.maximum)), axis=-1, keepdims=True)
        acc_ref[...] = jnp.zeros(acc_ref.shape, F32)

        def pass2(j, den, mx=mx):
            pr = jnp.exp(s_ref[:, tile(j)] - mx)
            vt = v_ref[tile(j), :].astype(BF16)
            acc_ref[...] += jnp.dot(pr.astype(BF16), vt, preferred_element_type=F32)
            return den + _fold_lanes(pr, jnp.add)

        den = _loop_by_two(n_full + 1, pass2, jnp.zeros((t, width), F32))
        outs.append(acc_ref[...] / jnp.sum(den, axis=-1, keepdims=True))
    o = outs[0] - lam * outs[1]
    o_ref[...] = (_rms(o, sub_ref[...]) * (1.0 - lam_init)).astype(BF16)


def _lam_specs(d):
    fixed = lambda *_: (0, 0)
    return [pl.BlockSpec((1, d), fixed)] * 4 + [pl.BlockSpec((1, 2 * d), fixed)]


def diff_attn_prompt(proj, q_col0, k_col0, v_col0, nb, seq_len, heads, d, lam_vecs, subln, lam_init, *,
                     tq_pref=256):
    hw = 2 * d
    tq = _tile(seq_len, tq_pref, CHUNK)
    nq = seq_len // tq
    tk = 2 * tq if nq % 2 == 0 else tq
    assert q_col0 % hw == 0 and k_col0 % hw == 0 and v_col0 % hw == 0
    qc, kc, vc = q_col0 // hw, k_col0 // hw, v_col0 // hw
    return pl.pallas_call(
        functools.partial(_diff_prompt_kernel, t=tq, tk=tk, d=d, scale=1.0 / math.sqrt(d), lam_init=lam_init),
        grid=(nb, heads, nq),
        in_specs=[pl.BlockSpec((tq, hw), lambda b, h, i: (b * nq + i, qc + h)),
                  pl.BlockSpec((seq_len, hw), lambda b, h, i: (b, kc + h)),
                  pl.BlockSpec((seq_len, hw), lambda b, h, i: (b, vc + h))] + _lam_specs(d),
        out_specs=pl.BlockSpec((tq, hw), lambda b, h, i: (b * nq + i, h)),
        out_shape=jax.ShapeDtypeStruct((proj.shape[0], heads * hw), BF16),
        scratch_shapes=[pltpu.VMEM((tq, seq_len), F32), pltpu.VMEM((tq, hw), F32)],
        compiler_params=_params("parallel", "parallel", "parallel"),
    )(proj, proj, proj, *[v.reshape(1, d) for v in lam_vecs], subln.reshape(1, hw))


def _diff_sample_kernel(q_ref, kn_ref, vn_ref, kp_ref, vp_ref, lq1_ref, lk1_ref, lq2_ref, lk2_ref, sub_ref,
                        o_ref, *, heads, d, past, scale, lam_init):
    lam = _lambda(lq1_ref, lk1_ref, lq2_ref, lk2_ref, lam_init)
    hw = 2 * d
    for h in range(heads):
        vp = vp_ref[:, h * hw:(h + 1) * hw].astype(BF16)
        vn = vn_ref[:, h * hw:(h + 1) * hw].astype(BF16)
        outs = []
        for m in range(2):
            cols = slice(h * hw + m * d, h * hw + (m + 1) * d)
            qm = (q_ref[:, cols] * scale).astype(BF16)
            kp = kp_ref[pl.ds(2 * h + m, past, stride=2 * heads), :].astype(BF16)
            s_p = lax.dot_general(qm, kp, NT_DIMS, preferred_element_type=F32)
            s_n = lax.dot_general(qm, kn_ref[:, cols].astype(BF16), NT_DIMS, preferred_element_type=F32)
            mx = jnp.maximum(jnp.max(s_p, axis=-1, keepdims=True), jnp.max(s_n, axis=-1, keepdims=True))
            p_p = jnp.exp(s_p - mx)
            p_n = jnp.exp(s_n - mx)
            den = jnp.sum(p_p, axis=-1, keepdims=True) + jnp.sum(p_n, axis=-1, keepdims=True)
            acc = jnp.dot(p_p.astype(BF16), vp, preferred_element_type=F32)
            acc += jnp.dot(p_n.astype(BF16), vn, preferred_element_type=F32)
            outs.append(acc / den)
        o = outs[0] - lam * outs[1]
        o_ref[:, h * hw:(h + 1) * hw] = (_rms(o, sub_ref[...]) * (1.0 - lam_init)).astype(BF16)


def diff_attn_sample(proj, q_col0, k_col0, v_col0, row0, nb, seq_len, heads, d, k_cache, v_cache, layer,
                     lam_vecs, subln, lam_init, joint):
    width = heads * 2 * d
    past = v_cache.shape[2]
    assert row0 % seq_len == 0 and q_col0 % width == 0 and k_col0 % width == 0 and v_col0 % width == 0
    rb = row0 // seq_len

    def new_spec(col0):
        return pl.BlockSpec((seq_len, width), lambda b: (rb + b, col0 // width))

    in_specs, operands, aliases = _into(
        joint,
        [new_spec(q_col0), new_spec(k_col0), new_spec(v_col0),
         pl.BlockSpec((None, None, past * heads * 2, d), lambda b: (layer, b, 0, 0)),
         pl.BlockSpec((None, None, past, width), lambda b: (layer, b, 0, 0))] + _lam_specs(d),
        [proj, proj, proj, k_cache, v_cache] + [v.reshape(1, d) for v in lam_vecs] + [subln.reshape(1, 2 * d)])
    body = functools.partial(_diff_sample_kernel, heads=heads, d=d, past=past, scale=1.0 / math.sqrt(d),
                             lam_init=lam_init)
    return pl.pallas_call(
        _drop_ref(body, 10),
        grid=(nb,),
        in_specs=in_specs,
        out_specs=pl.BlockSpec((seq_len, width), lambda b: (rb + b, 0)),
        out_shape=jax.ShapeDtypeStruct(joint.shape, BF16),
        input_output_aliases=aliases,
        compiler_params=_params("parallel"),
    )(*operands)


def _xattn_kernel(q_ref, k_ref, v_ref, o_ref, *, heads, hd, scale):
    for h in range(heads):
        cols = slice(h * hd, (h + 1) * hd)
        qh = (q_ref[:, cols] * scale).astype(BF16)
        s = lax.dot_general(qh, k_ref[:, cols].astype(BF16), NT_DIMS, preferred_element_type=F32)
        pr = jnp.exp(s - jnp.max(s, axis=-1, keepdims=True))
        den = jnp.sum(pr, axis=-1, keepdims=True)
        o = jnp.dot(pr.astype(BF16), v_ref[:, cols].astype(BF16), preferred_element_type=F32)
        o_ref[:, cols] = (o / den).astype(BF16)


def cross_attn(q_all, row0, nb, seq_len, k_arr, v_arr, kv_specs, heads, hd, joint, *, tq_pref=512):
    width = heads * hd
    tq = _tile(seq_len, tq_pref, PACKED_ROWS)
    assert row0 % tq == 0
    blk0, per_seq = row0 // tq, seq_len // tq
    in_specs, operands, aliases = _into(
        joint, [pl.BlockSpec((tq, width), lambda b, t: (blk0 + b * per_seq + t, 0))] + kv_specs,
        [q_all, k_arr, v_arr])
    body = functools.partial(_xattn_kernel, heads=heads, hd=hd, scale=1.0 / math.sqrt(hd))
    return pl.pallas_call(
        _drop_ref(body, 3) if aliases else body,
        grid=(nb, per_seq),
        in_specs=in_specs,
        out_specs=pl.BlockSpec((tq, width), lambda b, t: (blk0 + b * per_seq + t, 0)),
        out_shape=jax.ShapeDtypeStruct((q_all.shape[0], width), BF16),
        input_output_aliases=aliases,
        compiler_params=_params("parallel", "parallel"),
    )(*operands)


def kernel(x_prompt, x_sample, state_ssm, state_ssm_conv, state_pool, cache_diff_k, cache_diff_v, cache_mem_k, cache_mem_v, state_ffn_conv, mem_prompt, norm_mix_pre, norm_mix_post, w_in, ssm_conv_w, ssm_conv_b, ssm_dt_bias, ssm_a_log, ssm_d, ssm_norm, w_br_ssm, pool_w, pool_scale, w_br_pool, diff_lq1, diff_lk1, diff_lq2, diff_lk2, diff_subln, w_br_diff, w_o, norm_xa_pre, norm_xa_post, norm_mem, xa_wq, xa_wk, xa_wv, xa_wo, norm_ffn_pre, norm_ffn_post, ffn_w_up, ffn_conv_w, ffn_conv_b, ffn_w_down):
    bp, lp, dm = x_prompt.shape
    bs, ls, _ = x_sample.shape
    depth = w_in.shape[0]
    tp, ts = bp * lp, bs * ls
    heads, p_dim, n_state = state_ssm.shape[2:]
    d_inner = heads * p_dim
    xbc_w = state_ssm_conv.shape[3]
    groups = (xbc_w - d_inner) // (2 * n_state)
    k_ssm = ssm_conv_w.shape[1]
    pool_hist, pool_width = state_pool.shape[2:]
    past, dheads, _, dh = cache_diff_k.shape[2:]
    diff_w = dheads * 2 * dh
    n_mem, xheads, xhd = cache_mem_k.shape[2:]
    xa_w = xheads * xhd
    d_ff = ffn_w_down.shape[1]
    k_ffn = ffn_conv_w.shape[1]
    ffp = _round_up(d_ff, 1024)
    assert lp >= pool_hist and ls >= pool_hist and min(lp, ls) >= max(k_ssm, k_ffn) - 1

    o_z, o_xbc, o_dt = 0, d_inner, d_inner + xbc_w
    o_u = o_dt + heads
    o_q = o_u + pool_width
    o_g = o_q + 3 * diff_w

    def seg(off, width):
        return w_in[:, :, off:off + width]

    w_main = regroup_cast(w_in, [(o_g, 3 * dm), (o_xbc, xbc_w), (o_q, 3 * diff_w), (o_z, d_inner),
                                 (o_u, pool_width)])
    c_g, c_xbc = 0, 3 * dm
    c_q = c_xbc + xbc_w
    c_k, c_v = c_q + diff_w, c_q + 2 * diff_w
    c_z = c_q + 3 * diff_w
    c_u = c_z + d_inner
    dt_pad = _round_up(heads, LANES)
    w_dt = jnp.pad(seg(o_dt, heads), ((0, 0), (0, 0), (0, dt_pad - heads))).astype(BF16)
    w_bs, w_bp, w_bd = w_br_ssm.astype(BF16), w_br_pool.astype(BF16), w_br_diff.astype(BF16)
    w_out = w_o.astype(BF16)
    pool_wb = pool_w.astype(BF16)
    w_q = xa_wq.astype(BF16)
    w_kv = jnp.concatenate([xa_wk, xa_wv], axis=-1).astype(BF16)
    w_xo = xa_wo.astype(BF16)

    def pad_ff(a):
        pad = [(0, 0)] * (a.ndim - 1) + [(0, ffp - d_ff)]
        return jnp.concatenate([jnp.pad(a[..., :d_ff], pad), jnp.pad(a[..., d_ff:], pad)], axis=-1)

    w_up = cast_pad_halves(ffn_w_up, d_ff, ffp)
    conv_w_ff = pad_ff(ffn_conv_w)
    conv_b_ff = pad_ff(ffn_conv_b)
    ffn_hist = pad_ff(state_ffn_conv)
    w_down = ffn_w_down.astype(BF16)
    if d_ff % LANES:
        w_down = jnp.pad(w_down, ((0, 0), (0, ffp - d_ff), (0, 0)))

    k_cache = cache_diff_k.reshape(depth, bs, past * dheads * 2, dh)
    v_cache = cache_diff_v.reshape(depth, bs, past, diff_w)
    mem_k = cache_mem_k.reshape(depth, bs, n_mem, xa_w)
    mem_v = cache_mem_v.reshape(depth, bs, n_mem, xa_w)
    mem_rows = mem_prompt.reshape(bp * n_mem, dm)

    h = jnp.concatenate([x_prompt.reshape(tp, dm), x_sample.reshape(ts, dm)], axis=0)
    xn = rmsnorm_bf16(h, norm_mix_pre[0])
    zeros_p = lambda *shape: jnp.zeros((bp,) + shape, F32)
    q_ssd = _tile(lp, 128, LANES)
    nsub_ssd = _tile(lp // q_ssd, 4, 1)

    def tails(arr, col0, width, n):
        a_p = jnp.stack([arr[(b + 1) * lp - n:(b + 1) * lp, col0:col0 + width] for b in range(bp)])
        a_s = arr[tp:, col0:col0 + width].reshape(bs, ls, width)[:, ls - n:]
        return a_p, a_s

    def both(arr, col0, width):
        return arr[:tp, col0:col0 + width], arr[tp:, col0:col0 + width]

    outs = [[] for _ in range(14)]
    for l in range(depth):
        lam_init = 0.8 - 0.6 * math.exp(-0.3 * l)
        lam_vecs = (diff_lq1[l], diff_lk1[l], diff_lq2[l], diff_lk2[l])

        proj = mm(xn, w_main, l)
        dt_raw = mm(xn, w_dt, l)[:, :heads]
        y_ssm = y_pool = None
        h_last = []
        for (row0, nb, sl, conv_hist, h0, p_hist, pos_base, q_chunk, nsub) in (
                (0, bp, lp, zeros_p(k_ssm - 1, xbc_w), zeros_p(heads, p_dim, n_state),
                 zeros_p(pool_hist, pool_width), 0, q_ssd, nsub_ssd),
                (tp, bs, ls, state_ssm_conv[l], state_ssm[l], state_pool[l], past, ls, 1)):
            xbc = conv_silu(proj, c_xbc, xbc_w, row0, nb, sl, conv_hist, ssm_conv_w[l], ssm_conv_b[l])
            dt_g = dt_raw[row0:row0 + nb * sl].reshape(nb, sl, heads)
            y_ssm, hl = ssd_mix(xbc, proj, c_z, row0, nb, sl, dt_g, ssm_dt_bias[l], ssm_a_log[l], ssm_d[l],
                                ssm_norm[l], h0, y_ssm, groups=groups, n_state=n_state, q=q_chunk, nsub=nsub)
            h_last.append(hl)
            y_pool = pool_mix(proj, c_u, row0, nb, sl, p_hist, pos_base, pool_wb[l], pool_scale[l], y_pool)
        y_diff = diff_attn_prompt(proj, c_q, c_k, c_v, bp, lp, dheads, dh, lam_vecs, diff_subln[l], lam_init)
        y_diff = diff_attn_sample(proj, c_q, c_k, c_v, tp, bs, ls, dheads, dh, k_cache, v_cache, l,
                                  lam_vecs, diff_subln[l], lam_init, y_diff)
        merged = branch_merge(y_ssm, y_pool, y_diff, proj, c_g, w_bs, w_bp, w_bd, l)
        h, xn = post_norm(mm(merged, w_out, l), h, norm_mix_post[l], norm_xa_pre[l])

        mem_kv = mm(rmsnorm_bf16(mem_rows, norm_mem[l]), w_kv, l)
        q_xa = mm(xn, w_q, l)
        kv_p = [pl.BlockSpec((n_mem, xa_w), lambda b, t: (b, 0)), pl.BlockSpec((n_mem, xa_w), lambda b, t: (b, 1))]
        kv_s = [pl.BlockSpec((None, None, n_mem, xa_w), lambda b, t, l=l: (l, b, 0, 0))] * 2
        o_xa = cross_attn(q_xa, 0, bp, lp, mem_kv, mem_kv, kv_p, xheads, xhd, None)
        o_xa = cross_attn(q_xa, tp, bs, ls, mem_k, mem_v, kv_s, xheads, xhd, o_xa)
        h, xn = mm_post_norm(o_xa, w_xo, l, h, norm_xa_post[l], norm_ffn_pre[l])

        hid, tail_g, tail_v = ffn_up_geglu(xn, bp, lp, w_up, l, ffp, zeros_p(k_ffn - 1, 2 * ffp),
                                           conv_w_ff[l], conv_b_ff[l])
        up_s = mm(xn, w_up, l, row0=tp, tm_pref=512)
        hid = conv_geglu(up_s, ffp, bs, ls, ffn_hist[l], conv_w_ff[l], conv_b_ff[l], hid, tp)
        g_next = norm_mix_pre[l + 1] if l + 1 < depth else None
        res = post_norm(mm(hid, w_down, l, tm_pref=512, tn_pref=512), h, norm_ffn_post[l], g_next)
        h, xn = res if g_next is not None else (res, None)

        conv_p, conv_s = tails(proj, c_xbc, xbc_w, k_ssm - 1)
        pool_p, pool_s = tails(proj, c_u, pool_width, pool_hist)
        kk_p, kk_s = both(proj, c_k, diff_w)
        vv_p, vv_s = both(proj, c_v, diff_w)
        ups = up_s.reshape(bs, ls, 2 * ffp)[:, ls - (k_ffn - 1):]
        ffn_p = jnp.concatenate([tail_g[:, -(k_ffn - 1):, :d_ff], tail_v[:, -(k_ffn - 1):, :d_ff]], axis=-1)
        ffn_s = jnp.concatenate([ups[..., :d_ff], ups[..., ffp:ffp + d_ff]], axis=-1)
        layer_out = (h_last[0], h_last[1], conv_p, conv_s, pool_p, pool_s,
                     kk_p.reshape(bp, lp, dheads, 2, dh), kk_s.reshape(bs, ls, dheads, 2, dh),
                     vv_p.reshape(bp, lp, dheads, 2 * dh), vv_s.reshape(bs, ls, dheads, 2 * dh),
                     mem_kv[:, :xa_w].reshape(bp, n_mem, xheads, xhd),
                     mem_kv[:, xa_w:].reshape(bp, n_mem, xheads, xhd),
                     ffn_p, ffn_s)
        for acc, val in zip(outs, layer_out):
            acc.append(val)

    return (h[:tp].reshape(bp, lp, dm), h[tp:].reshape(bs, ls, dm)) + tuple(jnp.stack(o) for o in outs)
```

```python
import functools
import math

import jax
import jax.numpy as jnp
from jax import lax
from jax.experimental import pallas as pl
from jax.experimental.pallas import tpu as pltpu

F32 = jnp.float32
BF16 = jnp.bfloat16
EPS = 1e-6
CHUNK = 64
POOL_WINDOWS = (2, 4, 8, 16)
SUBLANES = 8
PACKED_ROWS = 16
LANES = 128
VMEM_LIMIT_BYTES = 56 * 1024 * 1024
NT_DIMS = (((1,), (1,)), ((), ()))
TN_DIMS = (((0,), (0,)), ((), ()))
SINGLE = pl.Buffered(1)
FFN_ROW_CHUNK = 64
NORM_UNROLL = 8


def _tile(n, pref, align):
    t = (min(pref, n) // align) * align
    while t >= align:
        if n % t == 0:
            return t
        t -= align
    return n


def _round_up(n, m):
    return (n + m - 1) // m * m


def _params(*sem):
    return pltpu.CompilerParams(dimension_semantics=sem, vmem_limit_bytes=VMEM_LIMIT_BYTES)


def _sigmoid(x):
    return 1.0 / (1.0 + jnp.exp(-x))


def _softplus(x):
    return jnp.maximum(x, 0.0) + jnp.log(1.0 + jnp.exp(-jnp.abs(x)))


def _gelu_tanh(x):
    return 0.5 * x * (1.0 + jnp.tanh(math.sqrt(2.0 / math.pi) * (x + 0.044715 * (x * x * x))))


def _rms(x, g):
    return x * lax.rsqrt(jnp.mean(x * x, axis=-1, keepdims=True) + EPS) * g


def _dot_sel(x, sel, sel_first=False):
    hi = x.astype(BF16)
    rest = x - hi.astype(F32)
    mid = rest.astype(BF16)
    lo = (rest - mid.astype(F32)).astype(BF16)
    sel = sel.astype(BF16)
    dots = [jnp.dot(sel, t, preferred_element_type=F32) if sel_first else
            jnp.dot(t, sel, preferred_element_type=F32) for t in (hi, mid, lo)]
    return dots[0] + dots[1] + dots[2]


def _drop_ref(body, idx):
    def wrapped(*refs):
        return body(*refs[:idx], *refs[idx + 1:])
    return wrapped


def _into(joint, in_specs, operands):
    if joint is None:
        return in_specs, operands, {}
    return (in_specs + [pl.BlockSpec(memory_space=pl.ANY)], operands + [joint], {len(operands): 0})


def _norm_kernel(x_ref, g_ref, o_ref, *, rows):
    def body(r, carry):
        sl = pl.ds(pl.multiple_of(r * rows, rows), rows)
        o_ref[sl, :] = _rms(x_ref[sl, :], g_ref[...]).astype(BF16)
        return carry
    lax.fori_loop(0, x_ref.shape[0] // rows, body, 0, unroll=NORM_UNROLL)


def rmsnorm_bf16(x, g, *, tm_pref=256):
    t, d = x.shape
    tm = _tile(t, tm_pref, PACKED_ROWS)
    return pl.pallas_call(
        functools.partial(_norm_kernel, rows=PACKED_ROWS),
        grid=(t // tm,),
        in_specs=[pl.BlockSpec((tm, d), lambda i: (i, 0)), pl.BlockSpec((1, d), lambda i: (0, 0))],
        out_specs=pl.BlockSpec((tm, d), lambda i: (i, 0)),
        out_shape=jax.ShapeDtypeStruct((t, d), BF16),
        compiler_params=_params("parallel"),
    )(x, g.reshape(1, d))


def _mm_kernel(x_ref, w_ref, o_ref):
    o_ref[...] = jnp.dot(x_ref[...], w_ref[...], preferred_element_type=F32)


def mm(x, w_all, layer, *, row0=0, nrows=None, tm_pref=768, tn_pref=1024):
    k, n = w_all.shape[1:]
    assert k == x.shape[1] or (k < x.shape[1] and k % LANES == 0)
    nrows = x.shape[0] - row0 if nrows is None else nrows
    tm = _tile(math.gcd(nrows, row0) if row0 else nrows, tm_pref, PACKED_ROWS)
    tn = _tile(n, tn_pref, LANES)
    rb = row0 // tm
    return pl.pallas_call(
        _mm_kernel,
        grid=(nrows // tm, n // tn),
        in_specs=[pl.BlockSpec((tm, k), lambda i, j: (rb + i, 0)),
                  pl.BlockSpec((None, k, tn), lambda i, j: (layer, 0, j))],
        out_specs=pl.BlockSpec((tm, tn), lambda i, j: (i, j)),
        out_shape=jax.ShapeDtypeStruct((nrows, n), F32),
        compiler_params=_params("parallel", "parallel"),
    )(x, w_all)


def _cast_pad_kernel(x_ref, o_ref, *, real_blocks, blocks):
    is_real = pl.program_id(1) % blocks < real_blocks
    o_ref[...] = jnp.where(is_real, x_ref[...], 0.0).astype(BF16)


def cast_pad_halves(w, half, half_pad, *, blk_pref=512):
    depth, k, _ = w.shape
    blk = _tile(math.gcd(half, half_pad), blk_pref, LANES)
    if blk % LANES:
        pad = ((0, 0), (0, 0), (0, half_pad - half))
        return jnp.concatenate([jnp.pad(w[..., :half], pad), jnp.pad(w[..., half:], pad)], -1).astype(BF16)
    real_blocks, blocks = half // blk, half_pad // blk

    def src(l, j):
        return (l, 0, (j // blocks) * real_blocks + jnp.minimum(j % blocks, real_blocks - 1))

    return pl.pallas_call(
        functools.partial(_cast_pad_kernel, real_blocks=real_blocks, blocks=blocks),
        grid=(depth, 2 * blocks),
        in_specs=[pl.BlockSpec((None, k, blk), src)],
        out_specs=pl.BlockSpec((None, k, blk), lambda l, j: (l, 0, j)),
        out_shape=jax.ShapeDtypeStruct((depth, k, 2 * half_pad), BF16),
        compiler_params=_params("parallel", "parallel"),
    )(w)


def _post_norm_kernel(y_ref, res_ref, g_ref, gn_ref, h_ref, *maybe_xn_ref, rows):
    def body(r, carry):
        sl = pl.ds(pl.multiple_of(r * rows, rows), rows)
        h_new = res_ref[sl, :] + _rms(y_ref[sl, :], g_ref[...])
        h_ref[sl, :] = h_new
        for xn_ref in maybe_xn_ref:
            xn_ref[sl, :] = _rms(h_new, gn_ref[...]).astype(BF16)
        return carry
    lax.fori_loop(0, y_ref.shape[0] // rows, body, 0, unroll=NORM_UNROLL)


def _mm_post_norm_kernel(x_ref, w_ref, res_ref, g_ref, gn_ref, h_ref, *rest, rows):
    *maybe_xn_ref, y_ref = rest
    y_ref[...] = jnp.dot(x_ref[...], w_ref[...], preferred_element_type=F32)
    _post_norm_kernel(y_ref, res_ref, g_ref, gn_ref, h_ref, *maybe_xn_ref, rows=rows)


def mm_post_norm(x, w_all, layer, res, g, g_next, *, tm_pref=256):
    t, k = x.shape
    d = w_all.shape[2]
    tm = _tile(t, tm_pref, PACKED_ROWS)
    row_spec = pl.BlockSpec((tm, d), lambda i: (i, 0))
    vec_spec = pl.BlockSpec((1, d), lambda i: (0, 0))
    emit = g_next is not None
    out = pl.pallas_call(
        functools.partial(_mm_post_norm_kernel, rows=PACKED_ROWS),
        grid=(t // tm,),
        in_specs=[pl.BlockSpec((tm, k), lambda i: (i, 0)),
                  pl.BlockSpec((None, k, d), lambda i: (layer, 0, 0), pipeline_mode=SINGLE),
                  row_spec, vec_spec, vec_spec],
        out_specs=[row_spec, row_spec] if emit else [row_spec],
        out_shape=[jax.ShapeDtypeStruct((t, d), F32)] + ([jax.ShapeDtypeStruct((t, d), BF16)] if emit else []),
        scratch_shapes=[pltpu.VMEM((tm, d), F32)],
        compiler_params=_params("parallel"),
    )(x, w_all, res, g.reshape(1, d), (g_next if emit else g).reshape(1, d))
    return tuple(out) if emit else out[0]


def post_norm(y, res, g, g_next, *, tm_pref=256):
    t, d = y.shape
    tm = _tile(t, tm_pref, PACKED_ROWS)
    row_spec = pl.BlockSpec((tm, d), lambda i: (i, 0))
    vec_spec = pl.BlockSpec((1, d), lambda i: (0, 0))
    emit = g_next is not None
    out = pl.pallas_call(
        functools.partial(_post_norm_kernel, rows=PACKED_ROWS),
        grid=(t // tm,),
        in_specs=[row_spec, row_spec, vec_spec, vec_spec],
        out_specs=[row_spec, row_spec] if emit else [row_spec],
        out_shape=[jax.ShapeDtypeStruct((t, d), F32)] + ([jax.ShapeDtypeStruct((t, d), BF16)] if emit else []),
        compiler_params=_params("parallel"),
    )(y, res, g.reshape(1, d), (g_next if emit else g).reshape(1, d))
    return tuple(out) if emit else out[0]


def _merge_kernel(ys_ref, yp_ref, yd_ref, g0_ref, g1_ref, g2_ref, ws_ref, wp_ref, wd_ref, o_ref):
    acc = _sigmoid(g0_ref[...]) * jnp.dot(ys_ref[...], ws_ref[...], preferred_element_type=F32)
    acc += _sigmoid(g1_ref[...]) * jnp.dot(yp_ref[...], wp_ref[...], preferred_element_type=F32)
    acc += _sigmoid(g2_ref[...]) * jnp.dot(yd_ref[...], wd_ref[...], preferred_element_type=F32)
    o_ref[...] = acc.astype(BF16)


def branch_merge(y_ssm, y_pool, y_diff, proj, gate_col0, w_s, w_p, w_d, layer, *, tm_pref=512, tn_pref=1024):
    t = y_ssm.shape[0]
    d = w_s.shape[2]
    tm = _tile(t, tm_pref, PACKED_ROWS)
    tn = _tile(d, tn_pref, LANES)
    assert gate_col0 % tn == 0
    nj = d // tn
    gj = gate_col0 // tn

    def gate_spec(br):
        return pl.BlockSpec((tm, tn), lambda i, j: (i, gj + br * nj + j))

    def x_spec(kdim):
        return pl.BlockSpec((tm, kdim), lambda i, j: (i, 0))

    def w_spec(kdim):
        return pl.BlockSpec((None, kdim, tn), lambda i, j: (layer, 0, j))

    return pl.pallas_call(
        _merge_kernel,
        grid=(t // tm, nj),
        in_specs=[x_spec(y_ssm.shape[1]), x_spec(y_pool.shape[1]), x_spec(y_diff.shape[1]),
                  gate_spec(0), gate_spec(1), gate_spec(2),
                  w_spec(w_s.shape[1]), w_spec(w_p.shape[1]), w_spec(w_d.shape[1])],
        out_specs=pl.BlockSpec((tm, tn), lambda i, j: (i, j)),
        out_shape=jax.ShapeDtypeStruct((t, d), BF16),
        compiler_params=_params("parallel", "parallel"),
    )(y_ssm, y_pool, y_diff, proj, proj, proj, w_s, w_p, w_d)


def _conv_rows(ext_ref, u, halo, w_ref, b_ref, k):
    n = u.shape[0]
    ext_ref[0:SUBLANES, :] = halo
    ext_ref[SUBLANES:SUBLANES + n, :] = u
    acc = u * w_ref[k - 1:k, :]
    for i in range(k - 1):
        acc = acc + ext_ref[pl.ds(SUBLANES - (k - 1) + i, n), :] * w_ref[i:i + 1, :]
    return acc + b_ref[...]


def _halo(prev_ref, hist_ref, s, nseq):
    if nseq > 1:
        return hist_ref[s]
    return jnp.where(pl.program_id(1) == 0, hist_ref[0], prev_ref[...])


def _seq_tiling(nb, seq_len, tl_pref, align):
    if seq_len >= tl_pref:
        return _tile(seq_len, tl_pref, align), 1
    nseq = _tile(nb, max(tl_pref // seq_len, 1), 1)
    return nseq * seq_len, nseq


def _seq_specs(row0, rows_per_b, tl, tc, col_blk0):
    blk0 = row0 // tl
    per_b = rows_per_b // tl
    sub0 = row0 // SUBLANES
    sub_per_b = rows_per_b // SUBLANES
    sub_per_tile = tl // SUBLANES
    cur = pl.BlockSpec((tl, tc), lambda b, t, c: (blk0 + b * per_b + t, col_blk0 + c))
    prev = pl.BlockSpec(
        (SUBLANES, tc),
        lambda b, t, c: (jnp.maximum(sub0 + b * sub_per_b + t * sub_per_tile - 1, 0), col_blk0 + c))
    return cur, prev


def _pad_hist(hist, rows):
    return jnp.pad(hist, ((0, 0), (rows - hist.shape[1], 0), (0, 0)))


def _conv_geglu_kernel(ug_ref, pg_ref, hg_ref, wg_ref, bg_ref, uv_ref, pv_ref, hv_ref, wv_ref, bv_ref,
                       o_ref, extg_ref, extv_ref, *, k, sl, nseq):
    for s in range(nseq):
        rows = slice(s * sl, (s + 1) * sl)
        gate = _conv_rows(extg_ref, ug_ref[rows, :], _halo(pg_ref, hg_ref, s, nseq), wg_ref, bg_ref, k)
        val = _conv_rows(extv_ref, uv_ref[rows, :], _halo(pv_ref, hv_ref, s, nseq), wv_ref, bv_ref, k)
        o_ref[rows, :] = (_gelu_tanh(gate) * val).astype(BF16)


def conv_geglu(up, half, nb, seq_len, hist, w, bias, joint, out_row0, *, tl_pref=512, tc_pref=1024):
    k = w.shape[0]
    tl, nseq = _seq_tiling(nb, seq_len, tl_pref, PACKED_ROWS)
    tc = _tile(half, tc_pref, LANES)
    rows_per_b = max(tl, seq_len)
    per_b = rows_per_b // tl
    ncb = half // tc
    assert out_row0 % tl == 0
    ob = out_row0 // tl
    cur_g, prev_g = _seq_specs(0, rows_per_b, tl, tc, 0)
    cur_v, prev_v = _seq_specs(0, rows_per_b, tl, tc, ncb)
    hist8 = _pad_hist(hist, SUBLANES)
    bias2 = bias.reshape(1, 2 * half)

    def side(off):
        return [pl.BlockSpec((nseq, SUBLANES, tc), lambda b, t, c: (b, 0, off + c)),
                pl.BlockSpec((k, tc), lambda b, t, c: (0, off + c)),
                pl.BlockSpec((1, tc), lambda b, t, c: (0, off + c))]

    in_specs, operands, aliases = _into(
        joint, [cur_g, prev_g] + side(0) + [cur_v, prev_v] + side(ncb),
        [up, up, hist8, w, bias2, up, up, hist8, w, bias2])
    body = functools.partial(_conv_geglu_kernel, k=k, sl=tl // nseq, nseq=nseq)
    return pl.pallas_call(
        _drop_ref(body, 10) if aliases else body,
        grid=(nb // nseq, per_b, ncb),
        in_specs=in_specs,
        out_specs=pl.BlockSpec((tl, tc), lambda b, t, c: (ob + b * per_b + t, c)),
        out_shape=jax.ShapeDtypeStruct(joint.shape, BF16),
        input_output_aliases=aliases,
        scratch_shapes=[pltpu.VMEM((SUBLANES + tl // nseq, tc), F32)] * 2,
        compiler_params=_params("parallel", "parallel", "parallel"),
    )(*operands)


def _ffn_up_kernel(xc_ref, xp_ref, wg_ref, wv_ref, hg_ref, hv_ref, cwg_ref, cbg_ref, cwv_ref, cbv_ref,
                   hid_ref, tg_ref, tv_ref, xe_ref, eg_ref, ev_ref, gg_ref, *, k, tm, tiles_per_seq):
    halo = PACKED_ROWS

    @pl.when(pl.program_id(1) == 0)
    def _():
        xe_ref[0:halo, :] = xp_ref[...]
        xe_ref[halo:halo + tm, :] = xc_ref[...]

    first = pl.program_id(0) % tiles_per_seq == 0

    def project(w_ref, h_ref, e_ref, t_ref):
        e_ref[...] = jnp.dot(xe_ref[...], w_ref[...], preferred_element_type=F32)
        e_ref[0:halo, :] = jnp.where(first, h_ref[...], e_ref[0:halo, :])
        t_ref[...] = e_ref[tm:tm + halo, :]

    def conv(e_ref, cw_ref, cb_ref, r0, n):
        acc = e_ref[halo + r0:halo + r0 + n, :] * cw_ref[k - 1:k, :]
        for i in range(k - 1):
            acc = acc + e_ref[pl.ds(halo + r0 - (k - 1) + i, n), :] * cw_ref[i:i + 1, :]
        return acc + cb_ref[...]

    chunk = _tile(tm, FFN_ROW_CHUNK, PACKED_ROWS)
    project(wg_ref, hg_ref, eg_ref, tg_ref)
    for r0 in range(0, tm, chunk):
        gg_ref[r0:r0 + chunk, :] = _gelu_tanh(conv(eg_ref, cwg_ref, cbg_ref, r0, chunk))
    project(wv_ref, hv_ref, ev_ref, tv_ref)
    for r0 in range(0, tm, chunk):
        val = conv(ev_ref, cwv_ref, cbv_ref, r0, chunk)
        hid_ref[r0:r0 + chunk, :] = (gg_ref[r0:r0 + chunk, :] * val).astype(BF16)


def ffn_up_geglu(xn, nb, seq_len, w_up_all, layer, half, hist, conv_w, conv_b, *, tm_pref=1024, tf_pref=512):
    t, d = xn.shape
    k = conv_w.shape[0]
    halo = PACKED_ROWS
    tm = _tile(seq_len, tm_pref, halo)
    tf = _tile(half, tf_pref, LANES)
    tiles_per_seq = seq_len // tm
    ncb = half // tf
    hist16 = _pad_hist(hist, halo)
    bias2 = conv_b.reshape(1, 2 * half)

    def w_spec(off):
        return pl.BlockSpec((None, d, tf), lambda i, j: (layer, 0, off + j))

    def hist_spec(off):
        return pl.BlockSpec((None, halo, tf), lambda i, j: (i // tiles_per_seq, 0, off + j))

    def cw_spec(off):
        return pl.BlockSpec((k, tf), lambda i, j: (0, off + j))

    def cb_spec(off):
        return pl.BlockSpec((1, tf), lambda i, j: (0, off + j))

    tail_spec = pl.BlockSpec((None, halo, tf), lambda i, j: (i, 0, j))
    tail_shape = jax.ShapeDtypeStruct((nb * tiles_per_seq, halo, half), F32)
    hid, tail_g, tail_v = pl.pallas_call(
        functools.partial(_ffn_up_kernel, k=k, tm=tm, tiles_per_seq=tiles_per_seq),
        grid=(nb * tiles_per_seq, ncb),
        in_specs=[pl.BlockSpec((tm, d), lambda i, j: (i, 0), pipeline_mode=SINGLE),
                  pl.BlockSpec((halo, d), lambda i, j: (jnp.maximum(i * (tm // halo) - 1, 0), 0)),
                  w_spec(0), w_spec(ncb), hist_spec(0), hist_spec(ncb),
                  cw_spec(0), cb_spec(0), cw_spec(ncb), cb_spec(ncb)],
        out_specs=[pl.BlockSpec((tm, tf), lambda i, j: (i, j)), tail_spec, tail_spec],
        out_shape=[jax.ShapeDtypeStruct((t, half), BF16), tail_shape, tail_shape],
        scratch_shapes=[pltpu.VMEM((halo + tm, d), BF16), pltpu.VMEM((halo + tm, tf), F32),
                        pltpu.VMEM((halo + tm, tf), F32), pltpu.VMEM((tm, tf), F32)],
        compiler_params=_params("parallel", "arbitrary"),
    )(xn, xn, w_up_all, w_up_all, hist16, hist16, conv_w, bias2, conv_w, bias2)
    last = slice(tiles_per_seq - 1, None, tiles_per_seq)
    return hid, tail_g[last], tail_v[last]


def _pool_kernel(u_ref, prev_ref, hist_ref, pw_ref, ps_ref, o_ref, ext_ref, *, tl, halo, pos_base, gd):
    t = pl.program_id(1)
    ext_ref[0:halo, :] = jnp.where(t == 0, hist_ref[...], prev_ref[...])
    ext_ref[halo:halo + tl, :] = u_ref[...]
    pos = pos_base + t * tl + lax.broadcasted_iota(jnp.int32, (tl, 1), 0)
    for gi, win in enumerate(POOL_WINDOWS):
        cols = slice(gi * gd, (gi + 1) * gd)
        cur = u_ref[:, cols]
        wsum = cur
        for i in range(1, win):
            wsum = wsum + ext_ref[pl.ds(halo - i, tl), cols]
        cnt = jnp.minimum(pos + 1, win).astype(F32)
        pooled = wsum / cnt - cur
        y = jnp.dot(pooled.astype(BF16), pw_ref[gi], preferred_element_type=F32) * ps_ref[:, cols]
        o_ref[:, cols] = y.astype(BF16)


def pool_mix(proj, col0, row0, nb, seq_len, hist, pos_base, pool_w, pool_scale, joint, *, tl_pref=512):
    ng, gd, _ = pool_w.shape
    width = ng * gd
    halo = 2 * SUBLANES
    assert ng == len(POOL_WINDOWS) and max(POOL_WINDOWS) <= halo
    tl = _tile(seq_len, tl_pref, halo)
    assert row0 % tl == 0 and col0 % width == 0
    blk0, per_seq = row0 // tl, seq_len // tl
    h0, h_per_seq, h_per_tile = row0 // halo, seq_len // halo, tl // halo
    cb = col0 // width
    in_specs, operands, aliases = _into(
        joint,
        [pl.BlockSpec((tl, width), lambda b, t: (blk0 + b * per_seq + t, cb)),
         pl.BlockSpec((halo, width),
                      lambda b, t: (jnp.maximum(h0 + b * h_per_seq + t * h_per_tile - 1, 0), cb)),
         pl.BlockSpec((None, halo, width), lambda b, t: (b, 0, 0)),
         pl.BlockSpec((ng, gd, gd), lambda b, t: (0, 0, 0)),
         pl.BlockSpec((1, width), lambda b, t: (0, 0))],
        [proj, proj, _pad_hist(hist, halo), pool_w, pool_scale.reshape(1, width)])
    body = functools.partial(_pool_kernel, tl=tl, halo=halo, pos_base=pos_base, gd=gd)
    return pl.pallas_call(
        _drop_ref(body, 5) if aliases else body,
        grid=(nb, per_seq),
        in_specs=in_specs,
        out_specs=pl.BlockSpec((tl, width), lambda b, t: (blk0 + b * per_seq + t, 0)),
        out_shape=jax.ShapeDtypeStruct((proj.shape[0], width), BF16),
        input_output_aliases=aliases,
        scratch_shapes=[pltpu.VMEM((halo + tl, width), F32)],
        compiler_params=_params("parallel", "parallel"),
    )(*operands)


def _ssd_kernel(*refs, q, nsub, hpg, p, k):
    conv_in, refs = refs[:15], refs[15:]
    (z_ref, dtc_ref, dtr_ref, biasc_ref, biasr_ref, alogc_ref, alogr_ref, dskip_ref, nw_ref, h0_ref,
     y_ref, hl_ref, ht_ref, xs_ref, b_ref, c_ref, ext_x_ref, ext_b_ref, ext_c_ref) = refs
    ti = pl.program_id(2)
    gw = hpg * p
    n = b_ref.shape[1]
    for i, (dst_ref, ext_ref) in enumerate(((xs_ref, ext_x_ref), (b_ref, ext_b_ref), (c_ref, ext_c_ref))):
        cur_ref, prev_ref, hist_ref, cw_ref, cb_ref = conv_in[5 * i:5 * i + 5]
        halo = jnp.where(ti == 0, hist_ref[...], prev_ref[...])
        u = _conv_rows(ext_ref, cur_ref[...], halo, cw_ref, cb_ref, k)
        dst_ref[...] = u * _sigmoid(u)

    @pl.when(ti == 0)
    def _():
        ht_ref[...] = h0_ref[...].reshape(gw, n).T

    dt_c_all = _softplus(dtc_ref[...] + biasc_ref[...])
    dt_r_all = _softplus(dtr_ref[...] + biasr_ref[...])
    a_c = -jnp.exp(alogc_ref[...])
    a_r = -jnp.exp(alogr_ref[...])
    row = lax.broadcasted_iota(jnp.int32, (q, q), 0)
    col = lax.broadcasted_iota(jnp.int32, (q, q), 1)
    lower = row >= col
    upper = row <= col

    def spread(width, rep):
        head = lax.broadcasted_iota(jnp.int32, (hpg, width), 0)
        lane = lax.broadcasted_iota(jnp.int32, (hpg, width), 1)
        return lane // rep == head

    to_lanes = spread(gw, p)
    to_keys = spread(hpg * q, q)
    lane_head = lax.broadcasted_iota(jnp.int32, (1, gw), 1) // p
    d_skip = _dot_sel(dskip_ref[...], to_lanes)
    ht = ht_ref[...]
    for s in range(nsub):
        rows = slice(s * q, (s + 1) * q)
        dt_c = dt_c_all[rows, :]
        dt_r = dt_r_all[:, rows]
        acum_c = _dot_sel(dt_c * a_c, lower, sel_first=True)
        acum_r = _dot_sel(dt_r * a_r, upper)
        a_last = acum_c[q - 1:q, :]
        per_head = jnp.concatenate([jnp.exp(acum_c), jnp.exp(a_last - acum_c) * dt_c], axis=0)
        per_lane = _dot_sel(per_head, to_lanes)
        exp_a, w_end = per_lane[:q], per_lane[q:]
        chunk_decay = _dot_sel(jnp.exp(a_last), to_lanes)
        ac_keys = _dot_sel(acum_c, to_keys)
        x = xs_ref[rows, :]
        bm = b_ref[rows, :].astype(BF16)
        cm = c_ref[rows, :].astype(BF16)
        cb = lax.dot_general(cm, bm, NT_DIMS, preferred_element_type=F32)
        wmats = []
        for r in range(hpg):
            seg = ac_keys[:, r * q:(r + 1) * q] - acum_r[r:r + 1, :]
            decay = jnp.exp(jnp.where(lower, seg, -jnp.inf))
            wmats.append((cb * decay * dt_r[r:r + 1, :]).astype(BF16))
        y_all = jnp.dot(jnp.concatenate(wmats, axis=0), x.astype(BF16), preferred_element_type=F32)
        y = y_all[:q]
        for r in range(1, hpg):
            y = jnp.where(lane_head == r, y_all[r * q:(r + 1) * q], y)
        y = y + jnp.dot(cm, ht.astype(BF16), preferred_element_type=F32) * exp_a
        s_new = lax.dot_general(bm, (x * w_end).astype(BF16), TN_DIMS, preferred_element_type=F32)
        ht = ht * chunk_decay + s_new
        zt = z_ref[rows, :]
        yg = (y + x * d_skip) * (zt * _sigmoid(zt))
        y_ref[rows, :] = _rms(yg, nw_ref[...]).astype(BF16)
    ht_ref[...] = ht

    @pl.when(ti == pl.num_programs(2) - 1)
    def _():
        hl_ref[...] = ht.T.reshape(hpg, p, n)


def ssd_mix(proj, xbc_col0, z_col0, row0, nb, seq_len, conv_hist, conv_w, conv_b, dt_raw, dt_bias, a_log,
            d_skip, norm_w, h0, joint, *, groups, n_state, q, nsub):
    heads, p = h0.shape[1], h0.shape[2]
    k = conv_w.shape[0]
    hpg = heads // groups
    gw = hpg * p
    d_inner = heads * p
    xbc_w = d_inner + 2 * groups * n_state
    tl = q * nsub
    nt = seq_len // tl
    assert seq_len % tl == 0 and row0 % tl == 0 and z_col0 % gw == 0
    assert xbc_col0 % gw == 0 and (xbc_col0 + d_inner) % n_state == 0 and d_inner % n_state == 0
    dt_g = dt_raw.reshape(nb, seq_len, groups, hpg)
    dt_col = dt_g.transpose(0, 2, 1, 3)
    dt_row = dt_g.transpose(0, 2, 3, 1)
    blk0 = row0 // tl
    sub0, sub_per_seq, sub_per_tile = row0 // SUBLANES, seq_len // SUBLANES, tl // SUBLANES
    zc = z_col0 // gw
    hist8 = _pad_hist(conv_hist, SUBLANES)
    bias2 = conv_b.reshape(1, xbc_w)

    def conv_inputs(width, col_of_group):
        src = xbc_col0 // width
        specs = [pl.BlockSpec((tl, width), lambda b, g, t: (blk0 + b * nt + t, src + col_of_group(g))),
                 pl.BlockSpec((SUBLANES, width),
                              lambda b, g, t: (jnp.maximum(sub0 + b * sub_per_seq + t * sub_per_tile - 1, 0),
                                               src + col_of_group(g))),
                 pl.BlockSpec((None, SUBLANES, width), lambda b, g, t: (b, 0, col_of_group(g))),
                 pl.BlockSpec((k, width), lambda b, g, t: (0, col_of_group(g))),
                 pl.BlockSpec((1, width), lambda b, g, t: (0, col_of_group(g)))]
        return specs, [proj, proj, hist8, conv_w, bias2]

    bcol = d_inner // n_state
    conv_specs, conv_ops = [], []
    for width, col_of_group in ((gw, lambda g: g), (n_state, lambda g: bcol + g),
                                (n_state, lambda g: bcol + groups + g)):
        specs, ops = conv_inputs(width, col_of_group)
        conv_specs += specs
        conv_ops += ops

    def per_group(arr, shape):
        return arr.reshape((groups,) + shape), pl.BlockSpec((None,) + shape, lambda b, g, t: (g, 0, 0))

    biasc, biasc_spec = per_group(dt_bias, (1, hpg))
    biasr, biasr_spec = per_group(dt_bias, (hpg, 1))
    alogc, alogc_spec = per_group(a_log, (1, hpg))
    alogr, alogr_spec = per_group(a_log, (hpg, 1))
    dsk, dsk_spec = per_group(d_skip, (1, hpg))
    h_spec = pl.BlockSpec((None, hpg, p, n_state), lambda b, g, t: (b, g, 0, 0))
    in_specs, operands, aliases = _into(
        joint,
        conv_specs +
        [pl.BlockSpec((tl, gw), lambda b, g, t: (blk0 + b * nt + t, zc + g)),
         pl.BlockSpec((None, None, tl, hpg), lambda b, g, t: (b, g, t, 0)),
         pl.BlockSpec((None, None, hpg, tl), lambda b, g, t: (b, g, 0, t)),
         biasc_spec, biasr_spec, alogc_spec, alogr_spec, dsk_spec,
         pl.BlockSpec((1, gw), lambda b, g, t: (0, g)),
         h_spec],
        conv_ops + [proj, dt_col, dt_row, biasc, biasr, alogc, alogr, dsk, norm_w.reshape(1, d_inner), h0])
    body = functools.partial(_ssd_kernel, q=q, nsub=nsub, hpg=hpg, p=p, k=k)
    return pl.pallas_call(
        _drop_ref(body, len(operands) - 1) if aliases else body,
        grid=(nb, groups, nt),
        in_specs=in_specs,
        out_specs=[pl.BlockSpec((tl, gw), lambda b, g, t: (blk0 + b * nt + t, g)), h_spec],
        out_shape=[jax.ShapeDtypeStruct((proj.shape[0], d_inner), BF16),
                   jax.ShapeDtypeStruct(h0.shape, F32)],
        input_output_aliases=aliases,
        scratch_shapes=[pltpu.VMEM((n_state, gw), F32),
                        pltpu.VMEM((tl, gw), F32), pltpu.VMEM((tl, n_state), F32), pltpu.VMEM((tl, n_state), F32),
                        pltpu.VMEM((SUBLANES + tl, gw), F32), pltpu.VMEM((SUBLANES + tl, n_state), F32),
                        pltpu.VMEM((SUBLANES + tl, n_state), F32)],
        compiler_params=_params("parallel", "parallel", "arbitrary"),
    )(*operands)


def _lambda(lq1_ref, lk1_ref, lq2_ref, lk2_ref, lam_init):
    s1 = jnp.sum(lq1_ref[...] * lk1_ref[...], axis=-1, keepdims=True)
    s2 = jnp.sum(lq2_ref[...] * lk2_ref[...], axis=-1, keepdims=True)
    return jnp.exp(s1) - jnp.exp(s2) + lam_init


def _loop_by_two(n, body, init):
    pairs = n // 2
    carry = lax.fori_loop(0, pairs, lambda i, c: body(2 * i + 1, body(2 * i, c)), init)
    return lax.fori_loop(2 * pairs, n, body, carry)


def _fold_lanes(a, op):
    if a.shape[1] % LANES:
        return a
    parts = [a[:, i:i + LANES] for i in range(0, a.shape[1], LANES)]
    while len(parts) > 1:
        parts = [op(parts[i], parts[i + 1]) for i in range(0, len(parts) - 1, 2)] + parts[len(parts) & ~1:]
    return parts[0]


def _diff_prompt_kernel(q_ref, k_ref, v_ref, lq1_ref, lk1_ref, lq2_ref, lk2_ref, sub_ref, o_ref,
                        s_ref, acc_ref, *, t, tk, d, scale, lam_init):
    iq = pl.program_id(2)
    lam = _lambda(lq1_ref, lk1_ref, lq2_ref, lk2_ref, lam_init)
    n_full = (iq * t) // tk
    q_chunk = (iq * t + lax.broadcasted_iota(jnp.int32, (t, 1), 0)) // CHUNK
    k_chunk = (n_full * tk + lax.broadcasted_iota(jnp.int32, (1, tk), 1)) // CHUNK
    visible = k_chunk <= q_chunk
    width = LANES if tk % LANES == 0 else tk

    def tile(j):
        return pl.ds(pl.multiple_of(j * tk, tk), tk)

    outs = []
    for m in range(2):
        cols = slice(m * d, (m + 1) * d)
        qm = (q_ref[:, cols] * scale).astype(BF16)

        def scores(j, qm=qm, cols=cols):
            kt = k_ref[tile(j), cols].astype(BF16)
            return lax.dot_general(qm, kt, NT_DIMS, preferred_element_type=F32)

        def pass1(j, mx, scores=scores):
            s = scores(j)
            s_ref[:, tile(j)] = s
            return jnp.maximum(mx, _fold_lanes(s, jnp.maximum))

        mx = _loop_by_two(n_full, pass1, jnp.full((t, width), -jnp.inf, F32))
        s_last = jnp.where(visible, scores(n_full), -jnp.inf)
        s_ref[:, tile(n_full)] = s_last
        mx = jnp.max(jnp.maximum(mx, _fold_lanes(s_last, jnp.maximum)), axis=-1, keepdims=True)
        acc_ref[...] = jnp.zeros(acc_ref.shape, F32)

        def pass2(j, den, mx=mx):
            pr = jnp.exp(s_ref[:, tile(j)] - mx)
            vt = v_ref[tile(j), :].astype(BF16)
            acc_ref[...] += jnp.dot(pr.astype(BF16), vt, preferred_element_type=F32)
            return den + _fold_lanes(pr, jnp.add)

        den = _loop_by_two(n_full + 1, pass2, jnp.zeros((t, width), F32))
        outs.append(acc_ref[...] / jnp.sum(den, axis=-1, keepdims=True))
    o = outs[0] - lam * outs[1]
    o_ref[...] = (_rms(o, sub_ref[...]) * (1.0 - lam_init)).astype(BF16)


def _lam_specs(d):
    fixed = lambda *_: (0, 0)
    return [pl.BlockSpec((1, d), fixed)] * 4 + [pl.BlockSpec((1, 2 * d), fixed)]


def diff_attn_prompt(proj, q_col0, k_col0, v_col0, nb, seq_len, heads, d, lam_vecs, subln, lam_init, *,
                     tq_pref=256):
    hw = 2 * d
    tq = _tile(seq_len, tq_pref, CHUNK)
    nq = seq_len // tq
    tk = 2 * tq if nq % 2 == 0 else tq
    assert q_col0 % hw == 0 and k_col0 % hw == 0 and v_col0 % hw == 0
    qc, kc, vc = q_col0 // hw, k_col0 // hw, v_col0 // hw
    return pl.pallas_call(
        functools.partial(_diff_prompt_kernel, t=tq, tk=tk, d=d, scale=1.0 / math.sqrt(d), lam_init=lam_init),
        grid=(nb, heads, nq),
        in_specs=[pl.BlockSpec((tq, hw), lambda b, h, i: (b * nq + i, qc + h)),
                  pl.BlockSpec((seq_len, hw), lambda b, h, i: (b, kc + h)),
                  pl.BlockSpec((seq_len, hw), lambda b, h, i: (b, vc + h))] + _lam_specs(d),
        out_specs=pl.BlockSpec((tq, hw), lambda b, h, i: (b * nq + i, h)),
        out_shape=jax.ShapeDtypeStruct((proj.shape[0], heads * hw), BF16),
        scratch_shapes=[pltpu.VMEM((tq, seq_len), F32), pltpu.VMEM((tq, hw), F32)],
        compiler_params=_params("parallel", "parallel", "parallel"),
    )(proj, proj, proj, *[v.reshape(1, d) for v in lam_vecs], subln.reshape(1, hw))


def _diff_sample_kernel(q_ref, kn_ref, vn_ref, kp_ref, vp_ref, lq1_ref, lk1_ref, lq2_ref, lk2_ref, sub_ref,
                        o_ref, *, heads, d, past, scale, lam_init):
    lam = _lambda(lq1_ref, lk1_ref, lq2_ref, lk2_ref, lam_init)
    hw = 2 * d
    for h in range(heads):
        vp = vp_ref[:, h * hw:(h + 1) * hw].astype(BF16)
        vn = vn_ref[:, h * hw:(h + 1) * hw].astype(BF16)
        outs = []
        for m in range(2):
            cols = slice(h * hw + m * d, h * hw + (m + 1) * d)
            qm = (q_ref[:, cols] * scale).astype(BF16)
            kp = kp_ref[pl.ds(2 * h + m, past, stride=2 * heads), :].astype(BF16)
            s_p = lax.dot_general(qm, kp, NT_DIMS, preferred_element_type=F32)
            s_n = lax.dot_general(qm, kn_ref[:, cols].astype(BF16), NT_DIMS, preferred_element_type=F32)
            mx = jnp.maximum(jnp.max(s_p, axis=-1, keepdims=True), jnp.max(s_n, axis=-1, keepdims=True))
            p_p = jnp.exp(s_p - mx)
            p_n = jnp.exp(s_n - mx)
            den = jnp.sum(p_p, axis=-1, keepdims=True) + jnp.sum(p_n, axis=-1, keepdims=True)
            acc = jnp.dot(p_p.astype(BF16), vp, preferred_element_type=F32)
            acc += jnp.dot(p_n.astype(BF16), vn, preferred_element_type=F32)
            outs.append(acc / den)
        o = outs[0] - lam * outs[1]
        o_ref[:, h * hw:(h + 1) * hw] = (_rms(o, sub_ref[...]) * (1.0 - lam_init)).astype(BF16)


def diff_attn_sample(proj, q_col0, k_col0, v_col0, row0, nb, seq_len, heads, d, k_cache, v_cache, layer,
                     lam_vecs, subln, lam_init, joint):
    width = heads * 2 * d
    past = v_cache.shape[2]
    assert row0 % seq_len == 0 and q_col0 % width == 0 and k_col0 % width == 0 and v_col0 % width == 0
    rb = row0 // seq_len

    def new_spec(col0):
        return pl.BlockSpec((seq_len, width), lambda b: (rb + b, col0 // width))

    in_specs, operands, aliases = _into(
        joint,
        [new_spec(q_col0), new_spec(k_col0), new_spec(v_col0),
         pl.BlockSpec((None, None, past * heads * 2, d), lambda b: (layer, b, 0, 0)),
         pl.BlockSpec((None, None, past, width), lambda b: (layer, b, 0, 0))] + _lam_specs(d),
        [proj, proj, proj, k_cache, v_cache] + [v.reshape(1, d) for v in lam_vecs] + [subln.reshape(1, 2 * d)])
    body = functools.partial(_diff_sample_kernel, heads=heads, d=d, past=past, scale=1.0 / math.sqrt(d),
                             lam_init=lam_init)
    return pl.pallas_call(
        _drop_ref(body, 10),
        grid=(nb,),
        in_specs=in_specs,
        out_specs=pl.BlockSpec((seq_len, width), lambda b: (rb + b, 0)),
        out_shape=jax.ShapeDtypeStruct(joint.shape, BF16),
        input_output_aliases=aliases,
        compiler_params=_params("parallel"),
    )(*operands)


def _xattn_kernel(q_ref, k_ref, v_ref, o_ref, *, heads, hd, scale):
    for h in range(heads):
        cols = slice(h * hd, (h + 1) * hd)
        qh = (q_ref[:, cols] * scale).astype(BF16)
        s = lax.dot_general(qh, k_ref[:, cols].astype(BF16), NT_DIMS, preferred_element_type=F32)
        pr = jnp.exp(s - jnp.max(s, axis=-1, keepdims=True))
        den = jnp.sum(pr, axis=-1, keepdims=True)
        o = jnp.dot(pr.astype(BF16), v_ref[:, cols].astype(BF16), preferred_element_type=F32)
        o_ref[:, cols] = (o / den).astype(BF16)


def cross_attn(q_all, row0, nb, seq_len, k_arr, v_arr, kv_specs, heads, hd, joint, *, tq_pref=512):
    width = heads * hd
    tq = _tile(seq_len, tq_pref, PACKED_ROWS)
    assert row0 % tq == 0
    blk0, per_seq = row0 // tq, seq_len // tq
    in_specs, operands, aliases = _into(
        joint, [pl.BlockSpec((tq, width), lambda b, t: (blk0 + b * per_seq + t, 0))] + kv_specs,
        [q_all, k_arr, v_arr])
    body = functools.partial(_xattn_kernel, heads=heads, hd=hd, scale=1.0 / math.sqrt(hd))
    return pl.pallas_call(
        _drop_ref(body, 3) if aliases else body,
        grid=(nb, per_seq),
        in_specs=in_specs,
        out_specs=pl.BlockSpec((tq, width), lambda b, t: (blk0 + b * per_seq + t, 0)),
        out_shape=jax.ShapeDtypeStruct((q_all.shape[0], width), BF16),
        input_output_aliases=aliases,
        compiler_params=_params("parallel", "parallel"),
    )(*operands)


def kernel(x_prompt, x_sample, state_ssm, state_ssm_conv, state_pool, cache_diff_k, cache_diff_v, cache_mem_k, cache_mem_v, state_ffn_conv, mem_prompt, norm_mix_pre, norm_mix_post, w_in, ssm_conv_w, ssm_conv_b, ssm_dt_bias, ssm_a_log, ssm_d, ssm_norm, w_br_ssm, pool_w, pool_scale, w_br_pool, diff_lq1, diff_lk1, diff_lq2, diff_lk2, diff_subln, w_br_diff, w_o, norm_xa_pre, norm_xa_post, norm_mem, xa_wq, xa_wk, xa_wv, xa_wo, norm_ffn_pre, norm_ffn_post, ffn_w_up, ffn_conv_w, ffn_conv_b, ffn_w_down):
    bp, lp, dm = x_prompt.shape
    bs, ls, _ = x_sample.shape
    depth = w_in.shape[0]
    tp, ts = bp * lp, bs * ls
    heads, p_dim, n_state = state_ssm.shape[2:]
    d_inner = heads * p_dim
    xbc_w = state_ssm_conv.shape[3]
    groups = (xbc_w - d_inner) // (2 * n_state)
    k_ssm = ssm_conv_w.shape[1]
    pool_hist, pool_width = state_pool.shape[2:]
    past, dheads, _, dh = cache_diff_k.shape[2:]
    diff_w = dheads * 2 * dh
    n_mem, xheads, xhd = cache_mem_k.shape[2:]
    xa_w = xheads * xhd
    d_ff = ffn_w_down.shape[1]
    k_ffn = ffn_conv_w.shape[1]
    ffp = _round_up(d_ff, 1024)
    assert lp >= pool_hist and ls >= pool_hist and min(lp, ls) >= max(k_ssm, k_ffn) - 1

    o_z, o_xbc, o_dt = 0, d_inner, d_inner + xbc_w
    o_u = o_dt + heads
    o_q = o_u + pool_width
    o_g = o_q + 3 * diff_w

    def seg(off, width):
        return w_in[:, :, off:off + width]

    w_main = jnp.concatenate(
        [seg(o_g, 3 * dm), seg(o_xbc, xbc_w), seg(o_q, 3 * diff_w), seg(o_z, d_inner), seg(o_u, pool_width)],
        axis=-1).astype(BF16)
    c_g, c_xbc = 0, 3 * dm
    c_q = c_xbc + xbc_w
    c_k, c_v = c_q + diff_w, c_q + 2 * diff_w
    c_z = c_q + 3 * diff_w
    c_u = c_z + d_inner
    dt_pad = _round_up(heads, LANES)
    w_dt = jnp.pad(seg(o_dt, heads), ((0, 0), (0, 0), (0, dt_pad - heads))).astype(BF16)
    w_bs, w_bp, w_bd = w_br_ssm.astype(BF16), w_br_pool.astype(BF16), w_br_diff.astype(BF16)
    w_out = w_o.astype(BF16)
    pool_wb = pool_w.astype(BF16)
    w_q = xa_wq.astype(BF16)
    w_kv = jnp.concatenate([xa_wk, xa_wv], axis=-1).astype(BF16)
    w_xo = xa_wo.astype(BF16)

    def pad_ff(a):
        pad = [(0, 0)] * (a.ndim - 1) + [(0, ffp - d_ff)]
        return jnp.concatenate([jnp.pad(a[..., :d_ff], pad), jnp.pad(a[..., d_ff:], pad)], axis=-1)

    w_up = cast_pad_halves(ffn_w_up, d_ff, ffp)
    conv_w_ff = pad_ff(ffn_conv_w)
    conv_b_ff = pad_ff(ffn_conv_b)
    ffn_hist = pad_ff(state_ffn_conv)
    w_down = ffn_w_down.astype(BF16)
    if d_ff % LANES:
        w_down = jnp.pad(w_down, ((0, 0), (0, ffp - d_ff), (0, 0)))

    k_cache = cache_diff_k.reshape(depth, bs, past * dheads * 2, dh)
    v_cache = cache_diff_v.reshape(depth, bs, past, diff_w)
    mem_k = cache_mem_k.reshape(depth, bs, n_mem, xa_w)
    mem_v = cache_mem_v.reshape(depth, bs, n_mem, xa_w)
    mem_rows = mem_prompt.reshape(bp * n_mem, dm)

    h = jnp.concatenate([x_prompt.reshape(tp, dm), x_sample.reshape(ts, dm)], axis=0)
    xn = rmsnorm_bf16(h, norm_mix_pre[0])
    zeros_p = lambda *shape: jnp.zeros((bp,) + shape, F32)
    q_ssd = _tile(lp, 128, LANES)
    nsub_ssd = _tile(lp // q_ssd, 4, 1)

    def tails(arr, col0, width, n):
        a_p = jnp.stack([arr[(b + 1) * lp - n:(b + 1) * lp, col0:col0 + width] for b in range(bp)])
        a_s = arr[tp:, col0:col0 + width].reshape(bs, ls, width)[:, ls - n:]
        return a_p, a_s

    def both(arr, col0, width):
        return arr[:tp, col0:col0 + width], arr[tp:, col0:col0 + width]

    outs = [[] for _ in range(14)]
    for l in range(depth):
        lam_init = 0.8 - 0.6 * math.exp(-0.3 * l)
        lam_vecs = (diff_lq1[l], diff_lk1[l], diff_lq2[l], diff_lk2[l])

        proj = mm(xn, w_main, l)
        dt_raw = mm(xn, w_dt, l)[:, :heads]
        y_ssm = y_pool = None
        h_last = []
        for (row0, nb, sl, conv_hist, h0, p_hist, pos_base, q_chunk, nsub) in (
                (0, bp, lp, zeros_p(k_ssm - 1, xbc_w), zeros_p(heads, p_dim, n_state),
                 zeros_p(pool_hist, pool_width), 0, q_ssd, nsub_ssd),
                (tp, bs, ls, state_ssm_conv[l], state_ssm[l], state_pool[l], past, ls, 1)):
            dt_g = dt_raw[row0:row0 + nb * sl].reshape(nb, sl, heads)
            y_ssm, hl = ssd_mix(proj, c_xbc, c_z, row0, nb, sl, conv_hist, ssm_conv_w[l], ssm_conv_b[l], dt_g,
                                ssm_dt_bias[l], ssm_a_log[l], ssm_d[l], ssm_norm[l], h0, y_ssm,
                                groups=groups, n_state=n_state, q=q_chunk, nsub=nsub)
            h_last.append(hl)
            y_pool = pool_mix(proj, c_u, row0, nb, sl, p_hist, pos_base, pool_wb[l], pool_scale[l], y_pool)
        y_diff = diff_attn_prompt(proj, c_q, c_k, c_v, bp, lp, dheads, dh, lam_vecs, diff_subln[l], lam_init)
        y_diff = diff_attn_sample(proj, c_q, c_k, c_v, tp, bs, ls, dheads, dh, k_cache, v_cache, l,
                                  lam_vecs, diff_subln[l], lam_init, y_diff)
        merged = branch_merge(y_ssm, y_pool, y_diff, proj, c_g, w_bs, w_bp, w_bd, l)
        h, xn = post_norm(mm(merged, w_out, l), h, norm_mix_post[l], norm_xa_pre[l])

        mem_kv = mm(rmsnorm_bf16(mem_rows, norm_mem[l]), w_kv, l)
        q_xa = mm(xn, w_q, l)
        kv_p = [pl.BlockSpec((n_mem, xa_w), lambda b, t: (b, 0)), pl.BlockSpec((n_mem, xa_w), lambda b, t: (b, 1))]
        kv_s = [pl.BlockSpec((None, None, n_mem, xa_w), lambda b, t, l=l: (l, b, 0, 0))] * 2
        o_xa = cross_attn(q_xa, 0, bp, lp, mem_kv, mem_kv, kv_p, xheads, xhd, None)
        o_xa = cross_attn(q_xa, tp, bs, ls, mem_k, mem_v, kv_s, xheads, xhd, o_xa)
        h, xn = mm_post_norm(o_xa, w_xo, l, h, norm_xa_post[l], norm_ffn_pre[l])

        hid, tail_g, tail_v = ffn_up_geglu(xn, bp, lp, w_up, l, ffp, zeros_p(k_ffn - 1, 2 * ffp),
                                           conv_w_ff[l], conv_b_ff[l])
        up_s = mm(xn, w_up, l, row0=tp, tm_pref=512)
        hid = conv_geglu(up_s, ffp, bs, ls, ffn_hist[l], conv_w_ff[l], conv_b_ff[l], hid, tp)
        g_next = norm_mix_pre[l + 1] if l + 1 < depth else None
        res = post_norm(mm(hid, w_down, l, tm_pref=512, tn_pref=512), h, norm_ffn_post[l], g_next)
        h, xn = res if g_next is not None else (res, None)

        conv_p, conv_s = tails(proj, c_xbc, xbc_w, k_ssm - 1)
        pool_p, pool_s = tails(proj, c_u, pool_width, pool_hist)
        kk_p, kk_s = both(proj, c_k, diff_w)
        vv_p, vv_s = both(proj, c_v, diff_w)
        ups = up_s.reshape(bs, ls, 2 * ffp)[:, ls - (k_ffn - 1):]
        ffn_p = jnp.concatenate([tail_g[:, -(k_ffn - 1):, :d_ff], tail_v[:, -(k_ffn - 1):, :d_ff]], axis=-1)
        ffn_s = jnp.concatenate([ups[..., :d_ff], ups[..., ffp:ffp + d_ff]], axis=-1)
        layer_out = (h_last[0], h_last[1], conv_p, conv_s, pool_p, pool_s,
                     kk_p.reshape(bp, lp, dheads, 2, dh), kk_s.reshape(bs, ls, dheads, 2, dh),
                     vv_p.reshape(bp, lp, dheads, 2 * dh), vv_s.reshape(bs, ls, dheads, 2 * dh),
                     mem_kv[:, :xa_w].reshape(bp, n_mem, xheads, xhd),
                     mem_kv[:, xa_w:].reshape(bp, n_mem, xheads, xhd),
                     ffn_p, ffn_s)
        for acc, val in zip(outs, layer_out):
            acc.append(val)

    return (h[:tp].reshape(bp, lp, dm), h[tp:].reshape(bs, ls, dm)) + tuple(jnp.stack(o) for o in outs)
```

```python
import functools
import math

import jax
import jax.numpy as jnp
from jax import lax
from jax.experimental import pallas as pl
from jax.experimental.pallas import tpu as pltpu

F32 = jnp.float32
BF16 = jnp.bfloat16
EPS = 1e-6
CHUNK = 64
POOL_WINDOWS = (2, 4, 8, 16)
SUBLANES = 8
PACKED_ROWS = 16
LANES = 128
VMEM_LIMIT_BYTES = 56 * 1024 * 1024
NT_DIMS = (((1,), (1,)), ((), ()))
TN_DIMS = (((0,), (0,)), ((), ()))
SINGLE = pl.Buffered(1)
FFN_ROW_CHUNK = 64
NORM_UNROLL = 8


def _tile(n, pref, align):
    t = (min(pref, n) // align) * align
    while t >= align:
        if n % t == 0:
            return t
        t -= align
    return n


def _round_up(n, m):
    return (n + m - 1) // m * m


def _params(*sem):
    return pltpu.CompilerParams(dimension_semantics=sem, vmem_limit_bytes=VMEM_LIMIT_BYTES)


def _sigmoid(x):
    return 1.0 / (1.0 + jnp.exp(-x))


def _softplus(x):
    return jnp.maximum(x, 0.0) + jnp.log(1.0 + jnp.exp(-jnp.abs(x)))


def _gelu_tanh(x):
    return 0.5 * x * (1.0 + jnp.tanh(math.sqrt(2.0 / math.pi) * (x + 0.044715 * (x * x * x))))


def _rms(x, g):
    return x * lax.rsqrt(jnp.mean(x * x, axis=-1, keepdims=True) + EPS) * g


def _dot_sel(x, sel, sel_first=False):
    hi = x.astype(BF16)
    rest = x - hi.astype(F32)
    mid = rest.astype(BF16)
    lo = (rest - mid.astype(F32)).astype(BF16)
    sel = sel.astype(BF16)
    dots = [jnp.dot(sel, t, preferred_element_type=F32) if sel_first else
            jnp.dot(t, sel, preferred_element_type=F32) for t in (hi, mid, lo)]
    return dots[0] + dots[1] + dots[2]


def _drop_ref(body, idx):
    def wrapped(*refs):
        return body(*refs[:idx], *refs[idx + 1:])
    return wrapped


def _into(joint, in_specs, operands):
    if joint is None:
        return in_specs, operands, {}
    return (in_specs + [pl.BlockSpec(memory_space=pl.ANY)], operands + [joint], {len(operands): 0})


def _norm_kernel(x_ref, g_ref, o_ref, *, rows):
    def body(r, carry):
        sl = pl.ds(pl.multiple_of(r * rows, rows), rows)
        o_ref[sl, :] = _rms(x_ref[sl, :], g_ref[...]).astype(BF16)
        return carry
    lax.fori_loop(0, x_ref.shape[0] // rows, body, 0, unroll=NORM_UNROLL)


def rmsnorm_bf16(x, g, *, tm_pref=256):
    t, d = x.shape
    tm = _tile(t, tm_pref, PACKED_ROWS)
    return pl.pallas_call(
        functools.partial(_norm_kernel, rows=PACKED_ROWS),
        grid=(t // tm,),
        in_specs=[pl.BlockSpec((tm, d), lambda i: (i, 0)), pl.BlockSpec((1, d), lambda i: (0, 0))],
        out_specs=pl.BlockSpec((tm, d), lambda i: (i, 0)),
        out_shape=jax.ShapeDtypeStruct((t, d), BF16),
        compiler_params=_params("parallel"),
    )(x, g.reshape(1, d))


def _mm_kernel(x_ref, w_ref, o_ref):
    o_ref[...] = jnp.dot(x_ref[...], w_ref[...], preferred_element_type=F32)


def mm(x, w_all, layer, *, row0=0, nrows=None, tm_pref=768, tn_pref=1024):
    k, n = w_all.shape[1:]
    assert k == x.shape[1] or (k < x.shape[1] and k % LANES == 0)
    nrows = x.shape[0] - row0 if nrows is None else nrows
    tm = _tile(math.gcd(nrows, row0) if row0 else nrows, tm_pref, PACKED_ROWS)
    tn = _tile(n, tn_pref, LANES)
    rb = row0 // tm
    return pl.pallas_call(
        _mm_kernel,
        grid=(nrows // tm, n // tn),
        in_specs=[pl.BlockSpec((tm, k), lambda i, j: (rb + i, 0)),
                  pl.BlockSpec((None, k, tn), lambda i, j: (layer, 0, j))],
        out_specs=pl.BlockSpec((tm, tn), lambda i, j: (i, j)),
        out_shape=jax.ShapeDtypeStruct((nrows, n), F32),
        compiler_params=_params("parallel", "parallel"),
    )(x, w_all)


def _cast_pad_kernel(x_ref, o_ref, *, real_blocks, blocks):
    is_real = pl.program_id(1) % blocks < real_blocks
    o_ref[...] = jnp.where(is_real, x_ref[...], 0.0).astype(BF16)


def cast_pad_halves(w, half, half_pad, *, blk_pref=512):
    depth, k, _ = w.shape
    blk = _tile(math.gcd(half, half_pad), blk_pref, LANES)
    if blk % LANES:
        pad = ((0, 0), (0, 0), (0, half_pad - half))
        return jnp.concatenate([jnp.pad(w[..., :half], pad), jnp.pad(w[..., half:], pad)], -1).astype(BF16)
    real_blocks, blocks = half // blk, half_pad // blk

    def src(l, j):
        return (l, 0, (j // blocks) * real_blocks + jnp.minimum(j % blocks, real_blocks - 1))

    return pl.pallas_call(
        functools.partial(_cast_pad_kernel, real_blocks=real_blocks, blocks=blocks),
        grid=(depth, 2 * blocks),
        in_specs=[pl.BlockSpec((None, k, blk), src)],
        out_specs=pl.BlockSpec((None, k, blk), lambda l, j: (l, 0, j)),
        out_shape=jax.ShapeDtypeStruct((depth, k, 2 * half_pad), BF16),
        compiler_params=_params("parallel", "parallel"),
    )(w)


def _post_norm_kernel(y_ref, res_ref, g_ref, gn_ref, h_ref, *maybe_xn_ref, rows):
    def body(r, carry):
        sl = pl.ds(pl.multiple_of(r * rows, rows), rows)
        h_new = res_ref[sl, :] + _rms(y_ref[sl, :], g_ref[...])
        h_ref[sl, :] = h_new
        for xn_ref in maybe_xn_ref:
            xn_ref[sl, :] = _rms(h_new, gn_ref[...]).astype(BF16)
        return carry
    lax.fori_loop(0, y_ref.shape[0] // rows, body, 0, unroll=NORM_UNROLL)


def _mm_post_norm_kernel(x_ref, w_ref, res_ref, g_ref, gn_ref, h_ref, *rest, rows):
    *maybe_xn_ref, y_ref = rest
    y_ref[...] = jnp.dot(x_ref[...], w_ref[...], preferred_element_type=F32)
    _post_norm_kernel(y_ref, res_ref, g_ref, gn_ref, h_ref, *maybe_xn_ref, rows=rows)


def mm_post_norm(x, w_all, layer, res, g, g_next, *, tm_pref=256):
    t, k = x.shape
    d = w_all.shape[2]
    tm = _tile(t, tm_pref, PACKED_ROWS)
    row_spec = pl.BlockSpec((tm, d), lambda i: (i, 0))
    vec_spec = pl.BlockSpec((1, d), lambda i: (0, 0))
    emit = g_next is not None
    out = pl.pallas_call(
        functools.partial(_mm_post_norm_kernel, rows=PACKED_ROWS),
        grid=(t // tm,),
        in_specs=[pl.BlockSpec((tm, k), lambda i: (i, 0)),
                  pl.BlockSpec((None, k, d), lambda i: (layer, 0, 0), pipeline_mode=SINGLE),
                  row_spec, vec_spec, vec_spec],
        out_specs=[row_spec, row_spec] if emit else [row_spec],
        out_shape=[jax.ShapeDtypeStruct((t, d), F32)] + ([jax.ShapeDtypeStruct((t, d), BF16)] if emit else []),
        scratch_shapes=[pltpu.VMEM((tm, d), F32)],
        compiler_params=_params("parallel"),
    )(x, w_all, res, g.reshape(1, d), (g_next if emit else g).reshape(1, d))
    return tuple(out) if emit else out[0]


def post_norm(y, res, g, g_next, *, tm_pref=256):
    t, d = y.shape
    tm = _tile(t, tm_pref, PACKED_ROWS)
    row_spec = pl.BlockSpec((tm, d), lambda i: (i, 0))
    vec_spec = pl.BlockSpec((1, d), lambda i: (0, 0))
    emit = g_next is not None
    out = pl.pallas_call(
        functools.partial(_post_norm_kernel, rows=PACKED_ROWS),
        grid=(t // tm,),
        in_specs=[row_spec, row_spec, vec_spec, vec_spec],
        out_specs=[row_spec, row_spec] if emit else [row_spec],
        out_shape=[jax.ShapeDtypeStruct((t, d), F32)] + ([jax.ShapeDtypeStruct((t, d), BF16)] if emit else []),
        compiler_params=_params("parallel"),
    )(y, res, g.reshape(1, d), (g_next if emit else g).reshape(1, d))
    return tuple(out) if emit else out[0]


def _merge_kernel(ys_ref, yp_ref, yd_ref, g0_ref, g1_ref, g2_ref, ws_ref, wp_ref, wd_ref, o_ref):
    acc = _sigmoid(g0_ref[...]) * jnp.dot(ys_ref[...], ws_ref[...], preferred_element_type=F32)
    acc += _sigmoid(g1_ref[...]) * jnp.dot(yp_ref[...], wp_ref[...], preferred_element_type=F32)
    acc += _sigmoid(g2_ref[...]) * jnp.dot(yd_ref[...], wd_ref[...], preferred_element_type=F32)
    o_ref[...] = acc.astype(BF16)


def branch_merge(y_ssm, y_pool, y_diff, proj, gate_col0, w_s, w_p, w_d, layer, *, tm_pref=768, tn_pref=1024):
    t = y_ssm.shape[0]
    d = w_s.shape[2]
    tm = _tile(t, tm_pref, PACKED_ROWS)
    tn = _tile(d, tn_pref, LANES)
    assert gate_col0 % tn == 0
    nj = d // tn
    gj = gate_col0 // tn

    def gate_spec(br):
        return pl.BlockSpec((tm, tn), lambda i, j: (i, gj + br * nj + j))

    def x_spec(kdim):
        return pl.BlockSpec((tm, kdim), lambda i, j: (i, 0))

    def w_spec(kdim):
        return pl.BlockSpec((None, kdim, tn), lambda i, j: (layer, 0, j))

    return pl.pallas_call(
        _merge_kernel,
        grid=(t // tm, nj),
        in_specs=[x_spec(y_ssm.shape[1]), x_spec(y_pool.shape[1]), x_spec(y_diff.shape[1]),
                  gate_spec(0), gate_spec(1), gate_spec(2),
                  w_spec(w_s.shape[1]), w_spec(w_p.shape[1]), w_spec(w_d.shape[1])],
        out_specs=pl.BlockSpec((tm, tn), lambda i, j: (i, j)),
        out_shape=jax.ShapeDtypeStruct((t, d), BF16),
        compiler_params=_params("parallel", "parallel"),
    )(y_ssm, y_pool, y_diff, proj, proj, proj, w_s, w_p, w_d)


def _conv_rows(ext_ref, u, halo, w_ref, b_ref, k):
    n = u.shape[0]
    ext_ref[0:SUBLANES, :] = halo
    ext_ref[SUBLANES:SUBLANES + n, :] = u
    acc = u * w_ref[k - 1:k, :]
    for i in range(k - 1):
        acc = acc + ext_ref[pl.ds(SUBLANES - (k - 1) + i, n), :] * w_ref[i:i + 1, :]
    return acc + b_ref[...]


def _halo(prev_ref, hist_ref, s, nseq):
    if nseq > 1:
        return hist_ref[s]
    return jnp.where(pl.program_id(1) == 0, hist_ref[0], prev_ref[...])


def _seq_tiling(nb, seq_len, tl_pref, align):
    if seq_len >= tl_pref:
        return _tile(seq_len, tl_pref, align), 1
    nseq = _tile(nb, max(tl_pref // seq_len, 1), 1)
    return nseq * seq_len, nseq


def _seq_specs(row0, rows_per_b, tl, tc, col_blk0):
    blk0 = row0 // tl
    per_b = rows_per_b // tl
    sub0 = row0 // SUBLANES
    sub_per_b = rows_per_b // SUBLANES
    sub_per_tile = tl // SUBLANES
    cur = pl.BlockSpec((tl, tc), lambda b, t, c: (blk0 + b * per_b + t, col_blk0 + c))
    prev = pl.BlockSpec(
        (SUBLANES, tc),
        lambda b, t, c: (jnp.maximum(sub0 + b * sub_per_b + t * sub_per_tile - 1, 0), col_blk0 + c))
    return cur, prev


def _pad_hist(hist, rows):
    return jnp.pad(hist, ((0, 0), (rows - hist.shape[1], 0), (0, 0)))


def _conv_geglu_kernel(ug_ref, pg_ref, hg_ref, wg_ref, bg_ref, uv_ref, pv_ref, hv_ref, wv_ref, bv_ref,
                       o_ref, extg_ref, extv_ref, *, k, sl, nseq):
    for s in range(nseq):
        rows = slice(s * sl, (s + 1) * sl)
        gate = _conv_rows(extg_ref, ug_ref[rows, :], _halo(pg_ref, hg_ref, s, nseq), wg_ref, bg_ref, k)
        val = _conv_rows(extv_ref, uv_ref[rows, :], _halo(pv_ref, hv_ref, s, nseq), wv_ref, bv_ref, k)
        o_ref[rows, :] = (_gelu_tanh(gate) * val).astype(BF16)


def conv_geglu(up, half, nb, seq_len, hist, w, bias, joint, out_row0, *, tl_pref=512, tc_pref=1024):
    k = w.shape[0]
    tl, nseq = _seq_tiling(nb, seq_len, tl_pref, PACKED_ROWS)
    tc = _tile(half, tc_pref, LANES)
    rows_per_b = max(tl, seq_len)
    per_b = rows_per_b // tl
    ncb = half // tc
    assert out_row0 % tl == 0
    ob = out_row0 // tl
    cur_g, prev_g = _seq_specs(0, rows_per_b, tl, tc, 0)
    cur_v, prev_v = _seq_specs(0, rows_per_b, tl, tc, ncb)
    hist8 = _pad_hist(hist, SUBLANES)
    bias2 = bias.reshape(1, 2 * half)

    def side(off):
        return [pl.BlockSpec((nseq, SUBLANES, tc), lambda b, t, c: (b, 0, off + c)),
                pl.BlockSpec((k, tc), lambda b, t, c: (0, off + c)),
                pl.BlockSpec((1, tc), lambda b, t, c: (0, off + c))]

    in_specs, operands, aliases = _into(
        joint, [cur_g, prev_g] + side(0) + [cur_v, prev_v] + side(ncb),
        [up, up, hist8, w, bias2, up, up, hist8, w, bias2])
    body = functools.partial(_conv_geglu_kernel, k=k, sl=tl // nseq, nseq=nseq)
    return pl.pallas_call(
        _drop_ref(body, 10) if aliases else body,
        grid=(nb // nseq, per_b, ncb),
        in_specs=in_specs,
        out_specs=pl.BlockSpec((tl, tc), lambda b, t, c: (ob + b * per_b + t, c)),
        out_shape=jax.ShapeDtypeStruct(joint.shape, BF16),
        input_output_aliases=aliases,
        scratch_shapes=[pltpu.VMEM((SUBLANES + tl // nseq, tc), F32)] * 2,
        compiler_params=_params("parallel", "parallel", "parallel"),
    )(*operands)


def _ffn_up_kernel(xc_ref, xp_ref, wg_ref, wv_ref, hg_ref, hv_ref, cwg_ref, cbg_ref, cwv_ref, cbv_ref,
                   hid_ref, tg_ref, tv_ref, xe_ref, eg_ref, ev_ref, gg_ref, *, k, tm, tiles_per_seq):
    halo = PACKED_ROWS

    @pl.when(pl.program_id(1) == 0)
    def _():
        xe_ref[0:halo, :] = xp_ref[...]
        xe_ref[halo:halo + tm, :] = xc_ref[...]

    first = pl.program_id(0) % tiles_per_seq == 0

    def project(w_ref, h_ref, e_ref, t_ref):
        e_ref[...] = jnp.dot(xe_ref[...], w_ref[...], preferred_element_type=F32)
        e_ref[0:halo, :] = jnp.where(first, h_ref[...], e_ref[0:halo, :])
        t_ref[...] = e_ref[tm:tm + halo, :]

    def conv(e_ref, cw_ref, cb_ref, r0, n):
        acc = e_ref[halo + r0:halo + r0 + n, :] * cw_ref[k - 1:k, :]
        for i in range(k - 1):
            acc = acc + e_ref[pl.ds(halo + r0 - (k - 1) + i, n), :] * cw_ref[i:i + 1, :]
        return acc + cb_ref[...]

    chunk = _tile(tm, FFN_ROW_CHUNK, PACKED_ROWS)
    project(wg_ref, hg_ref, eg_ref, tg_ref)
    for r0 in range(0, tm, chunk):
        gg_ref[r0:r0 + chunk, :] = _gelu_tanh(conv(eg_ref, cwg_ref, cbg_ref, r0, chunk))
    project(wv_ref, hv_ref, ev_ref, tv_ref)
    for r0 in range(0, tm, chunk):
        val = conv(ev_ref, cwv_ref, cbv_ref, r0, chunk)
        hid_ref[r0:r0 + chunk, :] = (gg_ref[r0:r0 + chunk, :] * val).astype(BF16)


def ffn_up_geglu(xn, nb, seq_len, w_up_all, layer, half, hist, conv_w, conv_b, *, tm_pref=1024, tf_pref=512):
    t, d = xn.shape
    k = conv_w.shape[0]
    halo = PACKED_ROWS
    tm = _tile(seq_len, tm_pref, halo)
    tf = _tile(half, tf_pref, LANES)
    tiles_per_seq = seq_len // tm
    ncb = half // tf
    hist16 = _pad_hist(hist, halo)
    bias2 = conv_b.reshape(1, 2 * half)

    def w_spec(off):
        return pl.BlockSpec((None, d, tf), lambda i, j: (layer, 0, off + j))

    def hist_spec(off):
        return pl.BlockSpec((None, halo, tf), lambda i, j: (i // tiles_per_seq, 0, off + j))

    def cw_spec(off):
        return pl.BlockSpec((k, tf), lambda i, j: (0, off + j))

    def cb_spec(off):
        return pl.BlockSpec((1, tf), lambda i, j: (0, off + j))

    tail_spec = pl.BlockSpec((None, halo, tf), lambda i, j: (i, 0, j))
    tail_shape = jax.ShapeDtypeStruct((nb * tiles_per_seq, halo, half), F32)
    hid, tail_g, tail_v = pl.pallas_call(
        functools.partial(_ffn_up_kernel, k=k, tm=tm, tiles_per_seq=tiles_per_seq),
        grid=(nb * tiles_per_seq, ncb),
        in_specs=[pl.BlockSpec((tm, d), lambda i, j: (i, 0), pipeline_mode=SINGLE),
                  pl.BlockSpec((halo, d), lambda i, j: (jnp.maximum(i * (tm // halo) - 1, 0), 0)),
                  w_spec(0), w_spec(ncb), hist_spec(0), hist_spec(ncb),
                  cw_spec(0), cb_spec(0), cw_spec(ncb), cb_spec(ncb)],
        out_specs=[pl.BlockSpec((tm, tf), lambda i, j: (i, j)), tail_spec, tail_spec],
        out_shape=[jax.ShapeDtypeStruct((t, half), BF16), tail_shape, tail_shape],
        scratch_shapes=[pltpu.VMEM((halo + tm, d), BF16), pltpu.VMEM((halo + tm, tf), F32),
                        pltpu.VMEM((halo + tm, tf), F32), pltpu.VMEM((tm, tf), F32)],
        compiler_params=_params("parallel", "arbitrary"),
    )(xn, xn, w_up_all, w_up_all, hist16, hist16, conv_w, bias2, conv_w, bias2)
    last = slice(tiles_per_seq - 1, None, tiles_per_seq)
    return hid, tail_g[last], tail_v[last]


def _pool_kernel(u_ref, prev_ref, hist_ref, pw_ref, ps_ref, o_ref, ext_ref, *, tl, halo, pos_base, gd):
    t = pl.program_id(1)
    ext_ref[0:halo, :] = jnp.where(t == 0, hist_ref[...], prev_ref[...])
    ext_ref[halo:halo + tl, :] = u_ref[...]
    pos = pos_base + t * tl + lax.broadcasted_iota(jnp.int32, (tl, 1), 0)
    for gi, win in enumerate(POOL_WINDOWS):
        cols = slice(gi * gd, (gi + 1) * gd)
        cur = u_ref[:, cols]
        wsum = cur
        for i in range(1, win):
            wsum = wsum + ext_ref[pl.ds(halo - i, tl), cols]
        cnt = jnp.minimum(pos + 1, win).astype(F32)
        pooled = wsum / cnt - cur
        y = jnp.dot(pooled.astype(BF16), pw_ref[gi], preferred_element_type=F32) * ps_ref[:, cols]
        o_ref[:, cols] = y.astype(BF16)


def pool_mix(proj, col0, row0, nb, seq_len, hist, pos_base, pool_w, pool_scale, joint, *, tl_pref=512):
    ng, gd, _ = pool_w.shape
    width = ng * gd
    halo = 2 * SUBLANES
    assert ng == len(POOL_WINDOWS) and max(POOL_WINDOWS) <= halo
    tl = _tile(seq_len, tl_pref, halo)
    assert row0 % tl == 0 and col0 % width == 0
    blk0, per_seq = row0 // tl, seq_len // tl
    h0, h_per_seq, h_per_tile = row0 // halo, seq_len // halo, tl // halo
    cb = col0 // width
    in_specs, operands, aliases = _into(
        joint,
        [pl.BlockSpec((tl, width), lambda b, t: (blk0 + b * per_seq + t, cb)),
         pl.BlockSpec((halo, width),
                      lambda b, t: (jnp.maximum(h0 + b * h_per_seq + t * h_per_tile - 1, 0), cb)),
         pl.BlockSpec((None, halo, width), lambda b, t: (b, 0, 0)),
         pl.BlockSpec((ng, gd, gd), lambda b, t: (0, 0, 0)),
         pl.BlockSpec((1, width), lambda b, t: (0, 0))],
        [proj, proj, _pad_hist(hist, halo), pool_w, pool_scale.reshape(1, width)])
    body = functools.partial(_pool_kernel, tl=tl, halo=halo, pos_base=pos_base, gd=gd)
    return pl.pallas_call(
        _drop_ref(body, 5) if aliases else body,
        grid=(nb, per_seq),
        in_specs=in_specs,
        out_specs=pl.BlockSpec((tl, width), lambda b, t: (blk0 + b * per_seq + t, 0)),
        out_shape=jax.ShapeDtypeStruct((proj.shape[0], width), BF16),
        input_output_aliases=aliases,
        scratch_shapes=[pltpu.VMEM((halo + tl, width), F32)],
        compiler_params=_params("parallel", "parallel"),
    )(*operands)


def _ssd_kernel(*refs, q, nsub, hpg, p, k):
    conv_in, refs = refs[:15], refs[15:]
    (z_ref, dtc_ref, dtr_ref, biasc_ref, biasr_ref, alogc_ref, alogr_ref, dskip_ref, nw_ref, h0_ref,
     y_ref, hl_ref, ht_ref, xs_ref, b_ref, c_ref, ext_x_ref, ext_b_ref, ext_c_ref) = refs
    ti = pl.program_id(2)
    gw = hpg * p
    n = b_ref.shape[1]
    for i, (dst_ref, ext_ref) in enumerate(((xs_ref, ext_x_ref), (b_ref, ext_b_ref), (c_ref, ext_c_ref))):
        cur_ref, prev_ref, hist_ref, cw_ref, cb_ref = conv_in[5 * i:5 * i + 5]
        halo = jnp.where(ti == 0, hist_ref[...], prev_ref[...])
        u = _conv_rows(ext_ref, cur_ref[...], halo, cw_ref, cb_ref, k)
        dst_ref[...] = u * _sigmoid(u)

    @pl.when(ti == 0)
    def _():
        ht_ref[...] = h0_ref[...].reshape(gw, n).T

    dt_c_all = _softplus(dtc_ref[...] + biasc_ref[...])
    dt_r_all = _softplus(dtr_ref[...] + biasr_ref[...])
    a_c = -jnp.exp(alogc_ref[...])
    a_r = -jnp.exp(alogr_ref[...])
    row = lax.broadcasted_iota(jnp.int32, (q, q), 0)
    col = lax.broadcasted_iota(jnp.int32, (q, q), 1)
    lower = row >= col
    upper = row <= col

    def spread(width, rep):
        head = lax.broadcasted_iota(jnp.int32, (hpg, width), 0)
        lane = lax.broadcasted_iota(jnp.int32, (hpg, width), 1)
        return lane // rep == head

    to_lanes = spread(gw, p)
    to_keys = spread(hpg * q, q)
    lane_head = lax.broadcasted_iota(jnp.int32, (1, gw), 1) // p
    d_skip = _dot_sel(dskip_ref[...], to_lanes)
    ht = ht_ref[...]
    for s in range(nsub):
        rows = slice(s * q, (s + 1) * q)
        dt_c = dt_c_all[rows, :]
        dt_r = dt_r_all[:, rows]
        acum_c = _dot_sel(dt_c * a_c, lower, sel_first=True)
        acum_r = _dot_sel(dt_r * a_r, upper)
        a_last = acum_c[q - 1:q, :]
        per_head = jnp.concatenate([jnp.exp(acum_c), jnp.exp(a_last - acum_c) * dt_c], axis=0)
        per_lane = _dot_sel(per_head, to_lanes)
        exp_a, w_end = per_lane[:q], per_lane[q:]
        chunk_decay = _dot_sel(jnp.exp(a_last), to_lanes)
        ac_keys = _dot_sel(acum_c, to_keys)
        x = xs_ref[rows, :]
        bm = b_ref[rows, :].astype(BF16)
        cm = c_ref[rows, :].astype(BF16)
        cb = lax.dot_general(cm, bm, NT_DIMS, preferred_element_type=F32)
        wmats = []
        for r in range(hpg):
            seg = ac_keys[:, r * q:(r + 1) * q] - acum_r[r:r + 1, :]
            decay = jnp.exp(jnp.where(lower, seg, -jnp.inf))
            wmats.append((cb * decay * dt_r[r:r + 1, :]).astype(BF16))
        y_all = jnp.dot(jnp.concatenate(wmats, axis=0), x.astype(BF16), preferred_element_type=F32)
        y = y_all[:q]
        for r in range(1, hpg):
            y = jnp.where(lane_head == r, y_all[r * q:(r + 1) * q], y)
        y = y + jnp.dot(cm, ht.astype(BF16), preferred_element_type=F32) * exp_a
        s_new = lax.dot_general(bm, (x * w_end).astype(BF16), TN_DIMS, preferred_element_type=F32)
        ht = ht * chunk_decay + s_new
        zt = z_ref[rows, :]
        yg = (y + x * d_skip) * (zt * _sigmoid(zt))
        y_ref[rows, :] = _rms(yg, nw_ref[...]).astype(BF16)
    ht_ref[...] = ht

    @pl.when(ti == pl.num_programs(2) - 1)
    def _():
        hl_ref[...] = ht.T.reshape(hpg, p, n)


def ssd_mix(proj, xbc_col0, z_col0, row0, nb, seq_len, conv_hist, conv_w, conv_b, dt_raw, dt_bias, a_log,
            d_skip, norm_w, h0, joint, *, groups, n_state, q, nsub):
    heads, p = h0.shape[1], h0.shape[2]
    k = conv_w.shape[0]
    hpg = heads // groups
    gw = hpg * p
    d_inner = heads * p
    xbc_w = d_inner + 2 * groups * n_state
    tl = q * nsub
    nt = seq_len // tl
    assert seq_len % tl == 0 and row0 % tl == 0 and z_col0 % gw == 0
    assert xbc_col0 % gw == 0 and (xbc_col0 + d_inner) % n_state == 0 and d_inner % n_state == 0
    dt_g = dt_raw.reshape(nb, seq_len, groups, hpg)
    dt_col = dt_g.transpose(0, 2, 1, 3)
    dt_row = dt_g.transpose(0, 2, 3, 1)
    blk0 = row0 // tl
    sub0, sub_per_seq, sub_per_tile = row0 // SUBLANES, seq_len // SUBLANES, tl // SUBLANES
    zc = z_col0 // gw
    hist8 = _pad_hist(conv_hist, SUBLANES)
    bias2 = conv_b.reshape(1, xbc_w)

    def conv_inputs(width, col_of_group):
        src = xbc_col0 // width
        specs = [pl.BlockSpec((tl, width), lambda b, g, t: (blk0 + b * nt + t, src + col_of_group(g))),
                 pl.BlockSpec((SUBLANES, width),
                              lambda b, g, t: (jnp.maximum(sub0 + b * sub_per_seq + t * sub_per_tile - 1, 0),
                                               src + col_of_group(g))),
                 pl.BlockSpec((None, SUBLANES, width), lambda b, g, t: (b, 0, col_of_group(g))),
                 pl.BlockSpec((k, width), lambda b, g, t: (0, col_of_group(g))),
                 pl.BlockSpec((1, width), lambda b, g, t: (0, col_of_group(g)))]
        return specs, [proj, proj, hist8, conv_w, bias2]

    bcol = d_inner // n_state
    conv_specs, conv_ops = [], []
    for width, col_of_group in ((gw, lambda g: g), (n_state, lambda g: bcol + g),
                                (n_state, lambda g: bcol + groups + g)):
        specs, ops = conv_inputs(width, col_of_group)
        conv_specs += specs
        conv_ops += ops

    def per_group(arr, shape):
        return arr.reshape((groups,) + shape), pl.BlockSpec((None,) + shape, lambda b, g, t: (g, 0, 0))

    biasc, biasc_spec = per_group(dt_bias, (1, hpg))
    biasr, biasr_spec = per_group(dt_bias, (hpg, 1))
    alogc, alogc_spec = per_group(a_log, (1, hpg))
    alogr, alogr_spec = per_group(a_log, (hpg, 1))
    dsk, dsk_spec = per_group(d_skip, (1, hpg))
    h_spec = pl.BlockSpec((None, hpg, p, n_state), lambda b, g, t: (b, g, 0, 0))
    in_specs, operands, aliases = _into(
        joint,
        conv_specs +
        [pl.BlockSpec((tl, gw), lambda b, g, t: (blk0 + b * nt + t, zc + g)),
         pl.BlockSpec((None, None, tl, hpg), lambda b, g, t: (b, g, t, 0)),
         pl.BlockSpec((None, None, hpg, tl), lambda b, g, t: (b, g, 0, t)),
         biasc_spec, biasr_spec, alogc_spec, alogr_spec, dsk_spec,
         pl.BlockSpec((1, gw), lambda b, g, t: (0, g)),
         h_spec],
        conv_ops + [proj, dt_col, dt_row, biasc, biasr, alogc, alogr, dsk, norm_w.reshape(1, d_inner), h0])
    body = functools.partial(_ssd_kernel, q=q, nsub=nsub, hpg=hpg, p=p, k=k)
    return pl.pallas_call(
        _drop_ref(body, len(operands) - 1) if aliases else body,
        grid=(nb, groups, nt),
        in_specs=in_specs,
        out_specs=[pl.BlockSpec((tl, gw), lambda b, g, t: (blk0 + b * nt + t, g)), h_spec],
        out_shape=[jax.ShapeDtypeStruct((proj.shape[0], d_inner), BF16),
                   jax.ShapeDtypeStruct(h0.shape, F32)],
        input_output_aliases=aliases,
        scratch_shapes=[pltpu.VMEM((n_state, gw), F32),
                        pltpu.VMEM((tl, gw), F32), pltpu.VMEM((tl, n_state), F32), pltpu.VMEM((tl, n_state), F32),
                        pltpu.VMEM((SUBLANES + tl, gw), F32), pltpu.VMEM((SUBLANES + tl, n_state), F32),
                        pltpu.VMEM((SUBLANES + tl, n_state), F32)],
        compiler_params=_params("parallel", "parallel", "arbitrary"),
    )(*operands)


def _lambda(lq1_ref, lk1_ref, lq2_ref, lk2_ref, lam_init):
    s1 = jnp.sum(lq1_ref[...] * lk1_ref[...], axis=-1, keepdims=True)
    s2 = jnp.sum(lq2_ref[...] * lk2_ref[...], axis=-1, keepdims=True)
    return jnp.exp(s1) - jnp.exp(s2) + lam_init


def _loop_by_two(n, body, init):
    pairs = n // 2
    carry = lax.fori_loop(0, pairs, lambda i, c: body(2 * i + 1, body(2 * i, c)), init)
    return lax.fori_loop(2 * pairs, n, body, carry)


def _fold_lanes(a, op):
    if a.shape[1] % LANES:
        return a
    parts = [a[:, i:i + LANES] for i in range(0, a.shape[1], LANES)]
    while len(parts) > 1:
        parts = [op(parts[i], parts[i + 1]) for i in range(0, len(parts) - 1, 2)] + parts[len(parts) & ~1:]
    return parts[0]


def _diff_prompt_kernel(q_ref, k_ref, v_ref, lq1_ref, lk1_ref, lq2_ref, lk2_ref, sub_ref, o_ref,
                        s_ref, acc_ref, *, t, tk, d, scale, lam_init):
    iq = pl.program_id(2)
    lam = _lambda(lq1_ref, lk1_ref, lq2_ref, lk2_ref, lam_init)
    n_full = (iq * t) // tk
    q_chunk = (iq * t + lax.broadcasted_iota(jnp.int32, (t, 1), 0)) // CHUNK
    k_chunk = (n_full * tk + lax.broadcasted_iota(jnp.int32, (1, tk), 1)) // CHUNK
    visible = k_chunk <= q_chunk
    width = LANES if tk % LANES == 0 else tk

    def tile(j):
        return pl.ds(pl.multiple_of(j * tk, tk), tk)

    outs = []
    for m in range(2):
        cols = slice(m * d, (m + 1) * d)
        qm = (q_ref[:, cols] * scale).astype(BF16)

        def scores(j, qm=qm, cols=cols):
            kt = k_ref[tile(j), cols].astype(BF16)
            return lax.dot_general(qm, kt, NT_DIMS, preferred_element_type=F32)

        def pass1(j, mx, scores=scores):
            s = scores(j)
            s_ref[:, tile(j)] = s
            return jnp.maximum(mx, _fold_lanes(s, jnp.maximum))

        mx = _loop_by_two(n_full, pass1, jnp.full((t, width), -jnp.inf, F32))
        s_last = jnp.where(visible, scores(n_full), -jnp.inf)
        s_ref[:, tile(n_full)] = s_last
        mx = jnp.max(jnp.maximum(mx, _fold_lanes(s_last, jnp.maximum)), axis=-1, keepdims=True)
        acc_ref[...] = jnp.zeros(acc_ref.shape, F32)

        def pass2(j, den, mx=mx):
            pr = jnp.exp(s_ref[:, tile(j)] - mx)
            vt = v_ref[tile(j), :].astype(BF16)
            acc_ref[...] += jnp.dot(pr.astype(BF16), vt, preferred_element_type=F32)
            return den + _fold_lanes(pr, jnp.add)

        den = _loop_by_two(n_full + 1, pass2, jnp.zeros((t, width), F32))
        outs.append(acc_ref[...] / jnp.sum(den, axis=-1, keepdims=True))
    o = outs[0] - lam * outs[1]
    o_ref[...] = (_rms(o, sub_ref[...]) * (1.0 - lam_init)).astype(BF16)


def _lam_specs(d):
    fixed = lambda *_: (0, 0)
    return [pl.BlockSpec((1, d), fixed)] * 4 + [pl.BlockSpec((1, 2 * d), fixed)]


def diff_attn_prompt(proj, q_col0, k_col0, v_col0, nb, seq_len, heads, d, lam_vecs, subln, lam_init, *,
                     tq_pref=256):
    hw = 2 * d
    tq = _tile(seq_len, tq_pref, CHUNK)
    nq = seq_len // tq
    tk = 2 * tq if nq % 2 == 0 else tq
    assert q_col0 % hw == 0 and k_col0 % hw == 0 and v_col0 % hw == 0
    qc, kc, vc = q_col0 // hw, k_col0 // hw, v_col0 // hw
    return pl.pallas_call(
        functools.partial(_diff_prompt_kernel, t=tq, tk=tk, d=d, scale=1.0 / math.sqrt(d), lam_init=lam_init),
        grid=(nb, heads, nq),
        in_specs=[pl.BlockSpec((tq, hw), lambda b, h, i: (b * nq + i, qc + h)),
                  pl.BlockSpec((seq_len, hw), lambda b, h, i: (b, kc + h)),
                  pl.BlockSpec((seq_len, hw), lambda b, h, i: (b, vc + h))] + _lam_specs(d),
        out_specs=pl.BlockSpec((tq, hw), lambda b, h, i: (b * nq + i, h)),
        out_shape=jax.ShapeDtypeStruct((proj.shape[0], heads * hw), BF16),
        scratch_shapes=[pltpu.VMEM((tq, seq_len), F32), pltpu.VMEM((tq, hw), F32)],
        compiler_params=_params("parallel", "parallel", "parallel"),
    )(proj, proj, proj, *[v.reshape(1, d) for v in lam_vecs], subln.reshape(1, hw))


def _diff_sample_kernel(q_ref, kn_ref, vn_ref, kp_ref, vp_ref, lq1_ref, lk1_ref, lq2_ref, lk2_ref, sub_ref,
                        o_ref, *, heads, d, past, scale, lam_init):
    lam = _lambda(lq1_ref, lk1_ref, lq2_ref, lk2_ref, lam_init)
    hw = 2 * d
    for h in range(heads):
        vp = vp_ref[:, h * hw:(h + 1) * hw].astype(BF16)
        vn = vn_ref[:, h * hw:(h + 1) * hw].astype(BF16)
        outs = []
        for m in range(2):
            cols = slice(h * hw + m * d, h * hw + (m + 1) * d)
            qm = (q_ref[:, cols] * scale).astype(BF16)
            kp = kp_ref[pl.ds(2 * h + m, past, stride=2 * heads), :].astype(BF16)
            s_p = lax.dot_general(qm, kp, NT_DIMS, preferred_element_type=F32)
            s_n = lax.dot_general(qm, kn_ref[:, cols].astype(BF16), NT_DIMS, preferred_element_type=F32)
            mx = jnp.maximum(jnp.max(s_p, axis=-1, keepdims=True), jnp.max(s_n, axis=-1, keepdims=True))
            p_p = jnp.exp(s_p - mx)
            p_n = jnp.exp(s_n - mx)
            den = jnp.sum(p_p, axis=-1, keepdims=True) + jnp.sum(p_n, axis=-1, keepdims=True)
            acc = jnp.dot(p_p.astype(BF16), vp, preferred_element_type=F32)
            acc += jnp.dot(p_n.astype(BF16), vn, preferred_element_type=F32)
            outs.append(acc / den)
        o = outs[0] - lam * outs[1]
        o_ref[:, h * hw:(h + 1) * hw] = (_rms(o, sub_ref[...]) * (1.0 - lam_init)).astype(BF16)


def diff_attn_sample(proj, q_col0, k_col0, v_col0, row0, nb, seq_len, heads, d, k_cache, v_cache, layer,
                     lam_vecs, subln, lam_init, joint):
    width = heads * 2 * d
    past = v_cache.shape[2]
    assert row0 % seq_len == 0 and q_col0 % width == 0 and k_col0 % width == 0 and v_col0 % width == 0
    rb = row0 // seq_len

    def new_spec(col0):
        return pl.BlockSpec((seq_len, width), lambda b: (rb + b, col0 // width))

    in_specs, operands, aliases = _into(
        joint,
        [new_spec(q_col0), new_spec(k_col0), new_spec(v_col0),
         pl.BlockSpec((None, None, past * heads * 2, d), lambda b: (layer, b, 0, 0)),
         pl.BlockSpec((None, None, past, width), lambda b: (layer, b, 0, 0))] + _lam_specs(d),
        [proj, proj, proj, k_cache, v_cache] + [v.reshape(1, d) for v in lam_vecs] + [subln.reshape(1, 2 * d)])
    body = functools.partial(_diff_sample_kernel, heads=heads, d=d, past=past, scale=1.0 / math.sqrt(d),
                             lam_init=lam_init)
    return pl.pallas_call(
        _drop_ref(body, 10),
        grid=(nb,),
        in_specs=in_specs,
        out_specs=pl.BlockSpec((seq_len, width), lambda b: (rb + b, 0)),
        out_shape=jax.ShapeDtypeStruct(joint.shape, BF16),
        input_output_aliases=aliases,
        compiler_params=_params("parallel"),
    )(*operands)


def _xattn_kernel(q_ref, k_ref, v_ref, o_ref, *, heads, hd, scale):
    for h in range(heads):
        cols = slice(h * hd, (h + 1) * hd)
        qh = (q_ref[:, cols] * scale).astype(BF16)
        s = lax.dot_general(qh, k_ref[:, cols].astype(BF16), NT_DIMS, preferred_element_type=F32)
        pr = jnp.exp(s - jnp.max(s, axis=-1, keepdims=True))
        den = jnp.sum(pr, axis=-1, keepdims=True)
        o = jnp.dot(pr.astype(BF16), v_ref[:, cols].astype(BF16), preferred_element_type=F32)
        o_ref[:, cols] = (o / den).astype(BF16)


def cross_attn(q_all, row0, nb, seq_len, k_arr, v_arr, kv_specs, heads, hd, joint, *, tq_pref=512):
    width = heads * hd
    tq = _tile(seq_len, tq_pref, PACKED_ROWS)
    assert row0 % tq == 0
    blk0, per_seq = row0 // tq, seq_len // tq
    in_specs, operands, aliases = _into(
        joint, [pl.BlockSpec((tq, width), lambda b, t: (blk0 + b * per_seq + t, 0))] + kv_specs,
        [q_all, k_arr, v_arr])
    body = functools.partial(_xattn_kernel, heads=heads, hd=hd, scale=1.0 / math.sqrt(hd))
    return pl.pallas_call(
        _drop_ref(body, 3) if aliases else body,
        grid=(nb, per_seq),
        in_specs=in_specs,
        out_specs=pl.BlockSpec((tq, width), lambda b, t: (blk0 + b * per_seq + t, 0)),
        out_shape=jax.ShapeDtypeStruct((q_all.shape[0], width), BF16),
        input_output_aliases=aliases,
        compiler_params=_params("parallel", "parallel"),
    )(*operands)


def kernel(x_prompt, x_sample, state_ssm, state_ssm_conv, state_pool, cache_diff_k, cache_diff_v, cache_mem_k, cache_mem_v, state_ffn_conv, mem_prompt, norm_mix_pre, norm_mix_post, w_in, ssm_conv_w, ssm_conv_b, ssm_dt_bias, ssm_a_log, ssm_d, ssm_norm, w_br_ssm, pool_w, pool_scale, w_br_pool, diff_lq1, diff_lk1, diff_lq2, diff_lk2, diff_subln, w_br_diff, w_o, norm_xa_pre, norm_xa_post, norm_mem, xa_wq, xa_wk, xa_wv, xa_wo, norm_ffn_pre, norm_ffn_post, ffn_w_up, ffn_conv_w, ffn_conv_b, ffn_w_down):
    bp, lp, dm = x_prompt.shape
    bs, ls, _ = x_sample.shape
    depth = w_in.shape[0]
    tp, ts = bp * lp, bs * ls
    heads, p_dim, n_state = state_ssm.shape[2:]
    d_inner = heads * p_dim
    xbc_w = state_ssm_conv.shape[3]
    groups = (xbc_w - d_inner) // (2 * n_state)
    k_ssm = ssm_conv_w.shape[1]
    pool_hist, pool_width = state_pool.shape[2:]
    past, dheads, _, dh = cache_diff_k.shape[2:]
    diff_w = dheads * 2 * dh
    n_mem, xheads, xhd = cache_mem_k.shape[2:]
    xa_w = xheads * xhd
    d_ff = ffn_w_down.shape[1]
    k_ffn = ffn_conv_w.shape[1]
    ffp = _round_up(d_ff, 1024)
    assert lp >= pool_hist and ls >= pool_hist and min(lp, ls) >= max(k_ssm, k_ffn) - 1

    o_z, o_xbc, o_dt = 0, d_inner, d_inner + xbc_w
    o_u = o_dt + heads
    o_q = o_u + pool_width
    o_g = o_q + 3 * diff_w

    def seg(off, width):
        return w_in[:, :, off:off + width]

    w_main = jnp.concatenate(
        [seg(o_g, 3 * dm), seg(o_xbc, xbc_w), seg(o_q, 3 * diff_w), seg(o_z, d_inner), seg(o_u, pool_width)],
        axis=-1).astype(BF16)
    c_g, c_xbc = 0, 3 * dm
    c_q = c_xbc + xbc_w
    c_k, c_v = c_q + diff_w, c_q + 2 * diff_w
    c_z = c_q + 3 * diff_w
    c_u = c_z + d_inner
    dt_pad = _round_up(heads, LANES)
    w_dt = jnp.pad(seg(o_dt, heads), ((0, 0), (0, 0), (0, dt_pad - heads))).astype(BF16)
    w_bs, w_bp, w_bd = w_br_ssm.astype(BF16), w_br_pool.astype(BF16), w_br_diff.astype(BF16)
    w_out = w_o.astype(BF16)
    pool_wb = pool_w.astype(BF16)
    w_q = xa_wq.astype(BF16)
    w_kv = jnp.concatenate([xa_wk, xa_wv], axis=-1).astype(BF16)
    w_xo = xa_wo.astype(BF16)

    def pad_ff(a):
        pad = [(0, 0)] * (a.ndim - 1) + [(0, ffp - d_ff)]
        return jnp.concatenate([jnp.pad(a[..., :d_ff], pad), jnp.pad(a[..., d_ff:], pad)], axis=-1)

    w_up = cast_pad_halves(ffn_w_up, d_ff, ffp)
    conv_w_ff = pad_ff(ffn_conv_w)
    conv_b_ff = pad_ff(ffn_conv_b)
    ffn_hist = pad_ff(state_ffn_conv)
    w_down = ffn_w_down.astype(BF16)
    if d_ff % LANES:
        w_down = jnp.pad(w_down, ((0, 0), (0, ffp - d_ff), (0, 0)))

    k_cache = cache_diff_k.reshape(depth, bs, past * dheads * 2, dh)
    v_cache = cache_diff_v.reshape(depth, bs, past, diff_w)
    mem_k = cache_mem_k.reshape(depth, bs, n_mem, xa_w)
    mem_v = cache_mem_v.reshape(depth, bs, n_mem, xa_w)
    mem_rows = mem_prompt.reshape(bp * n_mem, dm)

    h = jnp.concatenate([x_prompt.reshape(tp, dm), x_sample.reshape(ts, dm)], axis=0)
    xn = rmsnorm_bf16(h, norm_mix_pre[0])
    zeros_p = lambda *shape: jnp.zeros((bp,) + shape, F32)
    q_ssd = _tile(lp, 128, LANES)
    nsub_ssd = _tile(lp // q_ssd, 8, 1)

    def tails(arr, col0, width, n):
        a_p = jnp.stack([arr[(b + 1) * lp - n:(b + 1) * lp, col0:col0 + width] for b in range(bp)])
        a_s = arr[tp:, col0:col0 + width].reshape(bs, ls, width)[:, ls - n:]
        return a_p, a_s

    def both(arr, col0, width):
        return arr[:tp, col0:col0 + width], arr[tp:, col0:col0 + width]

    outs = [[] for _ in range(14)]
    for l in range(depth):
        lam_init = 0.8 - 0.6 * math.exp(-0.3 * l)
        lam_vecs = (diff_lq1[l], diff_lk1[l], diff_lq2[l], diff_lk2[l])

        proj = mm(xn, w_main, l)
        dt_raw = mm(xn, w_dt, l)[:, :heads]
        y_ssm = y_pool = None
        h_last = []
        for (row0, nb, sl, conv_hist, h0, p_hist, pos_base, q_chunk, nsub) in (
                (0, bp, lp, zeros_p(k_ssm - 1, xbc_w), zeros_p(heads, p_dim, n_state),
                 zeros_p(pool_hist, pool_width), 0, q_ssd, nsub_ssd),
                (tp, bs, ls, state_ssm_conv[l], state_ssm[l], state_pool[l], past, ls, 1)):
            dt_g = dt_raw[row0:row0 + nb * sl].reshape(nb, sl, heads)
            y_ssm, hl = ssd_mix(proj, c_xbc, c_z, row0, nb, sl, conv_hist, ssm_conv_w[l], ssm_conv_b[l], dt_g,
                                ssm_dt_bias[l], ssm_a_log[l], ssm_d[l], ssm_norm[l], h0, y_ssm,
                                groups=groups, n_state=n_state, q=q_chunk, nsub=nsub)
            h_last.append(hl)
            y_pool = pool_mix(proj, c_u, row0, nb, sl, p_hist, pos_base, pool_wb[l], pool_scale[l], y_pool)
        y_diff = diff_attn_prompt(proj, c_q, c_k, c_v, bp, lp, dheads, dh, lam_vecs, diff_subln[l], lam_init)
        y_diff = diff_attn_sample(proj, c_q, c_k, c_v, tp, bs, ls, dheads, dh, k_cache, v_cache, l,
                                  lam_vecs, diff_subln[l], lam_init, y_diff)
        merged = branch_merge(y_ssm, y_pool, y_diff, proj, c_g, w_bs, w_bp, w_bd, l)
        h, xn = post_norm(mm(merged, w_out, l), h, norm_mix_post[l], norm_xa_pre[l])

        mem_kv = mm(rmsnorm_bf16(mem_rows, norm_mem[l]), w_kv, l)
        q_xa = mm(xn, w_q, l)
        kv_p = [pl.BlockSpec((n_mem, xa_w), lambda b, t: (b, 0)), pl.BlockSpec((n_mem, xa_w), lambda b, t: (b, 1))]
        kv_s = [pl.BlockSpec((None, None, n_mem, xa_w), lambda b, t, l=l: (l, b, 0, 0))] * 2
        o_xa = cross_attn(q_xa, 0, bp, lp, mem_kv, mem_kv, kv_p, xheads, xhd, None)
        o_xa = cross_attn(q_xa, tp, bs, ls, mem_k, mem_v, kv_s, xheads, xhd, o_xa)
        h, xn = mm_post_norm(o_xa, w_xo, l, h, norm_xa_post[l], norm_ffn_pre[l])

        hid, tail_g, tail_v = ffn_up_geglu(xn, bp, lp, w_up, l, ffp, zeros_p(k_ffn - 1, 2 * ffp),
                                           conv_w_ff[l], conv_b_ff[l])
        up_s = mm(xn, w_up, l, row0=tp, tm_pref=512)
        hid = conv_geglu(up_s, ffp, bs, ls, ffn_hist[l], conv_w_ff[l], conv_b_ff[l], hid, tp)
        g_next = norm_mix_pre[l + 1] if l + 1 < depth else None
        res = post_norm(mm(hid, w_down, l, tm_pref=512, tn_pref=512), h, norm_ffn_post[l], g_next)
        h, xn = res if g_next is not None else (res, None)

        conv_p, conv_s = tails(proj, c_xbc, xbc_w, k_ssm - 1)
        pool_p, pool_s = tails(proj, c_u, pool_width, pool_hist)
        kk_p, kk_s = both(proj, c_k, diff_w)
        vv_p, vv_s = both(proj, c_v, diff_w)
        ups = up_s.reshape(bs, ls, 2 * ffp)[:, ls - (k_ffn - 1):]
        ffn_p = jnp.concatenate([tail_g[:, -(k_ffn - 1):, :d_ff], tail_v[:, -(k_ffn - 1):, :d_ff]], axis=-1)
        ffn_s = jnp.concatenate([ups[..., :d_ff], ups[..., ffp:ffp + d_ff]], axis=-1)
        layer_out = (h_last[0], h_last[1], conv_p, conv_s, pool_p, pool_s,
                     kk_p.reshape(bp, lp, dheads, 2, dh), kk_s.reshape(bs, ls, dheads, 2, dh),
                     vv_p.reshape(bp, lp, dheads, 2 * dh), vv_s.reshape(bs, ls, dheads, 2 * dh),
                     mem_kv[:, :xa_w].reshape(bp, n_mem, xheads, xhd),
                     mem_kv[:, xa_w:].reshape(bp, n_mem, xheads, xhd),
                     ffn_p, ffn_s)
        for acc, val in zip(outs, layer_out):
            acc.append(val)

    return (h[:tp].reshape(bp, lp, dm), h[tp:].reshape(bs, ls, dm)) + tuple(jnp.stack(o) for o in outs)
```

```python
import functools
import math

import jax
import jax.numpy as jnp
from jax import lax
from jax.experimental import pallas as pl
from jax.experimental.pallas import tpu as pltpu

F32 = jnp.float32
BF16 = jnp.bfloat16
EPS = 1e-6
CHUNK = 64
POOL_WINDOWS = (2, 4, 8, 16)
SUBLANES = 8
PACKED_ROWS = 16
LANES = 128
VMEM_LIMIT_BYTES = 56 * 1024 * 1024
NT_DIMS = (((1,), (1,)), ((), ()))
TN_DIMS = (((0,), (0,)), ((), ()))
SINGLE = pl.Buffered(1)
FFN_ROW_CHUNK = 64
NORM_UNROLL = 8


def _tile(n, pref, align):
    t = (min(pref, n) // align) * align
    while t >= align:
        if n % t == 0:
            return t
        t -= align
    return n


def _round_up(n, m):
    return (n + m - 1) // m * m


def _params(*sem):
    return pltpu.CompilerParams(dimension_semantics=sem, vmem_limit_bytes=VMEM_LIMIT_BYTES)


def _sigmoid(x):
    return 1.0 / (1.0 + jnp.exp(-x))


def _softplus(x):
    return jnp.maximum(x, 0.0) + jnp.log(1.0 + jnp.exp(-jnp.abs(x)))


def _gelu_tanh(x):
    return 0.5 * x * (1.0 + jnp.tanh(math.sqrt(2.0 / math.pi) * (x + 0.044715 * (x * x * x))))


def _rms(x, g):
    return x * lax.rsqrt(jnp.mean(x * x, axis=-1, keepdims=True) + EPS) * g


def _dot_sel(x, sel, sel_first=False):
    hi = x.astype(BF16)
    rest = x - hi.astype(F32)
    mid = rest.astype(BF16)
    lo = (rest - mid.astype(F32)).astype(BF16)
    sel = sel.astype(BF16)
    dots = [jnp.dot(sel, t, preferred_element_type=F32) if sel_first else
            jnp.dot(t, sel, preferred_element_type=F32) for t in (hi, mid, lo)]
    return dots[0] + dots[1] + dots[2]


def _drop_ref(body, idx):
    def wrapped(*refs):
        return body(*refs[:idx], *refs[idx + 1:])
    return wrapped


def _into(joint, in_specs, operands):
    if joint is None:
        return in_specs, operands, {}
    return (in_specs + [pl.BlockSpec(memory_space=pl.ANY)], operands + [joint], {len(operands): 0})


def _norm_kernel(x_ref, g_ref, o_ref, *, rows):
    def body(r, carry):
        sl = pl.ds(pl.multiple_of(r * rows, rows), rows)
        o_ref[sl, :] = _rms(x_ref[sl, :], g_ref[...]).astype(BF16)
        return carry
    lax.fori_loop(0, x_ref.shape[0] // rows, body, 0, unroll=NORM_UNROLL)


def rmsnorm_bf16(x, g, *, tm_pref=256):
    t, d = x.shape
    tm = _tile(t, tm_pref, PACKED_ROWS)
    return pl.pallas_call(
        functools.partial(_norm_kernel, rows=PACKED_ROWS),
        grid=(t // tm,),
        in_specs=[pl.BlockSpec((tm, d), lambda i: (i, 0)), pl.BlockSpec((1, d), lambda i: (0, 0))],
        out_specs=pl.BlockSpec((tm, d), lambda i: (i, 0)),
        out_shape=jax.ShapeDtypeStruct((t, d), BF16),
        compiler_params=_params("parallel"),
    )(x, g.reshape(1, d))


def _mm_kernel(x_ref, w_ref, o_ref):
    o_ref[...] = jnp.dot(x_ref[...], w_ref[...], preferred_element_type=F32)


def mm(x, w_all, layer, *, row0=0, nrows=None, tm_pref=768, tn_pref=1024):
    k, n = w_all.shape[1:]
    assert k == x.shape[1] or (k < x.shape[1] and k % LANES == 0)
    nrows = x.shape[0] - row0 if nrows is None else nrows
    tm = _tile(math.gcd(nrows, row0) if row0 else nrows, tm_pref, PACKED_ROWS)
    tn = _tile(n, tn_pref, LANES)
    rb = row0 // tm
    return pl.pallas_call(
        _mm_kernel,
        grid=(nrows // tm, n // tn),
        in_specs=[pl.BlockSpec((tm, k), lambda i, j: (rb + i, 0)),
                  pl.BlockSpec((None, k, tn), lambda i, j: (layer, 0, j))],
        out_specs=pl.BlockSpec((tm, tn), lambda i, j: (i, j)),
        out_shape=jax.ShapeDtypeStruct((nrows, n), F32),
        compiler_params=_params("parallel", "parallel"),
    )(x, w_all)


def _cast_pad_kernel(x_ref, o_ref, *, real_blocks, blocks):
    is_real = pl.program_id(1) % blocks < real_blocks
    o_ref[...] = jnp.where(is_real, x_ref[...], 0.0).astype(BF16)


def cast_pad_halves(w, half, half_pad, *, blk_pref=512):
    depth, k, _ = w.shape
    blk = _tile(math.gcd(half, half_pad), blk_pref, LANES)
    if blk % LANES:
        pad = ((0, 0), (0, 0), (0, half_pad - half))
        return jnp.concatenate([jnp.pad(w[..., :half], pad), jnp.pad(w[..., half:], pad)], -1).astype(BF16)
    real_blocks, blocks = half // blk, half_pad // blk

    def src(l, j):
        return (l, 0, (j // blocks) * real_blocks + jnp.minimum(j % blocks, real_blocks - 1))

    return pl.pallas_call(
        functools.partial(_cast_pad_kernel, real_blocks=real_blocks, blocks=blocks),
        grid=(depth, 2 * blocks),
        in_specs=[pl.BlockSpec((None, k, blk), src)],
        out_specs=pl.BlockSpec((None, k, blk), lambda l, j: (l, 0, j)),
        out_shape=jax.ShapeDtypeStruct((depth, k, 2 * half_pad), BF16),
        compiler_params=_params("parallel", "parallel"),
    )(w)


def _post_norm_kernel(y_ref, res_ref, g_ref, gn_ref, h_ref, *maybe_xn_ref, rows):
    def body(r, carry):
        sl = pl.ds(pl.multiple_of(r * rows, rows), rows)
        h_new = res_ref[sl, :] + _rms(y_ref[sl, :], g_ref[...])
        h_ref[sl, :] = h_new
        for xn_ref in maybe_xn_ref:
            xn_ref[sl, :] = _rms(h_new, gn_ref[...]).astype(BF16)
        return carry
    lax.fori_loop(0, y_ref.shape[0] // rows, body, 0, unroll=NORM_UNROLL)


def _mm_post_norm_kernel(x_ref, w_ref, res_ref, g_ref, gn_ref, h_ref, *rest, rows):
    *maybe_xn_ref, y_ref = rest
    y_ref[...] = jnp.dot(x_ref[...], w_ref[...], preferred_element_type=F32)
    _post_norm_kernel(y_ref, res_ref, g_ref, gn_ref, h_ref, *maybe_xn_ref, rows=rows)


def mm_post_norm(x, w_all, layer, res, g, g_next, *, tm_pref=256):
    t, k = x.shape
    d = w_all.shape[2]
    tm = _tile(t, tm_pref, PACKED_ROWS)
    row_spec = pl.BlockSpec((tm, d), lambda i: (i, 0))
    vec_spec = pl.BlockSpec((1, d), lambda i: (0, 0))
    emit = g_next is not None
    out = pl.pallas_call(
        functools.partial(_mm_post_norm_kernel, rows=PACKED_ROWS),
        grid=(t // tm,),
        in_specs=[pl.BlockSpec((tm, k), lambda i: (i, 0)),
                  pl.BlockSpec((None, k, d), lambda i: (layer, 0, 0), pipeline_mode=SINGLE),
                  row_spec, vec_spec, vec_spec],
        out_specs=[row_spec, row_spec] if emit else [row_spec],
        out_shape=[jax.ShapeDtypeStruct((t, d), F32)] + ([jax.ShapeDtypeStruct((t, d), BF16)] if emit else []),
        scratch_shapes=[pltpu.VMEM((tm, d), F32)],
        compiler_params=_params("parallel"),
    )(x, w_all, res, g.reshape(1, d), (g_next if emit else g).reshape(1, d))
    return tuple(out) if emit else out[0]


def post_norm(y, res, g, g_next, *, row0=0, nrows=None, tm_pref=256):
    d = y.shape[1]
    t = y.shape[0] - row0 if nrows is None else nrows
    tm = _tile(math.gcd(t, row0) if row0 else t, tm_pref, PACKED_ROWS)
    rb = row0 // tm
    in_row_spec = pl.BlockSpec((tm, d), lambda i: (rb + i, 0))
    row_spec = pl.BlockSpec((tm, d), lambda i: (i, 0))
    vec_spec = pl.BlockSpec((1, d), lambda i: (0, 0))
    emit = g_next is not None
    out = pl.pallas_call(
        functools.partial(_post_norm_kernel, rows=PACKED_ROWS),
        grid=(t // tm,),
        in_specs=[in_row_spec, in_row_spec, vec_spec, vec_spec],
        out_specs=[row_spec, row_spec] if emit else [row_spec],
        out_shape=[jax.ShapeDtypeStruct((t, d), F32)] + ([jax.ShapeDtypeStruct((t, d), BF16)] if emit else []),
        compiler_params=_params("parallel"),
    )(y, res, g.reshape(1, d), (g_next if emit else g).reshape(1, d))
    return tuple(out) if emit else out[0]


def _merge_kernel(ys_ref, yp_ref, yd_ref, g0_ref, g1_ref, g2_ref, ws_ref, wp_ref, wd_ref, o_ref):
    acc = _sigmoid(g0_ref[...]) * jnp.dot(ys_ref[...], ws_ref[...], preferred_element_type=F32)
    acc += _sigmoid(g1_ref[...]) * jnp.dot(yp_ref[...], wp_ref[...], preferred_element_type=F32)
    acc += _sigmoid(g2_ref[...]) * jnp.dot(yd_ref[...], wd_ref[...], preferred_element_type=F32)
    o_ref[...] = acc.astype(BF16)


def branch_merge(y_ssm, y_pool, y_diff, proj, gate_col0, w_s, w_p, w_d, layer, *, tm_pref=768, tn_pref=1024):
    t = y_ssm.shape[0]
    d = w_s.shape[2]
    tm = _tile(t, tm_pref, PACKED_ROWS)
    tn = _tile(d, tn_pref, LANES)
    assert gate_col0 % tn == 0
    nj = d // tn
    gj = gate_col0 // tn

    def gate_spec(br):
        return pl.BlockSpec((tm, tn), lambda i, j: (i, gj + br * nj + j))

    def x_spec(kdim):
        return pl.BlockSpec((tm, kdim), lambda i, j: (i, 0))

    def w_spec(kdim):
        return pl.BlockSpec((None, kdim, tn), lambda i, j: (layer, 0, j))

    return pl.pallas_call(
        _merge_kernel,
        grid=(t // tm, nj),
        in_specs=[x_spec(y_ssm.shape[1]), x_spec(y_pool.shape[1]), x_spec(y_diff.shape[1]),
                  gate_spec(0), gate_spec(1), gate_spec(2),
                  w_spec(w_s.shape[1]), w_spec(w_p.shape[1]), w_spec(w_d.shape[1])],
        out_specs=pl.BlockSpec((tm, tn), lambda i, j: (i, j)),
        out_shape=jax.ShapeDtypeStruct((t, d), BF16),
        compiler_params=_params("parallel", "parallel"),
    )(y_ssm, y_pool, y_diff, proj, proj, proj, w_s, w_p, w_d)


def _conv_rows(ext_ref, u, halo, w_ref, b_ref, k):
    n = u.shape[0]
    ext_ref[0:SUBLANES, :] = halo
    ext_ref[SUBLANES:SUBLANES + n, :] = u
    acc = u * w_ref[k - 1:k, :]
    for i in range(k - 1):
        acc = acc + ext_ref[pl.ds(SUBLANES - (k - 1) + i, n), :] * w_ref[i:i + 1, :]
    return acc + b_ref[...]


def _halo(prev_ref, hist_ref, s, nseq):
    if nseq > 1:
        return hist_ref[s]
    return jnp.where(pl.program_id(1) == 0, hist_ref[0], prev_ref[...])


def _seq_tiling(nb, seq_len, tl_pref, align):
    if seq_len >= tl_pref:
        return _tile(seq_len, tl_pref, align), 1
    nseq = _tile(nb, max(tl_pref // seq_len, 1), 1)
    return nseq * seq_len, nseq


def _seq_specs(row0, rows_per_b, tl, tc, col_blk0):
    blk0 = row0 // tl
    per_b = rows_per_b // tl
    sub0 = row0 // SUBLANES
    sub_per_b = rows_per_b // SUBLANES
    sub_per_tile = tl // SUBLANES
    cur = pl.BlockSpec((tl, tc), lambda b, t, c: (blk0 + b * per_b + t, col_blk0 + c))
    prev = pl.BlockSpec(
        (SUBLANES, tc),
        lambda b, t, c: (jnp.maximum(sub0 + b * sub_per_b + t * sub_per_tile - 1, 0), col_blk0 + c))
    return cur, prev


def _pad_hist(hist, rows):
    return jnp.pad(hist, ((0, 0), (rows - hist.shape[1], 0), (0, 0)))


def _conv_geglu_kernel(ug_ref, pg_ref, hg_ref, wg_ref, bg_ref, uv_ref, pv_ref, hv_ref, wv_ref, bv_ref,
                       o_ref, extg_ref, extv_ref, *, k, sl, nseq):
    for s in range(nseq):
        rows = slice(s * sl, (s + 1) * sl)
        gate = _conv_rows(extg_ref, ug_ref[rows, :], _halo(pg_ref, hg_ref, s, nseq), wg_ref, bg_ref, k)
        val = _conv_rows(extv_ref, uv_ref[rows, :], _halo(pv_ref, hv_ref, s, nseq), wv_ref, bv_ref, k)
        o_ref[rows, :] = (_gelu_tanh(gate) * val).astype(BF16)


def conv_geglu(up, half, nb, seq_len, hist, w, bias, joint, out_row0, *, tl_pref=512, tc_pref=1024):
    k = w.shape[0]
    tl, nseq = _seq_tiling(nb, seq_len, tl_pref, PACKED_ROWS)
    tc = _tile(half, tc_pref, LANES)
    rows_per_b = max(tl, seq_len)
    per_b = rows_per_b // tl
    ncb = half // tc
    assert out_row0 % tl == 0
    ob = out_row0 // tl
    cur_g, prev_g = _seq_specs(0, rows_per_b, tl, tc, 0)
    cur_v, prev_v = _seq_specs(0, rows_per_b, tl, tc, ncb)
    hist8 = _pad_hist(hist, SUBLANES)
    bias2 = bias.reshape(1, 2 * half)

    def side(off):
        return [pl.BlockSpec((nseq, SUBLANES, tc), lambda b, t, c: (b, 0, off + c)),
                pl.BlockSpec((k, tc), lambda b, t, c: (0, off + c)),
                pl.BlockSpec((1, tc), lambda b, t, c: (0, off + c))]

    in_specs, operands, aliases = _into(
        joint, [cur_g, prev_g] + side(0) + [cur_v, prev_v] + side(ncb),
        [up, up, hist8, w, bias2, up, up, hist8, w, bias2])
    body = functools.partial(_conv_geglu_kernel, k=k, sl=tl // nseq, nseq=nseq)
    return pl.pallas_call(
        _drop_ref(body, 10) if aliases else body,
        grid=(nb // nseq, per_b, ncb),
        in_specs=in_specs,
        out_specs=pl.BlockSpec((tl, tc), lambda b, t, c: (ob + b * per_b + t, c)),
        out_shape=jax.ShapeDtypeStruct(joint.shape, BF16),
        input_output_aliases=aliases,
        scratch_shapes=[pltpu.VMEM((SUBLANES + tl // nseq, tc), F32)] * 2,
        compiler_params=_params("parallel", "parallel", "parallel"),
    )(*operands)


def _ffn_up_kernel(xc_ref, xp_ref, wg_ref, wv_ref, hg_ref, hv_ref, cwg_ref, cbg_ref, cwv_ref, cbv_ref,
                   hid_ref, tg_ref, tv_ref, xe_ref, eg_ref, ev_ref, gg_ref, *, k, tm, tiles_per_seq):
    halo = PACKED_ROWS

    @pl.when(pl.program_id(1) == 0)
    def _():
        xe_ref[0:halo, :] = xp_ref[...]
        xe_ref[halo:halo + tm, :] = xc_ref[...]

    first = pl.program_id(0) % tiles_per_seq == 0

    def project(w_ref, h_ref, e_ref, t_ref):
        e_ref[...] = jnp.dot(xe_ref[...], w_ref[...], preferred_element_type=F32)
        e_ref[0:halo, :] = jnp.where(first, h_ref[...], e_ref[0:halo, :])
        t_ref[...] = e_ref[tm:tm + halo, :]

    def conv(e_ref, cw_ref, cb_ref, r0, n):
        acc = e_ref[halo + r0:halo + r0 + n, :] * cw_ref[k - 1:k, :]
        for i in range(k - 1):
            acc = acc + e_ref[pl.ds(halo + r0 - (k - 1) + i, n), :] * cw_ref[i:i + 1, :]
        return acc + cb_ref[...]

    chunk = _tile(tm, FFN_ROW_CHUNK, PACKED_ROWS)
    project(wg_ref, hg_ref, eg_ref, tg_ref)
    for r0 in range(0, tm, chunk):
        gg_ref[r0:r0 + chunk, :] = _gelu_tanh(conv(eg_ref, cwg_ref, cbg_ref, r0, chunk))
    project(wv_ref, hv_ref, ev_ref, tv_ref)
    for r0 in range(0, tm, chunk):
        val = conv(ev_ref, cwv_ref, cbv_ref, r0, chunk)
        hid_ref[r0:r0 + chunk, :] = (gg_ref[r0:r0 + chunk, :] * val).astype(BF16)


def ffn_up_geglu(xn, nb, seq_len, w_up_all, layer, half, hist, conv_w, conv_b, *, tm_pref=1024, tf_pref=512):
    t, d = xn.shape
    k = conv_w.shape[0]
    halo = PACKED_ROWS
    tm = _tile(seq_len, tm_pref, halo)
    tf = _tile(half, tf_pref, LANES)
    tiles_per_seq = seq_len // tm
    ncb = half // tf
    hist16 = _pad_hist(hist, halo)
    bias2 = conv_b.reshape(1, 2 * half)

    def w_spec(off):
        return pl.BlockSpec((None, d, tf), lambda i, j: (layer, 0, off + j))

    def hist_spec(off):
        return pl.BlockSpec((None, halo, tf), lambda i, j: (i // tiles_per_seq, 0, off + j))

    def cw_spec(off):
        return pl.BlockSpec((k, tf), lambda i, j: (0, off + j))

    def cb_spec(off):
        return pl.BlockSpec((1, tf), lambda i, j: (0, off + j))

    tail_spec = pl.BlockSpec((None, halo, tf), lambda i, j: (i, 0, j))
    tail_shape = jax.ShapeDtypeStruct((nb * tiles_per_seq, halo, half), F32)
    hid, tail_g, tail_v = pl.pallas_call(
        functools.partial(_ffn_up_kernel, k=k, tm=tm, tiles_per_seq=tiles_per_seq),
        grid=(nb * tiles_per_seq, ncb),
        in_specs=[pl.BlockSpec((tm, d), lambda i, j: (i, 0), pipeline_mode=SINGLE),
                  pl.BlockSpec((halo, d), lambda i, j: (jnp.maximum(i * (tm // halo) - 1, 0), 0)),
                  w_spec(0), w_spec(ncb), hist_spec(0), hist_spec(ncb),
                  cw_spec(0), cb_spec(0), cw_spec(ncb), cb_spec(ncb)],
        out_specs=[pl.BlockSpec((tm, tf), lambda i, j: (i, j)), tail_spec, tail_spec],
        out_shape=[jax.ShapeDtypeStruct((t, half), BF16), tail_shape, tail_shape],
        scratch_shapes=[pltpu.VMEM((halo + tm, d), BF16), pltpu.VMEM((halo + tm, tf), F32),
                        pltpu.VMEM((halo + tm, tf), F32), pltpu.VMEM((tm, tf), F32)],
        compiler_params=_params("parallel", "arbitrary"),
    )(xn, xn, w_up_all, w_up_all, hist16, hist16, conv_w, bias2, conv_w, bias2)
    last = slice(tiles_per_seq - 1, None, tiles_per_seq)
    return hid, tail_g[last], tail_v[last]


def _pool_kernel(u_ref, prev_ref, hist_ref, pw_ref, ps_ref, o_ref, ext_ref, *, tl, halo, pos_base, gd):
    t = pl.program_id(1)
    ext_ref[0:halo, :] = jnp.where(t == 0, hist_ref[...], prev_ref[...])
    ext_ref[halo:halo + tl, :] = u_ref[...]
    pos = pos_base + t * tl + lax.broadcasted_iota(jnp.int32, (tl, 1), 0)
    for gi, win in enumerate(POOL_WINDOWS):
        cols = slice(gi * gd, (gi + 1) * gd)
        cur = u_ref[:, cols]
        wsum = cur
        for i in range(1, win):
            wsum = wsum + ext_ref[pl.ds(halo - i, tl), cols]
        cnt = jnp.minimum(pos + 1, win).astype(F32)
        pooled = wsum / cnt - cur
        y = jnp.dot(pooled.astype(BF16), pw_ref[gi], preferred_element_type=F32) * ps_ref[:, cols]
        o_ref[:, cols] = y.astype(BF16)


def pool_mix(proj, col0, row0, nb, seq_len, hist, pos_base, pool_w, pool_scale, joint, *, tl_pref=512):
    ng, gd, _ = pool_w.shape
    width = ng * gd
    halo = 2 * SUBLANES
    assert ng == len(POOL_WINDOWS) and max(POOL_WINDOWS) <= halo
    tl = _tile(seq_len, tl_pref, halo)
    assert row0 % tl == 0 and col0 % width == 0
    blk0, per_seq = row0 // tl, seq_len // tl
    h0, h_per_seq, h_per_tile = row0 // halo, seq_len // halo, tl // halo
    cb = col0 // width
    in_specs, operands, aliases = _into(
        joint,
        [pl.BlockSpec((tl, width), lambda b, t: (blk0 + b * per_seq + t, cb)),
         pl.BlockSpec((halo, width),
                      lambda b, t: (jnp.maximum(h0 + b * h_per_seq + t * h_per_tile - 1, 0), cb)),
         pl.BlockSpec((None, halo, width), lambda b, t: (b, 0, 0)),
         pl.BlockSpec((ng, gd, gd), lambda b, t: (0, 0, 0)),
         pl.BlockSpec((1, width), lambda b, t: (0, 0))],
        [proj, proj, _pad_hist(hist, halo), pool_w, pool_scale.reshape(1, width)])
    body = functools.partial(_pool_kernel, tl=tl, halo=halo, pos_base=pos_base, gd=gd)
    return pl.pallas_call(
        _drop_ref(body, 5) if aliases else body,
        grid=(nb, per_seq),
        in_specs=in_specs,
        out_specs=pl.BlockSpec((tl, width), lambda b, t: (blk0 + b * per_seq + t, 0)),
        out_shape=jax.ShapeDtypeStruct((proj.shape[0], width), BF16),
        input_output_aliases=aliases,
        scratch_shapes=[pltpu.VMEM((halo + tl, width), F32)],
        compiler_params=_params("parallel", "parallel"),
    )(*operands)


def _ssd_kernel(*refs, q, nsub, hpg, p, k):
    conv_in, refs = refs[:15], refs[15:]
    (z_ref, dtc_ref, dtr_ref, biasc_ref, biasr_ref, alogc_ref, alogr_ref, dskip_ref, nw_ref, h0_ref,
     y_ref, hl_ref, ht_ref, xs_ref, b_ref, c_ref, ext_x_ref, ext_b_ref, ext_c_ref) = refs
    ti = pl.program_id(2)
    gw = hpg * p
    n = b_ref.shape[1]
    for i, (dst_ref, ext_ref) in enumerate(((xs_ref, ext_x_ref), (b_ref, ext_b_ref), (c_ref, ext_c_ref))):
        cur_ref, prev_ref, hist_ref, cw_ref, cb_ref = conv_in[5 * i:5 * i + 5]
        halo = jnp.where(ti == 0, hist_ref[...], prev_ref[...])
        u = _conv_rows(ext_ref, cur_ref[...], halo, cw_ref, cb_ref, k)
        dst_ref[...] = u * _sigmoid(u)

    @pl.when(ti == 0)
    def _():
        ht_ref[...] = h0_ref[...].reshape(gw, n).T

    dt_c_all = _softplus(dtc_ref[...] + biasc_ref[...])
    dt_r_all = _softplus(dtr_ref[...] + biasr_ref[...])
    a_c = -jnp.exp(alogc_ref[...])
    a_r = -jnp.exp(alogr_ref[...])
    row = lax.broadcasted_iota(jnp.int32, (q, q), 0)
    col = lax.broadcasted_iota(jnp.int32, (q, q), 1)
    lower = row >= col
    upper = row <= col

    def spread(width, rep):
        head = lax.broadcasted_iota(jnp.int32, (hpg, width), 0)
        lane = lax.broadcasted_iota(jnp.int32, (hpg, width), 1)
        return lane // rep == head

    to_lanes = spread(gw, p)
    to_keys = spread(hpg * q, q)
    lane_head = lax.broadcasted_iota(jnp.int32, (1, gw), 1) // p
    d_skip = _dot_sel(dskip_ref[...], to_lanes)
    ht = ht_ref[...]
    for s in range(nsub):
        rows = slice(s * q, (s + 1) * q)
        dt_c = dt_c_all[rows, :]
        dt_r = dt_r_all[:, rows]
        acum_c = _dot_sel(dt_c * a_c, lower, sel_first=True)
        acum_r = _dot_sel(dt_r * a_r, upper)
        a_last = acum_c[q - 1:q, :]
        per_head = jnp.concatenate([jnp.exp(acum_c), jnp.exp(a_last - acum_c) * dt_c], axis=0)
        per_lane = _dot_sel(per_head, to_lanes)
        exp_a, w_end = per_lane[:q], per_lane[q:]
        chunk_decay = _dot_sel(jnp.exp(a_last), to_lanes)
        ac_keys = _dot_sel(acum_c, to_keys)
        x = xs_ref[rows, :]
        bm = b_ref[rows, :].astype(BF16)
        cm = c_ref[rows, :].astype(BF16)
        cb = lax.dot_general(cm, bm, NT_DIMS, preferred_element_type=F32)
        wmats = []
        for r in range(hpg):
            seg = ac_keys[:, r * q:(r + 1) * q] - acum_r[r:r + 1, :]
            decay = jnp.exp(jnp.where(lower, seg, -jnp.inf))
            wmats.append((cb * decay * dt_r[r:r + 1, :]).astype(BF16))
        y_all = jnp.dot(jnp.concatenate(wmats, axis=0), x.astype(BF16), preferred_element_type=F32)
        y = y_all[:q]
        for r in range(1, hpg):
            y = jnp.where(lane_head == r, y_all[r * q:(r + 1) * q], y)
        y = y + jnp.dot(cm, ht.astype(BF16), preferred_element_type=F32) * exp_a
        s_new = lax.dot_general(bm, (x * w_end).astype(BF16), TN_DIMS, preferred_element_type=F32)
        ht = ht * chunk_decay + s_new
        zt = z_ref[rows, :]
        yg = (y + x * d_skip) * (zt * _sigmoid(zt))
        y_ref[rows, :] = _rms(yg, nw_ref[...]).astype(BF16)
    ht_ref[...] = ht

    @pl.when(ti == pl.num_programs(2) - 1)
    def _():
        hl_ref[...] = ht.T.reshape(hpg, p, n)


def ssd_mix(proj, xbc_col0, z_col0, row0, nb, seq_len, conv_hist, conv_w, conv_b, dt_raw, dt_bias, a_log,
            d_skip, norm_w, h0, joint, *, groups, n_state, q, nsub):
    heads, p = h0.shape[1], h0.shape[2]
    k = conv_w.shape[0]
    hpg = heads // groups
    gw = hpg * p
    d_inner = heads * p
    xbc_w = d_inner + 2 * groups * n_state
    tl = q * nsub
    nt = seq_len // tl
    assert seq_len % tl == 0 and row0 % tl == 0 and z_col0 % gw == 0
    assert xbc_col0 % gw == 0 and (xbc_col0 + d_inner) % n_state == 0 and d_inner % n_state == 0
    dt_g = dt_raw.reshape(nb, seq_len, groups, hpg)
    dt_col = dt_g.transpose(0, 2, 1, 3)
    dt_row = dt_g.transpose(0, 2, 3, 1)
    blk0 = row0 // tl
    sub0, sub_per_seq, sub_per_tile = row0 // SUBLANES, seq_len // SUBLANES, tl // SUBLANES
    zc = z_col0 // gw
    hist8 = _pad_hist(conv_hist, SUBLANES)
    bias2 = conv_b.reshape(1, xbc_w)

    def conv_inputs(width, col_of_group):
        src = xbc_col0 // width
        specs = [pl.BlockSpec((tl, width), lambda b, g, t: (blk0 + b * nt + t, src + col_of_group(g))),
                 pl.BlockSpec((SUBLANES, width),
                              lambda b, g, t: (jnp.maximum(sub0 + b * sub_per_seq + t * sub_per_tile - 1, 0),
                                               src + col_of_group(g))),
                 pl.BlockSpec((None, SUBLANES, width), lambda b, g, t: (b, 0, col_of_group(g))),
                 pl.BlockSpec((k, width), lambda b, g, t: (0, col_of_group(g))),
                 pl.BlockSpec((1, width), lambda b, g, t: (0, col_of_group(g)))]
        return specs, [proj, proj, hist8, conv_w, bias2]

    bcol = d_inner // n_state
    conv_specs, conv_ops = [], []
    for width, col_of_group in ((gw, lambda g: g), (n_state, lambda g: bcol + g),
                                (n_state, lambda g: bcol + groups + g)):
        specs, ops = conv_inputs(width, col_of_group)
        conv_specs += specs
        conv_ops += ops

    def per_group(arr, shape):
        return arr.reshape((groups,) + shape), pl.BlockSpec((None,) + shape, lambda b, g, t: (g, 0, 0))

    biasc, biasc_spec = per_group(dt_bias, (1, hpg))
    biasr, biasr_spec = per_group(dt_bias, (hpg, 1))
    alogc, alogc_spec = per_group(a_log, (1, hpg))
    alogr, alogr_spec = per_group(a_log, (hpg, 1))
    dsk, dsk_spec = per_group(d_skip, (1, hpg))
    h_spec = pl.BlockSpec((None, hpg, p, n_state), lambda b, g, t: (b, g, 0, 0))
    in_specs, operands, aliases = _into(
        joint,
        conv_specs +
        [pl.BlockSpec((tl, gw), lambda b, g, t: (blk0 + b * nt + t, zc + g)),
         pl.BlockSpec((None, None, tl, hpg), lambda b, g, t: (b, g, t, 0)),
         pl.BlockSpec((None, None, hpg, tl), lambda b, g, t: (b, g, 0, t)),
         biasc_spec, biasr_spec, alogc_spec, alogr_spec, dsk_spec,
         pl.BlockSpec((1, gw), lambda b, g, t: (0, g)),
         h_spec],
        conv_ops + [proj, dt_col, dt_row, biasc, biasr, alogc, alogr, dsk, norm_w.reshape(1, d_inner), h0])
    body = functools.partial(_ssd_kernel, q=q, nsub=nsub, hpg=hpg, p=p, k=k)
    return pl.pallas_call(
        _drop_ref(body, len(operands) - 1) if aliases else body,
        grid=(nb, groups, nt),
        in_specs=in_specs,
        out_specs=[pl.BlockSpec((tl, gw), lambda b, g, t: (blk0 + b * nt + t, g)), h_spec],
        out_shape=[jax.ShapeDtypeStruct((proj.shape[0], d_inner), BF16),
                   jax.ShapeDtypeStruct(h0.shape, F32)],
        input_output_aliases=aliases,
        scratch_shapes=[pltpu.VMEM((n_state, gw), F32),
                        pltpu.VMEM((tl, gw), F32), pltpu.VMEM((tl, n_state), F32), pltpu.VMEM((tl, n_state), F32),
                        pltpu.VMEM((SUBLANES + tl, gw), F32), pltpu.VMEM((SUBLANES + tl, n_state), F32),
                        pltpu.VMEM((SUBLANES + tl, n_state), F32)],
        compiler_params=_params("parallel", "parallel", "arbitrary"),
    )(*operands)


def _lambda(lq1_ref, lk1_ref, lq2_ref, lk2_ref, lam_init):
    s1 = jnp.sum(lq1_ref[...] * lk1_ref[...], axis=-1, keepdims=True)
    s2 = jnp.sum(lq2_ref[...] * lk2_ref[...], axis=-1, keepdims=True)
    return jnp.exp(s1) - jnp.exp(s2) + lam_init


def _loop_by_two(n, body, init):
    pairs = n // 2
    carry = lax.fori_loop(0, pairs, lambda i, c: body(2 * i + 1, body(2 * i, c)), init)
    return lax.fori_loop(2 * pairs, n, body, carry)


def _fold_lanes(a, op):
    if a.shape[1] % LANES:
        return a
    parts = [a[:, i:i + LANES] for i in range(0, a.shape[1], LANES)]
    while len(parts) > 1:
        parts = [op(parts[i], parts[i + 1]) for i in range(0, len(parts) - 1, 2)] + parts[len(parts) & ~1:]
    return parts[0]


def _diff_prompt_kernel(q_ref, k_ref, v_ref, lq1_ref, lk1_ref, lq2_ref, lk2_ref, sub_ref, o_ref,
                        s_ref, acc_ref, *, t, tk, d, scale, lam_init):
    iq = pl.program_id(2)
    lam = _lambda(lq1_ref, lk1_ref, lq2_ref, lk2_ref, lam_init)
    n_full = (iq * t) // tk
    q_chunk = (iq * t + lax.broadcasted_iota(jnp.int32, (t, 1), 0)) // CHUNK
    k_chunk = (n_full * tk + lax.broadcasted_iota(jnp.int32, (1, tk), 1)) // CHUNK
    visible = k_chunk <= q_chunk
    width = LANES if tk % LANES == 0 else tk

    def tile(j):
        return pl.ds(pl.multiple_of(j * tk, tk), tk)

    outs = []
    for m in range(2):
        cols = slice(m * d, (m + 1) * d)
        qm = (q_ref[:, cols] * scale).astype(BF16)

        def scores(j, qm=qm, cols=cols):
            kt = k_ref[tile(j), cols].astype(BF16)
            return lax.dot_general(qm, kt, NT_DIMS, preferred_element_type=F32)

        def pass1(j, mx, scores=scores):
            s = scores(j)
            s_ref[:, tile(j)] = s
            return jnp.maximum(mx, _fold_lanes(s, jnp.maximum))

        mx = _loop_by_two(n_full, pass1, jnp.full((t, width), -jnp.inf, F32))
        s_last = jnp.where(visible, scores(n_full), -jnp.inf)
        s_ref[:, tile(n_full)] = s_last
        mx = jnp.max(jnp.maximum(mx, _fold_lanes(s_last, jnp.maximum)), axis=-1, keepdims=True)
        acc_ref[...] = jnp.zeros(acc_ref.shape, F32)

        def pass2(j, den, mx=mx):
            pr = jnp.exp(s_ref[:, tile(j)] - mx)
            vt = v_ref[tile(j), :].astype(BF16)
            acc_ref[...] += jnp.dot(pr.astype(BF16), vt, preferred_element_type=F32)
            return den + _fold_lanes(pr, jnp.add)

        den = _loop_by_two(n_full + 1, pass2, jnp.zeros((t, width), F32))
        outs.append(acc_ref[...] / jnp.sum(den, axis=-1, keepdims=True))
    o = outs[0] - lam * outs[1]
    o_ref[...] = (_rms(o, sub_ref[...]) * (1.0 - lam_init)).astype(BF16)


def _lam_specs(d):
    fixed = lambda *_: (0, 0)
    return [pl.BlockSpec((1, d), fixed)] * 4 + [pl.BlockSpec((1, 2 * d), fixed)]


def diff_attn_prompt(proj, q_col0, k_col0, v_col0, nb, seq_len, heads, d, lam_vecs, subln, lam_init, *,
                     tq_pref=256):
    hw = 2 * d
    tq = _tile(seq_len, tq_pref, CHUNK)
    nq = seq_len // tq
    tk = 2 * tq if nq % 2 == 0 else tq
    assert q_col0 % hw == 0 and k_col0 % hw == 0 and v_col0 % hw == 0
    qc, kc, vc = q_col0 // hw, k_col0 // hw, v_col0 // hw
    return pl.pallas_call(
        functools.partial(_diff_prompt_kernel, t=tq, tk=tk, d=d, scale=1.0 / math.sqrt(d), lam_init=lam_init),
        grid=(nb, heads, nq),
        in_specs=[pl.BlockSpec((tq, hw), lambda b, h, i: (b * nq + i, qc + h)),
                  pl.BlockSpec((seq_len, hw), lambda b, h, i: (b, kc + h)),
                  pl.BlockSpec((seq_len, hw), lambda b, h, i: (b, vc + h))] + _lam_specs(d),
        out_specs=pl.BlockSpec((tq, hw), lambda b, h, i: (b * nq + i, h)),
        out_shape=jax.ShapeDtypeStruct((proj.shape[0], heads * hw), BF16),
        scratch_shapes=[pltpu.VMEM((tq, seq_len), F32), pltpu.VMEM((tq, hw), F32)],
        compiler_params=_params("parallel", "parallel", "parallel"),
    )(proj, proj, proj, *[v.reshape(1, d) for v in lam_vecs], subln.reshape(1, hw))


def _diff_sample_kernel(q_ref, kn_ref, vn_ref, kp_ref, vp_ref, lq1_ref, lk1_ref, lq2_ref, lk2_ref, sub_ref,
                        o_ref, *, heads, d, past, scale, lam_init):
    lam = _lambda(lq1_ref, lk1_ref, lq2_ref, lk2_ref, lam_init)
    hw = 2 * d
    for h in range(heads):
        vp = vp_ref[:, h * hw:(h + 1) * hw].astype(BF16)
        vn = vn_ref[:, h * hw:(h + 1) * hw].astype(BF16)
        outs = []
        for m in range(2):
            cols = slice(h * hw + m * d, h * hw + (m + 1) * d)
            qm = (q_ref[:, cols] * scale).astype(BF16)
            kp = kp_ref[pl.ds(2 * h + m, past, stride=2 * heads), :].astype(BF16)
            s_p = lax.dot_general(qm, kp, NT_DIMS, preferred_element_type=F32)
            s_n = lax.dot_general(qm, kn_ref[:, cols].astype(BF16), NT_DIMS, preferred_element_type=F32)
            mx = jnp.maximum(jnp.max(s_p, axis=-1, keepdims=True), jnp.max(s_n, axis=-1, keepdims=True))
            p_p = jnp.exp(s_p - mx)
            p_n = jnp.exp(s_n - mx)
            den = jnp.sum(p_p, axis=-1, keepdims=True) + jnp.sum(p_n, axis=-1, keepdims=True)
            acc = jnp.dot(p_p.astype(BF16), vp, preferred_element_type=F32)
            acc += jnp.dot(p_n.astype(BF16), vn, preferred_element_type=F32)
            outs.append(acc / den)
        o = outs[0] - lam * outs[1]
        o_ref[:, h * hw:(h + 1) * hw] = (_rms(o, sub_ref[...]) * (1.0 - lam_init)).astype(BF16)


def diff_attn_sample(proj, q_col0, k_col0, v_col0, row0, nb, seq_len, heads, d, k_cache, v_cache, layer,
                     lam_vecs, subln, lam_init, joint):
    width = heads * 2 * d
    past = v_cache.shape[2]
    assert row0 % seq_len == 0 and q_col0 % width == 0 and k_col0 % width == 0 and v_col0 % width == 0
    rb = row0 // seq_len

    def new_spec(col0):
        return pl.BlockSpec((seq_len, width), lambda b: (rb + b, col0 // width))

    in_specs, operands, aliases = _into(
        joint,
        [new_spec(q_col0), new_spec(k_col0), new_spec(v_col0),
         pl.BlockSpec((None, None, past * heads * 2, d), lambda b: (layer, b, 0, 0)),
         pl.BlockSpec((None, None, past, width), lambda b: (layer, b, 0, 0))] + _lam_specs(d),
        [proj, proj, proj, k_cache, v_cache] + [v.reshape(1, d) for v in lam_vecs] + [subln.reshape(1, 2 * d)])
    body = functools.partial(_diff_sample_kernel, heads=heads, d=d, past=past, scale=1.0 / math.sqrt(d),
                             lam_init=lam_init)
    return pl.pallas_call(
        _drop_ref(body, 10),
        grid=(nb,),
        in_specs=in_specs,
        out_specs=pl.BlockSpec((seq_len, width), lambda b: (rb + b, 0)),
        out_shape=jax.ShapeDtypeStruct(joint.shape, BF16),
        input_output_aliases=aliases,
        compiler_params=_params("parallel"),
    )(*operands)


def _xattn_kernel(q_ref, k_ref, v_ref, o_ref, *, heads, hd, scale):
    for h in range(heads):
        cols = slice(h * hd, (h + 1) * hd)
        qh = (q_ref[:, cols] * scale).astype(BF16)
        s = lax.dot_general(qh, k_ref[:, cols].astype(BF16), NT_DIMS, preferred_element_type=F32)
        pr = jnp.exp(s - jnp.max(s, axis=-1, keepdims=True))
        den = jnp.sum(pr, axis=-1, keepdims=True)
        o = jnp.dot(pr.astype(BF16), v_ref[:, cols].astype(BF16), preferred_element_type=F32)
        o_ref[:, cols] = (o / den).astype(BF16)


def cross_attn(q_all, row0, nb, seq_len, k_arr, v_arr, kv_specs, heads, hd, joint, *, tq_pref=512):
    width = heads * hd
    tq = _tile(seq_len, tq_pref, PACKED_ROWS)
    assert row0 % tq == 0
    blk0, per_seq = row0 // tq, seq_len // tq
    in_specs, operands, aliases = _into(
        joint, [pl.BlockSpec((tq, width), lambda b, t: (blk0 + b * per_seq + t, 0))] + kv_specs,
        [q_all, k_arr, v_arr])
    body = functools.partial(_xattn_kernel, heads=heads, hd=hd, scale=1.0 / math.sqrt(hd))
    return pl.pallas_call(
        _drop_ref(body, 3) if aliases else body,
        grid=(nb, per_seq),
        in_specs=in_specs,
        out_specs=pl.BlockSpec((tq, width), lambda b, t: (blk0 + b * per_seq + t, 0)),
        out_shape=jax.ShapeDtypeStruct((q_all.shape[0], width), BF16),
        input_output_aliases=aliases,
        compiler_params=_params("parallel", "parallel"),
    )(*operands)


def kernel(x_prompt, x_sample, state_ssm, state_ssm_conv, state_pool, cache_diff_k, cache_diff_v, cache_mem_k, cache_mem_v, state_ffn_conv, mem_prompt, norm_mix_pre, norm_mix_post, w_in, ssm_conv_w, ssm_conv_b, ssm_dt_bias, ssm_a_log, ssm_d, ssm_norm, w_br_ssm, pool_w, pool_scale, w_br_pool, diff_lq1, diff_lk1, diff_lq2, diff_lk2, diff_subln, w_br_diff, w_o, norm_xa_pre, norm_xa_post, norm_mem, xa_wq, xa_wk, xa_wv, xa_wo, norm_ffn_pre, norm_ffn_post, ffn_w_up, ffn_conv_w, ffn_conv_b, ffn_w_down):
    bp, lp, dm = x_prompt.shape
    bs, ls, _ = x_sample.shape
    depth = w_in.shape[0]
    tp, ts = bp * lp, bs * ls
    heads, p_dim, n_state = state_ssm.shape[2:]
    d_inner = heads * p_dim
    xbc_w = state_ssm_conv.shape[3]
    groups = (xbc_w - d_inner) // (2 * n_state)
    k_ssm = ssm_conv_w.shape[1]
    pool_hist, pool_width = state_pool.shape[2:]
    past, dheads, _, dh = cache_diff_k.shape[2:]
    diff_w = dheads * 2 * dh
    n_mem, xheads, xhd = cache_mem_k.shape[2:]
    xa_w = xheads * xhd
    d_ff = ffn_w_down.shape[1]
    k_ffn = ffn_conv_w.shape[1]
    ffp = _round_up(d_ff, 1024)
    assert lp >= pool_hist and ls >= pool_hist and min(lp, ls) >= max(k_ssm, k_ffn) - 1

    o_z, o_xbc, o_dt = 0, d_inner, d_inner + xbc_w
    o_u = o_dt + heads
    o_q = o_u + pool_width
    o_g = o_q + 3 * diff_w

    def seg(off, width):
        return w_in[:, :, off:off + width]

    w_main = jnp.concatenate(
        [seg(o_g, 3 * dm), seg(o_xbc, xbc_w), seg(o_q, 3 * diff_w), seg(o_z, d_inner), seg(o_u, pool_width)],
        axis=-1).astype(BF16)
    c_g, c_xbc = 0, 3 * dm
    c_q = c_xbc + xbc_w
    c_k, c_v = c_q + diff_w, c_q + 2 * diff_w
    c_z = c_q + 3 * diff_w
    c_u = c_z + d_inner
    dt_pad = _round_up(heads, LANES)
    w_dt = jnp.pad(seg(o_dt, heads), ((0, 0), (0, 0), (0, dt_pad - heads))).astype(BF16)
    w_bs, w_bp, w_bd = w_br_ssm.astype(BF16), w_br_pool.astype(BF16), w_br_diff.astype(BF16)
    w_out = w_o.astype(BF16)
    pool_wb = pool_w.astype(BF16)
    w_q = xa_wq.astype(BF16)
    w_kv = jnp.concatenate([xa_wk, xa_wv], axis=-1).astype(BF16)
    w_xo = xa_wo.astype(BF16)

    def pad_ff(a):
        pad = [(0, 0)] * (a.ndim - 1) + [(0, ffp - d_ff)]
        return jnp.concatenate([jnp.pad(a[..., :d_ff], pad), jnp.pad(a[..., d_ff:], pad)], axis=-1)

    w_up = cast_pad_halves(ffn_w_up, d_ff, ffp)
    conv_w_ff = pad_ff(ffn_conv_w)
    conv_b_ff = pad_ff(ffn_conv_b)
    ffn_hist = pad_ff(state_ffn_conv)
    w_down = ffn_w_down.astype(BF16)
    if d_ff % LANES:
        w_down = jnp.pad(w_down, ((0, 0), (0, ffp - d_ff), (0, 0)))

    k_cache = cache_diff_k.reshape(depth, bs, past * dheads * 2, dh)
    v_cache = cache_diff_v.reshape(depth, bs, past, diff_w)
    mem_k = cache_mem_k.reshape(depth, bs, n_mem, xa_w)
    mem_v = cache_mem_v.reshape(depth, bs, n_mem, xa_w)
    mem_rows = mem_prompt.reshape(bp * n_mem, dm)

    h = jnp.concatenate([x_prompt.reshape(tp, dm), x_sample.reshape(ts, dm)], axis=0)
    xn = rmsnorm_bf16(h, norm_mix_pre[0])
    zeros_p = lambda *shape: jnp.zeros((bp,) + shape, F32)
    q_ssd = _tile(lp, 128, LANES)
    nsub_ssd = _tile(lp // q_ssd, 16, 1)

    def tails(arr, col0, width, n):
        a_p = jnp.stack([arr[(b + 1) * lp - n:(b + 1) * lp, col0:col0 + width] for b in range(bp)])
        a_s = arr[tp:, col0:col0 + width].reshape(bs, ls, width)[:, ls - n:]
        return a_p, a_s

    def both(arr, col0, width):
        return arr[:tp, col0:col0 + width], arr[tp:, col0:col0 + width]

    outs = [[] for _ in range(14)]
    for l in range(depth):
        lam_init = 0.8 - 0.6 * math.exp(-0.3 * l)
        lam_vecs = (diff_lq1[l], diff_lk1[l], diff_lq2[l], diff_lk2[l])

        proj = mm(xn, w_main, l)
        dt_raw = mm(xn, w_dt, l)[:, :heads]
        y_ssm = y_pool = None
        h_last = []
        for (row0, nb, sl, conv_hist, h0, p_hist, pos_base, q_chunk, nsub) in (
                (0, bp, lp, zeros_p(k_ssm - 1, xbc_w), zeros_p(heads, p_dim, n_state),
                 zeros_p(pool_hist, pool_width), 0, q_ssd, nsub_ssd),
                (tp, bs, ls, state_ssm_conv[l], state_ssm[l], state_pool[l], past, ls, 1)):
            dt_g = dt_raw[row0:row0 + nb * sl].reshape(nb, sl, heads)
            y_ssm, hl = ssd_mix(proj, c_xbc, c_z, row0, nb, sl, conv_hist, ssm_conv_w[l], ssm_conv_b[l], dt_g,
                                ssm_dt_bias[l], ssm_a_log[l], ssm_d[l], ssm_norm[l], h0, y_ssm,
                                groups=groups, n_state=n_state, q=q_chunk, nsub=nsub)
            h_last.append(hl)
            y_pool = pool_mix(proj, c_u, row0, nb, sl, p_hist, pos_base, pool_wb[l], pool_scale[l], y_pool)
        y_diff = diff_attn_prompt(proj, c_q, c_k, c_v, bp, lp, dheads, dh, lam_vecs, diff_subln[l], lam_init)
        y_diff = diff_attn_sample(proj, c_q, c_k, c_v, tp, bs, ls, dheads, dh, k_cache, v_cache, l,
                                  lam_vecs, diff_subln[l], lam_init, y_diff)
        merged = branch_merge(y_ssm, y_pool, y_diff, proj, c_g, w_bs, w_bp, w_bd, l)
        h, xn = post_norm(mm(merged, w_out, l), h, norm_mix_post[l], norm_xa_pre[l])

        mem_kv = mm(rmsnorm_bf16(mem_rows, norm_mem[l]), w_kv, l)
        q_xa = mm(xn, w_q, l)
        kv_p = [pl.BlockSpec((n_mem, xa_w), lambda b, t: (b, 0)), pl.BlockSpec((n_mem, xa_w), lambda b, t: (b, 1))]
        kv_s = [pl.BlockSpec((None, None, n_mem, xa_w), lambda b, t, l=l: (l, b, 0, 0))] * 2
        o_xa = cross_attn(q_xa, 0, bp, lp, mem_kv, mem_kv, kv_p, xheads, xhd, None)
        o_xa = cross_attn(q_xa, tp, bs, ls, mem_k, mem_v, kv_s, xheads, xhd, o_xa)
        h, xn = mm_post_norm(o_xa, w_xo, l, h, norm_xa_post[l], norm_ffn_pre[l])

        hid, tail_g, tail_v = ffn_up_geglu(xn, bp, lp, w_up, l, ffp, zeros_p(k_ffn - 1, 2 * ffp),
                                           conv_w_ff[l], conv_b_ff[l])
        up_s = mm(xn, w_up, l, row0=tp, tm_pref=512)
        hid = conv_geglu(up_s, ffp, bs, ls, ffn_hist[l], conv_w_ff[l], conv_b_ff[l], hid, tp)
        down = mm(hid, w_down, l, tm_pref=512, tn_pref=512)
        if l + 1 < depth:
            h, xn = post_norm(down, h, norm_ffn_post[l], norm_mix_pre[l + 1])
        else:
            h_p = post_norm(down, h, norm_ffn_post[l], None, nrows=tp)
            h_s = post_norm(down, h, norm_ffn_post[l], None, row0=tp)

        conv_p, conv_s = tails(proj, c_xbc, xbc_w, k_ssm - 1)
        pool_p, pool_s = tails(proj, c_u, pool_width, pool_hist)
        kk_p, kk_s = both(proj, c_k, diff_w)
        vv_p, vv_s = both(proj, c_v, diff_w)
        ups = up_s.reshape(bs, ls, 2 * ffp)[:, ls - (k_ffn - 1):]
        ffn_p = jnp.concatenate([tail_g[:, -(k_ffn - 1):, :d_ff], tail_v[:, -(k_ffn - 1):, :d_ff]], axis=-1)
        ffn_s = jnp.concatenate([ups[..., :d_ff], ups[..., ffp:ffp + d_ff]], axis=-1)
        layer_out = (h_last[0], h_last[1], conv_p, conv_s, pool_p, pool_s,
                     kk_p.reshape(bp, lp, dheads, 2, dh), kk_s.reshape(bs, ls, dheads, 2, dh),
                     vv_p.reshape(bp, lp, dheads, 2 * dh), vv_s.reshape(bs, ls, dheads, 2 * dh),
                     mem_kv[:, :xa_w].reshape(bp, n_mem, xheads, xhd),
                     mem_kv[:, xa_w:].reshape(bp, n_mem, xheads, xhd),
                     ffn_p, ffn_s)
        for acc, val in zip(outs, layer_out):
            acc.append(val)

    return (h_p.reshape(bp, lp, dm), h_s.reshape(bs, ls, dm)) + tuple(jnp.stack(o) for o in outs)
```

```python
import functools
import math

import jax
import jax.numpy as jnp
from jax import lax
from jax.experimental import pallas as pl
from jax.experimental.pallas import tpu as pltpu

F32 = jnp.float32
BF16 = jnp.bfloat16
EPS = 1e-6
CHUNK = 64
POOL_WINDOWS = (2, 4, 8, 16)
SUBLANES = 8
PACKED_ROWS = 16
LANES = 128
VMEM_LIMIT_BYTES = 56 * 1024 * 1024
NT_DIMS = (((1,), (1,)), ((), ()))
TN_DIMS = (((0,), (0,)), ((), ()))
SINGLE = pl.Buffered(1)
FFN_ROW_CHUNK = 64
NORM_UNROLL = 8


def _tile(n, pref, align):
    t = (min(pref, n) // align) * align
    while t >= align:
        if n % t == 0:
            return t
        t -= align
    return n


def _round_up(n, m):
    return (n + m - 1) // m * m


def _params(*sem):
    return pltpu.CompilerParams(dimension_semantics=sem, vmem_limit_bytes=VMEM_LIMIT_BYTES)


def _sigmoid(x):
    return 1.0 / (1.0 + jnp.exp(-x))


def _softplus(x):
    return jnp.maximum(x, 0.0) + jnp.log(1.0 + jnp.exp(-jnp.abs(x)))


def _gelu_tanh(x):
    return 0.5 * x * (1.0 + jnp.tanh(math.sqrt(2.0 / math.pi) * (x + 0.044715 * (x * x * x))))


def _rms(x, g):
    return x * lax.rsqrt(jnp.mean(x * x, axis=-1, keepdims=True) + EPS) * g


def _dot_sel(x, sel, sel_first=False):
    hi = x.astype(BF16)
    rest = x - hi.astype(F32)
    mid = rest.astype(BF16)
    lo = (rest - mid.astype(F32)).astype(BF16)
    sel = sel.astype(BF16)
    dots = [jnp.dot(sel, t, preferred_element_type=F32) if sel_first else
            jnp.dot(t, sel, preferred_element_type=F32) for t in (hi, mid, lo)]
    return dots[0] + dots[1] + dots[2]


def _drop_ref(body, idx):
    def wrapped(*refs):
        return body(*refs[:idx], *refs[idx + 1:])
    return wrapped


def _into(joint, in_specs, operands):
    if joint is None:
        return in_specs, operands, {}
    return (in_specs + [pl.BlockSpec(memory_space=pl.ANY)], operands + [joint], {len(operands): 0})


def _norm_kernel(x_ref, g_ref, o_ref, *, rows):
    def body(r, carry):
        sl = pl.ds(pl.multiple_of(r * rows, rows), rows)
        o_ref[sl, :] = _rms(x_ref[sl, :], g_ref[...]).astype(BF16)
        return carry
    lax.fori_loop(0, x_ref.shape[0] // rows, body, 0, unroll=NORM_UNROLL)


def rmsnorm_bf16(x, g, *, tm_pref=256):
    t, d = x.shape
    tm = _tile(t, tm_pref, PACKED_ROWS)
    return pl.pallas_call(
        functools.partial(_norm_kernel, rows=PACKED_ROWS),
        grid=(t // tm,),
        in_specs=[pl.BlockSpec((tm, d), lambda i: (i, 0)), pl.BlockSpec((1, d), lambda i: (0, 0))],
        out_specs=pl.BlockSpec((tm, d), lambda i: (i, 0)),
        out_shape=jax.ShapeDtypeStruct((t, d), BF16),
        compiler_params=_params("parallel"),
    )(x, g.reshape(1, d))


def _mm_kernel(x_ref, w_ref, o_ref):
    o_ref[...] = jnp.dot(x_ref[...], w_ref[...], preferred_element_type=F32)


def mm(x, w_all, layer, *, row0=0, nrows=None, tm_pref=768, tn_pref=1024):
    k, n = w_all.shape[1:]
    assert k == x.shape[1] or (k < x.shape[1] and k % LANES == 0)
    nrows = x.shape[0] - row0 if nrows is None else nrows
    tm = _tile(math.gcd(nrows, row0) if row0 else nrows, tm_pref, PACKED_ROWS)
    tn = _tile(n, tn_pref, LANES)
    rb = row0 // tm
    return pl.pallas_call(
        _mm_kernel,
        grid=(nrows // tm, n // tn),
        in_specs=[pl.BlockSpec((tm, k), lambda i, j: (rb + i, 0)),
                  pl.BlockSpec((None, k, tn), lambda i, j: (layer, 0, j))],
        out_specs=pl.BlockSpec((tm, tn), lambda i, j: (i, j)),
        out_shape=jax.ShapeDtypeStruct((nrows, n), F32),
        compiler_params=_params("parallel", "parallel"),
    )(x, w_all)


def _cast_pad_kernel(x_ref, o_ref, *, real_blocks, blocks):
    is_real = pl.program_id(1) % blocks < real_blocks
    o_ref[...] = jnp.where(is_real, x_ref[...], 0.0).astype(BF16)


def cast_pad_halves(w, half, half_pad, *, blk_pref=512):
    depth, k, _ = w.shape
    blk = _tile(math.gcd(half, half_pad), blk_pref, LANES)
    if blk % LANES:
        pad = ((0, 0), (0, 0), (0, half_pad - half))
        return jnp.concatenate([jnp.pad(w[..., :half], pad), jnp.pad(w[..., half:], pad)], -1).astype(BF16)
    real_blocks, blocks = half // blk, half_pad // blk

    def src(l, j):
        return (l, 0, (j // blocks) * real_blocks + jnp.minimum(j % blocks, real_blocks - 1))

    return pl.pallas_call(
        functools.partial(_cast_pad_kernel, real_blocks=real_blocks, blocks=blocks),
        grid=(depth, 2 * blocks),
        in_specs=[pl.BlockSpec((None, k, blk), src)],
        out_specs=pl.BlockSpec((None, k, blk), lambda l, j: (l, 0, j)),
        out_shape=jax.ShapeDtypeStruct((depth, k, 2 * half_pad), BF16),
        compiler_params=_params("parallel", "parallel"),
    )(w)


def _post_norm_kernel(y_ref, res_ref, g_ref, gn_ref, h_ref, *maybe_xn_ref, rows):
    def body(r, carry):
        sl = pl.ds(pl.multiple_of(r * rows, rows), rows)
        h_new = res_ref[sl, :] + _rms(y_ref[sl, :], g_ref[...])
        h_ref[sl, :] = h_new
        for xn_ref in maybe_xn_ref:
            xn_ref[sl, :] = _rms(h_new, gn_ref[...]).astype(BF16)
        return carry
    lax.fori_loop(0, y_ref.shape[0] // rows, body, 0, unroll=NORM_UNROLL)


def _mm_post_norm_kernel(x_ref, w_ref, res_ref, g_ref, gn_ref, h_ref, *rest, rows):
    *maybe_xn_ref, y_ref = rest
    y_ref[...] = jnp.dot(x_ref[...], w_ref[...], preferred_element_type=F32)
    _post_norm_kernel(y_ref, res_ref, g_ref, gn_ref, h_ref, *maybe_xn_ref, rows=rows)


def mm_post_norm(x, w_all, layer, res, g, g_next, *, tm_pref=256):
    t, k = x.shape
    d = w_all.shape[2]
    tm = _tile(t, tm_pref, PACKED_ROWS)
    row_spec = pl.BlockSpec((tm, d), lambda i: (i, 0))
    vec_spec = pl.BlockSpec((1, d), lambda i: (0, 0))
    emit = g_next is not None
    out = pl.pallas_call(
        functools.partial(_mm_post_norm_kernel, rows=PACKED_ROWS),
        grid=(t // tm,),
        in_specs=[pl.BlockSpec((tm, k), lambda i: (i, 0)),
                  pl.BlockSpec((None, k, d), lambda i: (layer, 0, 0), pipeline_mode=SINGLE),
                  row_spec, vec_spec, vec_spec],
        out_specs=[row_spec, row_spec] if emit else [row_spec],
        out_shape=[jax.ShapeDtypeStruct((t, d), F32)] + ([jax.ShapeDtypeStruct((t, d), BF16)] if emit else []),
        scratch_shapes=[pltpu.VMEM((tm, d), F32)],
        compiler_params=_params("parallel"),
    )(x, w_all, res, g.reshape(1, d), (g_next if emit else g).reshape(1, d))
    return tuple(out) if emit else out[0]


def post_norm(y, res, g, g_next, *, row0=0, nrows=None, tm_pref=256):
    d = y.shape[1]
    t = y.shape[0] - row0 if nrows is None else nrows
    tm = _tile(math.gcd(t, row0) if row0 else t, tm_pref, PACKED_ROWS)
    rb = row0 // tm
    in_row_spec = pl.BlockSpec((tm, d), lambda i: (rb + i, 0))
    row_spec = pl.BlockSpec((tm, d), lambda i: (i, 0))
    vec_spec = pl.BlockSpec((1, d), lambda i: (0, 0))
    emit = g_next is not None
    out = pl.pallas_call(
        functools.partial(_post_norm_kernel, rows=PACKED_ROWS),
        grid=(t // tm,),
        in_specs=[in_row_spec, in_row_spec, vec_spec, vec_spec],
        out_specs=[row_spec, row_spec] if emit else [row_spec],
        out_shape=[jax.ShapeDtypeStruct((t, d), F32)] + ([jax.ShapeDtypeStruct((t, d), BF16)] if emit else []),
        compiler_params=_params("parallel"),
    )(y, res, g.reshape(1, d), (g_next if emit else g).reshape(1, d))
    return tuple(out) if emit else out[0]


def _merge_kernel(ys_ref, yp_ref, yd_ref, g0_ref, g1_ref, g2_ref, ws_ref, wp_ref, wd_ref, o_ref):
    acc = _sigmoid(g0_ref[...]) * jnp.dot(ys_ref[...], ws_ref[...], preferred_element_type=F32)
    acc += _sigmoid(g1_ref[...]) * jnp.dot(yp_ref[...], wp_ref[...], preferred_element_type=F32)
    acc += _sigmoid(g2_ref[...]) * jnp.dot(yd_ref[...], wd_ref[...], preferred_element_type=F32)
    o_ref[...] = acc.astype(BF16)


def branch_merge(y_ssm, y_pool, y_diff, proj, gate_col0, w_s, w_p, w_d, layer, *, tm_pref=768, tn_pref=1024):
    t = y_ssm.shape[0]
    d = w_s.shape[2]
    tm = _tile(t, tm_pref, PACKED_ROWS)
    tn = _tile(d, tn_pref, LANES)
    assert gate_col0 % tn == 0
    nj = d // tn
    gj = gate_col0 // tn

    def gate_spec(br):
        return pl.BlockSpec((tm, tn), lambda i, j: (i, gj + br * nj + j))

    def x_spec(kdim):
        return pl.BlockSpec((tm, kdim), lambda i, j: (i, 0))

    def w_spec(kdim):
        return pl.BlockSpec((None, kdim, tn), lambda i, j: (layer, 0, j))

    return pl.pallas_call(
        _merge_kernel,
        grid=(t // tm, nj),
        in_specs=[x_spec(y_ssm.shape[1]), x_spec(y_pool.shape[1]), x_spec(y_diff.shape[1]),
                  gate_spec(0), gate_spec(1), gate_spec(2),
                  w_spec(w_s.shape[1]), w_spec(w_p.shape[1]), w_spec(w_d.shape[1])],
        out_specs=pl.BlockSpec((tm, tn), lambda i, j: (i, j)),
        out_shape=jax.ShapeDtypeStruct((t, d), BF16),
        compiler_params=_params("parallel", "parallel"),
    )(y_ssm, y_pool, y_diff, proj, proj, proj, w_s, w_p, w_d)


def _conv_rows(ext_ref, u, halo, w_ref, b_ref, k):
    n = u.shape[0]
    ext_ref[0:SUBLANES, :] = halo
    ext_ref[SUBLANES:SUBLANES + n, :] = u
    acc = u * w_ref[k - 1:k, :]
    for i in range(k - 1):
        acc = acc + ext_ref[pl.ds(SUBLANES - (k - 1) + i, n), :] * w_ref[i:i + 1, :]
    return acc + b_ref[...]


def _halo(prev_ref, hist_ref, s, nseq):
    if nseq > 1:
        return hist_ref[s]
    return jnp.where(pl.program_id(1) == 0, hist_ref[0], prev_ref[...])


def _seq_tiling(nb, seq_len, tl_pref, align):
    if seq_len >= tl_pref:
        return _tile(seq_len, tl_pref, align), 1
    nseq = _tile(nb, max(tl_pref // seq_len, 1), 1)
    return nseq * seq_len, nseq


def _seq_specs(row0, rows_per_b, tl, tc, col_blk0):
    blk0 = row0 // tl
    per_b = rows_per_b // tl
    sub0 = row0 // SUBLANES
    sub_per_b = rows_per_b // SUBLANES
    sub_per_tile = tl // SUBLANES
    cur = pl.BlockSpec((tl, tc), lambda b, t, c: (blk0 + b * per_b + t, col_blk0 + c))
    prev = pl.BlockSpec(
        (SUBLANES, tc),
        lambda b, t, c: (jnp.maximum(sub0 + b * sub_per_b + t * sub_per_tile - 1, 0), col_blk0 + c))
    return cur, prev


def _pad_hist(hist, rows):
    return jnp.pad(hist, ((0, 0), (rows - hist.shape[1], 0), (0, 0)))


def _conv_geglu_kernel(ug_ref, pg_ref, hg_ref, wg_ref, bg_ref, uv_ref, pv_ref, hv_ref, wv_ref, bv_ref,
                       o_ref, extg_ref, extv_ref, *, k, sl, nseq):
    for s in range(nseq):
        rows = slice(s * sl, (s + 1) * sl)
        gate = _conv_rows(extg_ref, ug_ref[rows, :], _halo(pg_ref, hg_ref, s, nseq), wg_ref, bg_ref, k)
        val = _conv_rows(extv_ref, uv_ref[rows, :], _halo(pv_ref, hv_ref, s, nseq), wv_ref, bv_ref, k)
        o_ref[rows, :] = (_gelu_tanh(gate) * val).astype(BF16)


def conv_geglu(up, half, nb, seq_len, hist, w, bias, joint, out_row0, *, tl_pref=512, tc_pref=1024):
    k = w.shape[0]
    tl, nseq = _seq_tiling(nb, seq_len, tl_pref, PACKED_ROWS)
    tc = _tile(half, tc_pref, LANES)
    rows_per_b = max(tl, seq_len)
    per_b = rows_per_b // tl
    ncb = half // tc
    assert out_row0 % tl == 0
    ob = out_row0 // tl
    cur_g, prev_g = _seq_specs(0, rows_per_b, tl, tc, 0)
    cur_v, prev_v = _seq_specs(0, rows_per_b, tl, tc, ncb)
    hist8 = _pad_hist(hist, SUBLANES)
    bias2 = bias.reshape(1, 2 * half)

    def side(off):
        return [pl.BlockSpec((nseq, SUBLANES, tc), lambda b, t, c: (b, 0, off + c)),
                pl.BlockSpec((k, tc), lambda b, t, c: (0, off + c)),
                pl.BlockSpec((1, tc), lambda b, t, c: (0, off + c))]

    in_specs, operands, aliases = _into(
        joint, [cur_g, prev_g] + side(0) + [cur_v, prev_v] + side(ncb),
        [up, up, hist8, w, bias2, up, up, hist8, w, bias2])
    body = functools.partial(_conv_geglu_kernel, k=k, sl=tl // nseq, nseq=nseq)
    return pl.pallas_call(
        _drop_ref(body, 10) if aliases else body,
        grid=(nb // nseq, per_b, ncb),
        in_specs=in_specs,
        out_specs=pl.BlockSpec((tl, tc), lambda b, t, c: (ob + b * per_b + t, c)),
        out_shape=jax.ShapeDtypeStruct(joint.shape, BF16),
        input_output_aliases=aliases,
        scratch_shapes=[pltpu.VMEM((SUBLANES + tl // nseq, tc), F32)] * 2,
        compiler_params=_params("parallel", "parallel", "parallel"),
    )(*operands)


def _ffn_up_kernel(xc_ref, xp_ref, wg_ref, wv_ref, hg_ref, hv_ref, cwg_ref, cbg_ref, cwv_ref, cbv_ref,
                   hid_ref, tg_ref, tv_ref, xe_ref, eg_ref, ev_ref, gg_ref, *, k, tm, tiles_per_seq):
    halo = PACKED_ROWS

    @pl.when(pl.program_id(1) == 0)
    def _():
        xe_ref[0:halo, :] = xp_ref[...]
        xe_ref[halo:halo + tm, :] = xc_ref[...]

    first = pl.program_id(0) % tiles_per_seq == 0

    def project(w_ref, h_ref, e_ref, t_ref):
        e_ref[...] = jnp.dot(xe_ref[...], w_ref[...], preferred_element_type=F32)
        e_ref[0:halo, :] = jnp.where(first, h_ref[...], e_ref[0:halo, :])
        t_ref[...] = e_ref[tm:tm + halo, :]

    def conv(e_ref, cw_ref, cb_ref, r0, n):
        acc = e_ref[halo + r0:halo + r0 + n, :] * cw_ref[k - 1:k, :]
        for i in range(k - 1):
            acc = acc + e_ref[pl.ds(halo + r0 - (k - 1) + i, n), :] * cw_ref[i:i + 1, :]
        return acc + cb_ref[...]

    chunk = _tile(tm, FFN_ROW_CHUNK, PACKED_ROWS)
    project(wg_ref, hg_ref, eg_ref, tg_ref)
    for r0 in range(0, tm, chunk):
        gg_ref[r0:r0 + chunk, :] = _gelu_tanh(conv(eg_ref, cwg_ref, cbg_ref, r0, chunk))
    project(wv_ref, hv_ref, ev_ref, tv_ref)
    for r0 in range(0, tm, chunk):
        val = conv(ev_ref, cwv_ref, cbv_ref, r0, chunk)
        hid_ref[r0:r0 + chunk, :] = (gg_ref[r0:r0 + chunk, :] * val).astype(BF16)


def ffn_up_geglu(xn, nb, seq_len, w_up_all, layer, half, hist, conv_w, conv_b, *, tm_pref=1024, tf_pref=512):
    t, d = xn.shape
    k = conv_w.shape[0]
    halo = PACKED_ROWS
    tm = _tile(seq_len, tm_pref, halo)
    tf = _tile(half, tf_pref, LANES)
    tiles_per_seq = seq_len // tm
    ncb = half // tf
    hist16 = _pad_hist(hist, halo)
    bias2 = conv_b.reshape(1, 2 * half)

    def w_spec(off):
        return pl.BlockSpec((None, d, tf), lambda i, j: (layer, 0, off + j))

    def hist_spec(off):
        return pl.BlockSpec((None, halo, tf), lambda i, j: (i // tiles_per_seq, 0, off + j))

    def cw_spec(off):
        return pl.BlockSpec((k, tf), lambda i, j: (0, off + j))

    def cb_spec(off):
        return pl.BlockSpec((1, tf), lambda i, j: (0, off + j))

    tail_spec = pl.BlockSpec((None, halo, tf), lambda i, j: (i, 0, j))
    tail_shape = jax.ShapeDtypeStruct((nb * tiles_per_seq, halo, half), F32)
    hid, tail_g, tail_v = pl.pallas_call(
        functools.partial(_ffn_up_kernel, k=k, tm=tm, tiles_per_seq=tiles_per_seq),
        grid=(nb * tiles_per_seq, ncb),
        in_specs=[pl.BlockSpec((tm, d), lambda i, j: (i, 0), pipeline_mode=SINGLE),
                  pl.BlockSpec((halo, d), lambda i, j: (jnp.maximum(i * (tm // halo) - 1, 0), 0)),
                  w_spec(0), w_spec(ncb), hist_spec(0), hist_spec(ncb),
                  cw_spec(0), cb_spec(0), cw_spec(ncb), cb_spec(ncb)],
        out_specs=[pl.BlockSpec((tm, tf), lambda i, j: (i, j)), tail_spec, tail_spec],
        out_shape=[jax.ShapeDtypeStruct((t, half), BF16), tail_shape, tail_shape],
        scratch_shapes=[pltpu.VMEM((halo + tm, d), BF16), pltpu.VMEM((halo + tm, tf), F32),
                        pltpu.VMEM((halo + tm, tf), F32), pltpu.VMEM((tm, tf), F32)],
        compiler_params=_params("parallel", "arbitrary"),
    )(xn, xn, w_up_all, w_up_all, hist16, hist16, conv_w, bias2, conv_w, bias2)
    last = slice(tiles_per_seq - 1, None, tiles_per_seq)
    return hid, tail_g[last], tail_v[last]


def _pool_kernel(u_ref, prev_ref, hist_ref, pw_ref, ps_ref, o_ref, ext_ref, *, tl, halo, pos_base, gd):
    t = pl.program_id(1)
    ext_ref[0:halo, :] = jnp.where(t == 0, hist_ref[...], prev_ref[...])
    ext_ref[halo:halo + tl, :] = u_ref[...]
    pos = pos_base + t * tl + lax.broadcasted_iota(jnp.int32, (tl, 1), 0)
    for gi, win in enumerate(POOL_WINDOWS):
        cols = slice(gi * gd, (gi + 1) * gd)
        cur = u_ref[:, cols]
        wsum = cur
        for i in range(1, win):
            wsum = wsum + ext_ref[pl.ds(halo - i, tl), cols]
        cnt = jnp.minimum(pos + 1, win).astype(F32)
        pooled = wsum / cnt - cur
        y = jnp.dot(pooled.astype(BF16), pw_ref[gi], preferred_element_type=F32) * ps_ref[:, cols]
        o_ref[:, cols] = y.astype(BF16)


def pool_mix(proj, col0, row0, nb, seq_len, hist, pos_base, pool_w, pool_scale, joint, *, tl_pref=512):
    ng, gd, _ = pool_w.shape
    width = ng * gd
    halo = 2 * SUBLANES
    assert ng == len(POOL_WINDOWS) and max(POOL_WINDOWS) <= halo
    tl = _tile(seq_len, tl_pref, halo)
    assert row0 % tl == 0 and col0 % width == 0
    blk0, per_seq = row0 // tl, seq_len // tl
    h0, h_per_seq, h_per_tile = row0 // halo, seq_len // halo, tl // halo
    cb = col0 // width
    in_specs, operands, aliases = _into(
        joint,
        [pl.BlockSpec((tl, width), lambda b, t: (blk0 + b * per_seq + t, cb)),
         pl.BlockSpec((halo, width),
                      lambda b, t: (jnp.maximum(h0 + b * h_per_seq + t * h_per_tile - 1, 0), cb)),
         pl.BlockSpec((None, halo, width), lambda b, t: (b, 0, 0)),
         pl.BlockSpec((ng, gd, gd), lambda b, t: (0, 0, 0)),
         pl.BlockSpec((1, width), lambda b, t: (0, 0))],
        [proj, proj, _pad_hist(hist, halo), pool_w, pool_scale.reshape(1, width)])
    body = functools.partial(_pool_kernel, tl=tl, halo=halo, pos_base=pos_base, gd=gd)
    return pl.pallas_call(
        _drop_ref(body, 5) if aliases else body,
        grid=(nb, per_seq),
        in_specs=in_specs,
        out_specs=pl.BlockSpec((tl, width), lambda b, t: (blk0 + b * per_seq + t, 0)),
        out_shape=jax.ShapeDtypeStruct((proj.shape[0], width), BF16),
        input_output_aliases=aliases,
        scratch_shapes=[pltpu.VMEM((halo + tl, width), F32)],
        compiler_params=_params("parallel", "parallel"),
    )(*operands)


def _ssd_kernel(*refs, q, nsub, hpg, p, k):
    conv_in, refs = refs[:15], refs[15:]
    (z_ref, dtc_ref, dtr_ref, biasc_ref, biasr_ref, alogc_ref, alogr_ref, dskip_ref, nw_ref, h0_ref,
     y_ref, hl_ref, ht_ref, xs_ref, b_ref, c_ref, ext_x_ref, ext_b_ref, ext_c_ref) = refs
    ti = pl.program_id(2)
    gw = hpg * p
    n = b_ref.shape[1]
    for i, (dst_ref, ext_ref) in enumerate(((xs_ref, ext_x_ref), (b_ref, ext_b_ref), (c_ref, ext_c_ref))):
        cur_ref, prev_ref, hist_ref, cw_ref, cb_ref = conv_in[5 * i:5 * i + 5]
        halo = jnp.where(ti == 0, hist_ref[...], prev_ref[...])
        u = _conv_rows(ext_ref, cur_ref[...], halo, cw_ref, cb_ref, k)
        dst_ref[...] = u * _sigmoid(u)

    @pl.when(ti == 0)
    def _():
        ht_ref[...] = h0_ref[...].reshape(gw, n).T

    dt_c_all = _softplus(dtc_ref[...] + biasc_ref[...])
    dt_r_all = _softplus(dtr_ref[...] + biasr_ref[...])
    a_c = -jnp.exp(alogc_ref[...])
    a_r = -jnp.exp(alogr_ref[...])
    row = lax.broadcasted_iota(jnp.int32, (q, q), 0)
    col = lax.broadcasted_iota(jnp.int32, (q, q), 1)
    lower = row >= col
    upper = row <= col

    def spread(width, rep):
        head = lax.broadcasted_iota(jnp.int32, (hpg, width), 0)
        lane = lax.broadcasted_iota(jnp.int32, (hpg, width), 1)
        return lane // rep == head

    to_lanes = spread(gw, p)
    to_keys = spread(hpg * q, q)
    lane_head = lax.broadcasted_iota(jnp.int32, (1, gw), 1) // p
    d_skip = _dot_sel(dskip_ref[...], to_lanes)
    ht = ht_ref[...]
    for s in range(nsub):
        rows = slice(s * q, (s + 1) * q)
        dt_c = dt_c_all[rows, :]
        dt_r = dt_r_all[:, rows]
        acum_c = _dot_sel(dt_c * a_c, lower, sel_first=True)
        acum_r = _dot_sel(dt_r * a_r, upper)
        a_last = acum_c[q - 1:q, :]
        per_head = jnp.concatenate([jnp.exp(acum_c), jnp.exp(a_last - acum_c) * dt_c], axis=0)
        per_lane = _dot_sel(per_head, to_lanes)
        exp_a, w_end = per_lane[:q], per_lane[q:]
        chunk_decay = _dot_sel(jnp.exp(a_last), to_lanes)
        ac_keys = _dot_sel(acum_c, to_keys)
        x = xs_ref[rows, :]
        bm = b_ref[rows, :].astype(BF16)
        cm = c_ref[rows, :].astype(BF16)
        cb = lax.dot_general(cm, bm, NT_DIMS, preferred_element_type=F32)
        wmats = []
        for r in range(hpg):
            seg = ac_keys[:, r * q:(r + 1) * q] - acum_r[r:r + 1, :]
            decay = jnp.exp(jnp.where(lower, seg, -jnp.inf))
            wmats.append((cb * decay * dt_r[r:r + 1, :]).astype(BF16))
        y_all = jnp.dot(jnp.concatenate(wmats, axis=0), x.astype(BF16), preferred_element_type=F32)
        y = y_all[:q]
        for r in range(1, hpg):
            y = jnp.where(lane_head == r, y_all[r * q:(r + 1) * q], y)
        y = y + jnp.dot(cm, ht.astype(BF16), preferred_element_type=F32) * exp_a
        s_new = lax.dot_general(bm, (x * w_end).astype(BF16), TN_DIMS, preferred_element_type=F32)
        ht = ht * chunk_decay + s_new
        zt = z_ref[rows, :]
        yg = (y + x * d_skip) * (zt * _sigmoid(zt))
        y_ref[rows, :] = _rms(yg, nw_ref[...]).astype(BF16)
    ht_ref[...] = ht

    @pl.when(ti == pl.num_programs(2) - 1)
    def _():
        hl_ref[...] = ht.T.reshape(hpg, p, n)


def ssd_mix(proj, xbc_col0, z_col0, row0, nb, seq_len, conv_hist, conv_w, conv_b, dt_raw, dt_bias, a_log,
            d_skip, norm_w, h0, joint, *, groups, n_state, q, nsub):
    heads, p = h0.shape[1], h0.shape[2]
    k = conv_w.shape[0]
    hpg = heads // groups
    gw = hpg * p
    d_inner = heads * p
    xbc_w = d_inner + 2 * groups * n_state
    tl = q * nsub
    nt = seq_len // tl
    assert seq_len % tl == 0 and row0 % tl == 0 and z_col0 % gw == 0
    assert xbc_col0 % gw == 0 and (xbc_col0 + d_inner) % n_state == 0 and d_inner % n_state == 0
    dt_g = dt_raw.reshape(nb, seq_len, groups, hpg)
    dt_col = dt_g.transpose(0, 2, 1, 3)
    dt_row = dt_g.transpose(0, 2, 3, 1)
    blk0 = row0 // tl
    sub0, sub_per_seq, sub_per_tile = row0 // SUBLANES, seq_len // SUBLANES, tl // SUBLANES
    zc = z_col0 // gw
    hist8 = _pad_hist(conv_hist, SUBLANES)
    bias2 = conv_b.reshape(1, xbc_w)

    def conv_inputs(width, col_of_group):
        src = xbc_col0 // width
        specs = [pl.BlockSpec((tl, width), lambda b, g, t: (blk0 + b * nt + t, src + col_of_group(g))),
                 pl.BlockSpec((SUBLANES, width),
                              lambda b, g, t: (jnp.maximum(sub0 + b * sub_per_seq + t * sub_per_tile - 1, 0),
                                               src + col_of_group(g))),
                 pl.BlockSpec((None, SUBLANES, width), lambda b, g, t: (b, 0, col_of_group(g))),
                 pl.BlockSpec((k, width), lambda b, g, t: (0, col_of_group(g))),
                 pl.BlockSpec((1, width), lambda b, g, t: (0, col_of_group(g)))]
        return specs, [proj, proj, hist8, conv_w, bias2]

    bcol = d_inner // n_state
    conv_specs, conv_ops = [], []
    for width, col_of_group in ((gw, lambda g: g), (n_state, lambda g: bcol + g),
                                (n_state, lambda g: bcol + groups + g)):
        specs, ops = conv_inputs(width, col_of_group)
        conv_specs += specs
        conv_ops += ops

    def per_group(arr, shape):
        return arr.reshape((groups,) + shape), pl.BlockSpec((None,) + shape, lambda b, g, t: (g, 0, 0))

    biasc, biasc_spec = per_group(dt_bias, (1, hpg))
    biasr, biasr_spec = per_group(dt_bias, (hpg, 1))
    alogc, alogc_spec = per_group(a_log, (1, hpg))
    alogr, alogr_spec = per_group(a_log, (hpg, 1))
    dsk, dsk_spec = per_group(d_skip, (1, hpg))
    h_spec = pl.BlockSpec((None, hpg, p, n_state), lambda b, g, t: (b, g, 0, 0))
    in_specs, operands, aliases = _into(
        joint,
        conv_specs +
        [pl.BlockSpec((tl, gw), lambda b, g, t: (blk0 + b * nt + t, zc + g)),
         pl.BlockSpec((None, None, tl, hpg), lambda b, g, t: (b, g, t, 0)),
         pl.BlockSpec((None, None, hpg, tl), lambda b, g, t: (b, g, 0, t)),
         biasc_spec, biasr_spec, alogc_spec, alogr_spec, dsk_spec,
         pl.BlockSpec((1, gw), lambda b, g, t: (0, g)),
         h_spec],
        conv_ops + [proj, dt_col, dt_row, biasc, biasr, alogc, alogr, dsk, norm_w.reshape(1, d_inner), h0])
    body = functools.partial(_ssd_kernel, q=q, nsub=nsub, hpg=hpg, p=p, k=k)
    return pl.pallas_call(
        _drop_ref(body, len(operands) - 1) if aliases else body,
        grid=(nb, groups, nt),
        in_specs=in_specs,
        out_specs=[pl.BlockSpec((tl, gw), lambda b, g, t: (blk0 + b * nt + t, g)), h_spec],
        out_shape=[jax.ShapeDtypeStruct((proj.shape[0], d_inner), BF16),
                   jax.ShapeDtypeStruct(h0.shape, F32)],
        input_output_aliases=aliases,
        scratch_shapes=[pltpu.VMEM((n_state, gw), F32),
                        pltpu.VMEM((tl, gw), F32), pltpu.VMEM((tl, n_state), F32), pltpu.VMEM((tl, n_state), F32),
                        pltpu.VMEM((SUBLANES + tl, gw), F32), pltpu.VMEM((SUBLANES + tl, n_state), F32),
                        pltpu.VMEM((SUBLANES + tl, n_state), F32)],
        compiler_params=_params("parallel", "parallel", "arbitrary"),
    )(*operands)


def _lambda(lq1_ref, lk1_ref, lq2_ref, lk2_ref, lam_init):
    s1 = jnp.sum(lq1_ref[...] * lk1_ref[...], axis=-1, keepdims=True)
    s2 = jnp.sum(lq2_ref[...] * lk2_ref[...], axis=-1, keepdims=True)
    return jnp.exp(s1) - jnp.exp(s2) + lam_init


def _loop_by_two(n, body, init):
    pairs = n // 2
    carry = lax.fori_loop(0, pairs, lambda i, c: body(2 * i + 1, body(2 * i, c)), init)
    return lax.fori_loop(2 * pairs, n, body, carry)


def _fold_lanes(a, op):
    if a.shape[1] % LANES:
        return a
    parts = [a[:, i:i + LANES] for i in range(0, a.shape[1], LANES)]
    while len(parts) > 1:
        parts = [op(parts[i], parts[i + 1]) for i in range(0, len(parts) - 1, 2)] + parts[len(parts) & ~1:]
    return parts[0]


def _diff_prompt_kernel(q_ref, k_ref, v_ref, lq1_ref, lk1_ref, lq2_ref, lk2_ref, sub_ref, o_ref,
                        s_ref, acc_ref, *, t, tk, d, scale, lam_init):
    iq = pl.program_id(2)
    lam = _lambda(lq1_ref, lk1_ref, lq2_ref, lk2_ref, lam_init)
    n_full = (iq * t) // tk
    q_chunk = (iq * t + lax.broadcasted_iota(jnp.int32, (t, 1), 0)) // CHUNK
    k_chunk = (n_full * tk + lax.broadcasted_iota(jnp.int32, (1, tk), 1)) // CHUNK
    visible = k_chunk <= q_chunk
    width = LANES if tk % LANES == 0 else tk

    def tile(j):
        return pl.ds(pl.multiple_of(j * tk, tk), tk)

    outs = []
    for m in range(2):
        cols = slice(m * d, (m + 1) * d)
        qm = (q_ref[:, cols] * scale).astype(BF16)

        def scores(j, qm=qm, cols=cols):
            kt = k_ref[tile(j), cols].astype(BF16)
            return lax.dot_general(qm, kt, NT_DIMS, preferred_element_type=F32)

        def pass1(j, mx, scores=scores):
            s = scores(j)
            s_ref[:, tile(j)] = s
            return jnp.maximum(mx, _fold_lanes(s, jnp.maximum))

        mx = _loop_by_two(n_full, pass1, jnp.full((t, width), -jnp.inf, F32))
        s_last = jnp.where(visible, scores(n_full), -jnp.inf)
        s_ref[:, tile(n_full)] = s_last
        mx = jnp.max(jnp.maximum(mx, _fold_lanes(s_last, jnp.maximum)), axis=-1, keepdims=True)
        acc_ref[...] = jnp.zeros(acc_ref.shape, F32)

        def pass2(j, den, mx=mx):
            pr = jnp.exp(s_ref[:, tile(j)] - mx)
            vt = v_ref[tile(j), :].astype(BF16)
            acc_ref[...] += jnp.dot(pr.astype(BF16), vt, preferred_element_type=F32)
            return den + _fold_lanes(pr, jnp.add)

        den = _loop_by_two(n_full + 1, pass2, jnp.zeros((t, width), F32))
        outs.append(acc_ref[...] / jnp.sum(den, axis=-1, keepdims=True))
    o = outs[0] - lam * outs[1]
    o_ref[...] = (_rms(o, sub_ref[...]) * (1.0 - lam_init)).astype(BF16)


def _lam_specs(d):
    fixed = lambda *_: (0, 0)
    return [pl.BlockSpec((1, d), fixed)] * 4 + [pl.BlockSpec((1, 2 * d), fixed)]


def diff_attn_prompt(proj, q_col0, k_col0, v_col0, nb, seq_len, heads, d, lam_vecs, subln, lam_init, *,
                     tq_pref=256):
    hw = 2 * d
    tq = _tile(seq_len, tq_pref, CHUNK)
    nq = seq_len // tq
    tk = 2 * tq if nq % 2 == 0 else tq
    assert q_col0 % hw == 0 and k_col0 % hw == 0 and v_col0 % hw == 0
    qc, kc, vc = q_col0 // hw, k_col0 // hw, v_col0 // hw
    return pl.pallas_call(
        functools.partial(_diff_prompt_kernel, t=tq, tk=tk, d=d, scale=1.0 / math.sqrt(d), lam_init=lam_init),
        grid=(nb, heads, nq),
        in_specs=[pl.BlockSpec((tq, hw), lambda b, h, i: (b * nq + i, qc + h)),
                  pl.BlockSpec((seq_len, hw), lambda b, h, i: (b, kc + h)),
                  pl.BlockSpec((seq_len, hw), lambda b, h, i: (b, vc + h))] + _lam_specs(d),
        out_specs=pl.BlockSpec((tq, hw), lambda b, h, i: (b * nq + i, h)),
        out_shape=jax.ShapeDtypeStruct((proj.shape[0], heads * hw), BF16),
        scratch_shapes=[pltpu.VMEM((tq, seq_len), F32), pltpu.VMEM((tq, hw), F32)],
        compiler_params=_params("parallel", "parallel", "parallel"),
    )(proj, proj, proj, *[v.reshape(1, d) for v in lam_vecs], subln.reshape(1, hw))


def _diff_sample_kernel(q_ref, kn_ref, vn_ref, kp_ref, vp_ref, lq1_ref, lk1_ref, lq2_ref, lk2_ref, sub_ref,
                        o_ref, *, heads, d, past, scale, lam_init):
    lam = _lambda(lq1_ref, lk1_ref, lq2_ref, lk2_ref, lam_init)
    hw = 2 * d
    for h in range(heads):
        vp = vp_ref[:, h * hw:(h + 1) * hw].astype(BF16)
        vn = vn_ref[:, h * hw:(h + 1) * hw].astype(BF16)
        outs = []
        for m in range(2):
            cols = slice(h * hw + m * d, h * hw + (m + 1) * d)
            qm = (q_ref[:, cols] * scale).astype(BF16)
            kp = kp_ref[pl.ds(2 * h + m, past, stride=2 * heads), :].astype(BF16)
            s_p = lax.dot_general(qm, kp, NT_DIMS, preferred_element_type=F32)
            s_n = lax.dot_general(qm, kn_ref[:, cols].astype(BF16), NT_DIMS, preferred_element_type=F32)
            mx = jnp.maximum(jnp.max(s_p, axis=-1, keepdims=True), jnp.max(s_n, axis=-1, keepdims=True))
            p_p = jnp.exp(s_p - mx)
            p_n = jnp.exp(s_n - mx)
            den = jnp.sum(p_p, axis=-1, keepdims=True) + jnp.sum(p_n, axis=-1, keepdims=True)
            acc = jnp.dot(p_p.astype(BF16), vp, preferred_element_type=F32)
            acc += jnp.dot(p_n.astype(BF16), vn, preferred_element_type=F32)
            outs.append(acc / den)
        o = outs[0] - lam * outs[1]
        o_ref[:, h * hw:(h + 1) * hw] = (_rms(o, sub_ref[...]) * (1.0 - lam_init)).astype(BF16)


def diff_attn_sample(proj, q_col0, k_col0, v_col0, row0, nb, seq_len, heads, d, k_cache, v_cache, layer,
                     lam_vecs, subln, lam_init, joint):
    width = heads * 2 * d
    past = v_cache.shape[2]
    assert row0 % seq_len == 0 and q_col0 % width == 0 and k_col0 % width == 0 and v_col0 % width == 0
    rb = row0 // seq_len

    def new_spec(col0):
        return pl.BlockSpec((seq_len, width), lambda b: (rb + b, col0 // width))

    in_specs, operands, aliases = _into(
        joint,
        [new_spec(q_col0), new_spec(k_col0), new_spec(v_col0),
         pl.BlockSpec((None, None, past * heads * 2, d), lambda b: (layer, b, 0, 0)),
         pl.BlockSpec((None, None, past, width), lambda b: (layer, b, 0, 0))] + _lam_specs(d),
        [proj, proj, proj, k_cache, v_cache] + [v.reshape(1, d) for v in lam_vecs] + [subln.reshape(1, 2 * d)])
    body = functools.partial(_diff_sample_kernel, heads=heads, d=d, past=past, scale=1.0 / math.sqrt(d),
                             lam_init=lam_init)
    return pl.pallas_call(
        _drop_ref(body, 10),
        grid=(nb,),
        in_specs=in_specs,
        out_specs=pl.BlockSpec((seq_len, width), lambda b: (rb + b, 0)),
        out_shape=jax.ShapeDtypeStruct(joint.shape, BF16),
        input_output_aliases=aliases,
        compiler_params=_params("parallel"),
    )(*operands)


def _xattn_kernel(q_ref, k_ref, v_ref, o_ref, *, heads, hd, scale):
    for h in range(heads):
        cols = slice(h * hd, (h + 1) * hd)
        qh = (q_ref[:, cols] * scale).astype(BF16)
        s = lax.dot_general(qh, k_ref[:, cols].astype(BF16), NT_DIMS, preferred_element_type=F32)
        pr = jnp.exp(s - jnp.max(s, axis=-1, keepdims=True))
        den = jnp.sum(pr, axis=-1, keepdims=True)
        o = jnp.dot(pr.astype(BF16), v_ref[:, cols].astype(BF16), preferred_element_type=F32)
        o_ref[:, cols] = (o / den).astype(BF16)


def cross_attn(q_all, row0, nb, seq_len, k_arr, v_arr, kv_specs, heads, hd, joint, *, tq_pref=512):
    width = heads * hd
    tq = _tile(seq_len, tq_pref, PACKED_ROWS)
    assert row0 % tq == 0
    blk0, per_seq = row0 // tq, seq_len // tq
    in_specs, operands, aliases = _into(
        joint, [pl.BlockSpec((tq, width), lambda b, t: (blk0 + b * per_seq + t, 0))] + kv_specs,
        [q_all, k_arr, v_arr])
    body = functools.partial(_xattn_kernel, heads=heads, hd=hd, scale=1.0 / math.sqrt(hd))
    return pl.pallas_call(
        _drop_ref(body, 3) if aliases else body,
        grid=(nb, per_seq),
        in_specs=in_specs,
        out_specs=pl.BlockSpec((tq, width), lambda b, t: (blk0 + b * per_seq + t, 0)),
        out_shape=jax.ShapeDtypeStruct((q_all.shape[0], width), BF16),
        input_output_aliases=aliases,
        compiler_params=_params("parallel", "parallel"),
    )(*operands)


def kernel(x_prompt, x_sample, state_ssm, state_ssm_conv, state_pool, cache_diff_k, cache_diff_v, cache_mem_k, cache_mem_v, state_ffn_conv, mem_prompt, norm_mix_pre, norm_mix_post, w_in, ssm_conv_w, ssm_conv_b, ssm_dt_bias, ssm_a_log, ssm_d, ssm_norm, w_br_ssm, pool_w, pool_scale, w_br_pool, diff_lq1, diff_lk1, diff_lq2, diff_lk2, diff_subln, w_br_diff, w_o, norm_xa_pre, norm_xa_post, norm_mem, xa_wq, xa_wk, xa_wv, xa_wo, norm_ffn_pre, norm_ffn_post, ffn_w_up, ffn_conv_w, ffn_conv_b, ffn_w_down):
    bp, lp, dm = x_prompt.shape
    bs, ls, _ = x_sample.shape
    depth = w_in.shape[0]
    tp, ts = bp * lp, bs * ls
    heads, p_dim, n_state = state_ssm.shape[2:]
    d_inner = heads * p_dim
    xbc_w = state_ssm_conv.shape[3]
    groups = (xbc_w - d_inner) // (2 * n_state)
    k_ssm = ssm_conv_w.shape[1]
    pool_hist, pool_width = state_pool.shape[2:]
    past, dheads, _, dh = cache_diff_k.shape[2:]
    diff_w = dheads * 2 * dh
    n_mem, xheads, xhd = cache_mem_k.shape[2:]
    xa_w = xheads * xhd
    d_ff = ffn_w_down.shape[1]
    k_ffn = ffn_conv_w.shape[1]
    ffp = _round_up(d_ff, 1024)
    assert lp >= pool_hist and ls >= pool_hist and min(lp, ls) >= max(k_ssm, k_ffn) - 1

    o_z, o_xbc, o_dt = 0, d_inner, d_inner + xbc_w
    o_u = o_dt + heads
    o_q = o_u + pool_width
    o_g = o_q + 3 * diff_w

    def seg(off, width):
        return w_in[:, :, off:off + width]

    w_main = jnp.concatenate(
        [seg(o_g, 3 * dm), seg(o_xbc, xbc_w), seg(o_q, 3 * diff_w), seg(o_z, d_inner), seg(o_u, pool_width)],
        axis=-1).astype(BF16)
    c_g, c_xbc = 0, 3 * dm
    c_q = c_xbc + xbc_w
    c_k, c_v = c_q + diff_w, c_q + 2 * diff_w
    c_z = c_q + 3 * diff_w
    c_u = c_z + d_inner
    dt_pad = _round_up(heads, LANES)
    w_dt = jnp.pad(seg(o_dt, heads), ((0, 0), (0, 0), (0, dt_pad - heads))).astype(BF16)
    w_bs, w_bp, w_bd = w_br_ssm.astype(BF16), w_br_pool.astype(BF16), w_br_diff.astype(BF16)
    w_out = w_o.astype(BF16)
    pool_wb = pool_w.astype(BF16)
    w_q = xa_wq.astype(BF16)
    w_kv = jnp.concatenate([xa_wk, xa_wv], axis=-1).astype(BF16)
    w_xo = xa_wo.astype(BF16)

    def pad_ff(a):
        pad = [(0, 0)] * (a.ndim - 1) + [(0, ffp - d_ff)]
        return jnp.concatenate([jnp.pad(a[..., :d_ff], pad), jnp.pad(a[..., d_ff:], pad)], axis=-1)

    w_up = cast_pad_halves(ffn_w_up, d_ff, ffp)
    conv_w_ff = pad_ff(ffn_conv_w)
    conv_b_ff = pad_ff(ffn_conv_b)
    ffn_hist = pad_ff(state_ffn_conv)
    w_down = ffn_w_down.astype(BF16)
    if d_ff % LANES:
        w_down = jnp.pad(w_down, ((0, 0), (0, ffp - d_ff), (0, 0)))

    k_cache = cache_diff_k.reshape(depth, bs, past * dheads * 2, dh)
    v_cache = cache_diff_v.reshape(depth, bs, past, diff_w)
    mem_k = cache_mem_k.reshape(depth, bs, n_mem, xa_w)
    mem_v = cache_mem_v.reshape(depth, bs, n_mem, xa_w)
    mem_rows = mem_prompt.reshape(bp * n_mem, dm)

    h = jnp.concatenate([x_prompt.reshape(tp, dm), x_sample.reshape(ts, dm)], axis=0)
    xn = rmsnorm_bf16(h, norm_mix_pre[0])
    zeros_p = lambda *shape: jnp.zeros((bp,) + shape, F32)
    q_ssd = _tile(lp, 128, LANES)
    nsub_ssd = _tile(lp // q_ssd, 32, 1)

    def tails(arr, col0, width, n):
        a_p = jnp.stack([arr[(b + 1) * lp - n:(b + 1) * lp, col0:col0 + width] for b in range(bp)])
        a_s = arr[tp:, col0:col0 + width].reshape(bs, ls, width)[:, ls - n:]
        return a_p, a_s

    def both(arr, col0, width):
        return arr[:tp, col0:col0 + width], arr[tp:, col0:col0 + width]

    outs = [[] for _ in range(14)]
    for l in range(depth):
        lam_init = 0.8 - 0.6 * math.exp(-0.3 * l)
        lam_vecs = (diff_lq1[l], diff_lk1[l], diff_lq2[l], diff_lk2[l])

        proj = mm(xn, w_main, l)
        dt_raw = mm(xn, w_dt, l)[:, :heads]
        y_ssm = y_pool = None
        h_last = []
        for (row0, nb, sl, conv_hist, h0, p_hist, pos_base, q_chunk, nsub) in (
                (0, bp, lp, zeros_p(k_ssm - 1, xbc_w), zeros_p(heads, p_dim, n_state),
                 zeros_p(pool_hist, pool_width), 0, q_ssd, nsub_ssd),
                (tp, bs, ls, state_ssm_conv[l], state_ssm[l], state_pool[l], past, ls, 1)):
            dt_g = dt_raw[row0:row0 + nb * sl].reshape(nb, sl, heads)
            y_ssm, hl = ssd_mix(proj, c_xbc, c_z, row0, nb, sl, conv_hist, ssm_conv_w[l], ssm_conv_b[l], dt_g,
                                ssm_dt_bias[l], ssm_a_log[l], ssm_d[l], ssm_norm[l], h0, y_ssm,
                                groups=groups, n_state=n_state, q=q_chunk, nsub=nsub)
            h_last.append(hl)
            y_pool = pool_mix(proj, c_u, row0, nb, sl, p_hist, pos_base, pool_wb[l], pool_scale[l], y_pool)
        y_diff = diff_attn_prompt(proj, c_q, c_k, c_v, bp, lp, dheads, dh, lam_vecs, diff_subln[l], lam_init)
        y_diff = diff_attn_sample(proj, c_q, c_k, c_v, tp, bs, ls, dheads, dh, k_cache, v_cache, l,
                                  lam_vecs, diff_subln[l], lam_init, y_diff)
        merged = branch_merge(y_ssm, y_pool, y_diff, proj, c_g, w_bs, w_bp, w_bd, l)
        h, xn = post_norm(mm(merged, w_out, l), h, norm_mix_post[l], norm_xa_pre[l])

        mem_kv = mm(rmsnorm_bf16(mem_rows, norm_mem[l]), w_kv, l)
        q_xa = mm(xn, w_q, l)
        kv_p = [pl.BlockSpec((n_mem, xa_w), lambda b, t: (b, 0)), pl.BlockSpec((n_mem, xa_w), lambda b, t: (b, 1))]
        kv_s = [pl.BlockSpec((None, None, n_mem, xa_w), lambda b, t, l=l: (l, b, 0, 0))] * 2
        o_xa = cross_attn(q_xa, 0, bp, lp, mem_kv, mem_kv, kv_p, xheads, xhd, None)
        o_xa = cross_attn(q_xa, tp, bs, ls, mem_k, mem_v, kv_s, xheads, xhd, o_xa)
        h, xn = mm_post_norm(o_xa, w_xo, l, h, norm_xa_post[l], norm_ffn_pre[l])

        hid, tail_g, tail_v = ffn_up_geglu(xn, bp, lp, w_up, l, ffp, zeros_p(k_ffn - 1, 2 * ffp),
                                           conv_w_ff[l], conv_b_ff[l])
        up_s = mm(xn, w_up, l, row0=tp, tm_pref=512)
        hid = conv_geglu(up_s, ffp, bs, ls, ffn_hist[l], conv_w_ff[l], conv_b_ff[l], hid, tp)
        down = mm(hid, w_down, l, tm_pref=512, tn_pref=512)
        if l + 1 < depth:
            h, xn = post_norm(down, h, norm_ffn_post[l], norm_mix_pre[l + 1])
        else:
            h_p = post_norm(down, h, norm_ffn_post[l], None, nrows=tp)
            h_s = post_norm(down, h, norm_ffn_post[l], None, row0=tp)

        conv_p, conv_s = tails(proj, c_xbc, xbc_w, k_ssm - 1)
        pool_p, pool_s = tails(proj, c_u, pool_width, pool_hist)
        kk_p, kk_s = both(proj, c_k, diff_w)
        vv_p, vv_s = both(proj, c_v, diff_w)
        ups = up_s.reshape(bs, ls, 2 * ffp)[:, ls - (k_ffn - 1):]
        ffn_p = jnp.concatenate([tail_g[:, -(k_ffn - 1):, :d_ff], tail_v[:, -(k_ffn - 1):, :d_ff]], axis=-1)
        ffn_s = jnp.concatenate([ups[..., :d_ff], ups[..., ffp:ffp + d_ff]], axis=-1)
        layer_out = (h_last[0], h_last[1], conv_p, conv_s, pool_p, pool_s,
                     kk_p.reshape(bp, lp, dheads, 2, dh), kk_s.reshape(bs, ls, dheads, 2, dh),
                     vv_p.reshape(bp, lp, dheads, 2 * dh), vv_s.reshape(bs, ls, dheads, 2 * dh),
                     mem_kv[:, :xa_w].reshape(bp, n_mem, xheads, xhd),
                     mem_kv[:, xa_w:].reshape(bp, n_mem, xheads, xhd),
                     ffn_p, ffn_s)
        for acc, val in zip(outs, layer_out):
            acc.append(val)

    return (h_p.reshape(bp, lp, dm), h_s.reshape(bs, ls, dm)) + tuple(jnp.stack(o) for o in outs)
```
